```python
import jax, jax.numpy as jnp
from jax import lax
import numpy as np

D_MODEL = 1024
BATCH = 16
SEQ = 2048
DEPTH = 1

MIX_WIDTH = D_MODEL
MOBA_HEAD_DIM = 64
MOBA_WIDTH = MIX_WIDTH // 2
MOBA_HEADS = MOBA_WIDTH // MOBA_HEAD_DIM
MOBA_BLOCK = 256
MOBA_TOPK = 3
MOBA_Q_CHUNK = 64
GDN_HEAD_DIM = 128
GDN_WIDTH = MIX_WIDTH - MOBA_WIDTH
GDN_HEADS = GDN_WIDTH // GDN_HEAD_DIM
GDN_CONV = 4
GDN_CHUNK = 64
N_GROUPS = 4
EXPERTS_PER_GROUP = 8
EXPERT_TOPK = 2
EXPERT_FF = D_MODEL // 4
RMS_EPS = 1e-6
IN_SPLITS = (MOBA_WIDTH, MOBA_WIDTH, MOBA_WIDTH, 3 * GDN_WIDTH, GDN_WIDTH, GDN_HEADS, GDN_HEADS)
IN_PROJ_WIDTH = 3 * MOBA_WIDTH + 4 * GDN_WIDTH + 2 * GDN_HEADS

kernel_name = 'hybrid_moba_gdn_hmoe'


def rmsnorm(x, gain):
    x32 = x.astype(jnp.float32)
    y = x32 * lax.rsqrt(jnp.mean(x32 * x32, axis=-1, keepdims=True) + RMS_EPS)
    return (y * gain.astype(jnp.float32)).astype(x.dtype)


def l2norm(x):
    return x * lax.rsqrt(jnp.sum(x * x, axis=-1, keepdims=True) + 1e-6)


def moba_attention(q, k, v):
    B, S, H, Dh = q.shape
    nb = -(-S // MOBA_BLOCK)
    pad = nb * MOBA_BLOCK - S
    topk = min(MOBA_TOPK, nb)
    scale = Dh ** -0.5
    q = q.transpose(0, 2, 1, 3)
    kp = jnp.pad(k.transpose(0, 2, 1, 3), ((0, 0), (0, 0), (0, pad), (0, 0)))
    vp = jnp.pad(v.transpose(0, 2, 1, 3), ((0, 0), (0, 0), (0, pad), (0, 0)))
    kb = kp.reshape(B, H, nb, MOBA_BLOCK, Dh)
    vb = vp.reshape(B, H, nb, MOBA_BLOCK, Dh)
    k_mean = jnp.mean(kb.astype(jnp.float32), axis=3)
    b_idx = jnp.arange(B)[:, None, None, None]
    h_idx = jnp.arange(H)[None, :, None, None]
    blk_ids = jnp.arange(nb)
    n_chunks = S // MOBA_Q_CHUNK

    def chunk(c):
        start = c * MOBA_Q_CHUNK
        qc = lax.dynamic_slice_in_dim(q, start, MOBA_Q_CHUNK, axis=2)
        qpos = start + jnp.arange(MOBA_Q_CHUNK)
        own = start // MOBA_BLOCK
        gate = jnp.einsum('bhqd,bhnd->bhqn', qc.astype(jnp.float32), k_mean)
        gate = jnp.where(blk_ids < own, gate, -jnp.inf)
        _, gidx = lax.top_k(gate, topk)
        sel_ok = jnp.arange(topk) < own
        k_sel = kb[b_idx, h_idx, gidx]
        v_sel = vb[b_idx, h_idx, gidx]
        s_sel = jnp.einsum('bhqd,bhqkld->bhqkl', qc, k_sel).astype(jnp.float32) * scale
        s_sel = jnp.where(sel_ok[:, None], s_sel, -jnp.inf)
        k_own = lax.dynamic_slice_in_dim(kb, own, 1, axis=2)[:, :, 0]
        v_own = lax.dynamic_slice_in_dim(vb, own, 1, axis=2)[:, :, 0]
        s_own = jnp.einsum('bhqd,bhld->bhql', qc, k_own).astype(jnp.float32) * scale
        kpos = own * MOBA_BLOCK + jnp.arange(MOBA_BLOCK)
        s_own = jnp.where(kpos[None, :] <= qpos[:, None], s_own, -jnp.inf)
        s = jnp.concatenate([s_sel.reshape(B, H, MOBA_Q_CHUNK, topk * MOBA_BLOCK), s_own], axis=-1)
        p = jax.nn.softmax(s, axis=-1).astype(v.dtype)
        p_sel = p[..., :topk * MOBA_BLOCK].reshape(B, H, MOBA_Q_CHUNK, topk, MOBA_BLOCK)
        p_own = p[..., topk * MOBA_BLOCK:]
        return (jnp.einsum('bhqkl,bhqkld->bhqd', p_sel, v_sel)
                + jnp.einsum('bhql,bhld->bhqd', p_own, v_own))

    out = lax.map(chunk, jnp.arange(n_chunks))
    return out.transpose(1, 0, 3, 2, 4).reshape(B, S, H, Dh)


def causal_depthwise_conv(x, w):
    C = x.shape[-1]
    return lax.conv_general_dilated(
        x, w[:, None, :].astype(x.dtype), window_strides=(1,),
        padding=((GDN_CONV - 1, 0),), dimension_numbers=('NWC', 'WIO', 'NWC'),
        feature_group_count=C)


def gated_delta_rule(q, k, v, g, beta):
    B, S, H, Dk = q.shape
    Dv = v.shape[-1]
    C = GDN_CHUNK
    N = S // C

    def to_chunks(t):
        return t.reshape(B, N, C, H, *t.shape[3:]).swapaxes(2, 3)

    q, k, v, g, beta = (to_chunks(t) for t in (q, k, v, g, beta))
    G = jnp.cumsum(g, axis=-1)
    incl = jnp.tril(jnp.ones((C, C), dtype=bool))
    strict = jnp.tril(jnp.ones((C, C), dtype=bool), k=-1)
    decay = jnp.exp(jnp.where(incl, G[..., :, None] - G[..., None, :], -jnp.inf))
    kk = jnp.einsum('bnhid,bnhjd->bnhij', k, k)
    A = jnp.where(strict, beta[..., :, None] * kk * decay, 0.0)
    eye = jnp.eye(C, dtype=A.dtype)
    gam = jnp.exp(G)
    rhs = jnp.concatenate([beta[..., None] * v, (beta * gam)[..., None] * k], axis=-1)
    sol = lax.linalg.triangular_solve(eye + A, rhs, left_side=True, lower=True, unit_diagonal=True)
    u_v, w = sol[..., :Dv], sol[..., Dv:]
    qk = jnp.einsum('bnhid,bnhjd->bnhij', q, k) * decay
    q_dec = q * gam[..., None]
    k_dec = k * jnp.exp(G[..., -1:] - G)[..., None]
    g_last = jnp.exp(G[..., -1])

    def step(state, xs):
        u_v_c, w_c, qk_c, q_dec_c, k_dec_c, g_last_c = xs
        u = u_v_c - jnp.einsum('bhck,bhkv->bhcv', w_c, state)
        o = jnp.einsum('bhck,bhkv->bhcv', q_dec_c, state) + jnp.einsum('bhij,bhjv->bhiv', qk_c, u)
        new_state = g_last_c[..., None, None] * state + jnp.einsum('bhck,bhcv->bhkv', k_dec_c, u)
        return new_state, o

    xs = tuple(t.swapaxes(0, 1) for t in (u_v, w, qk, q_dec, k_dec, g_last))
    state0 = jnp.zeros((B, H, Dk, Dv), dtype=q.dtype)
    _, o = lax.scan(step, state0, xs)
    return o.transpose(1, 0, 3, 2, 4).reshape(B, S, H, Dv)


def hierarchical_moe(h, w_group, w_router, w_gate, w_up, w_down):
    T = h.shape[0]
    group_logits = jnp.einsum('td,dg->tg', h, w_group).astype(jnp.float32)
    group_prob = jax.nn.softmax(group_logits, axis=-1)
    g_sel = jnp.argmax(group_logits, axis=-1)
    p_group = jnp.take_along_axis(group_prob, g_sel[:, None], axis=-1)
    expert_logits = jnp.einsum('td,de->te', h, w_router).astype(jnp.float32)
    expert_logits = expert_logits.reshape(T, N_GROUPS, EXPERTS_PER_GROUP)
    in_group = jnp.take_along_axis(expert_logits, g_sel[:, None, None], axis=1)[:, 0]
    in_prob = jax.nn.softmax(in_group, axis=-1)
    top_p, top_i = lax.top_k(in_prob, EXPERT_TOPK)
    top_p = top_p / jnp.sum(top_p, axis=-1, keepdims=True)
    within = jnp.sum(jax.nn.one_hot(top_i, EXPERTS_PER_GROUP, dtype=jnp.float32) * top_p[..., None], axis=1)
    combine = (jax.nn.one_hot(g_sel, N_GROUPS, dtype=jnp.float32)[:, :, None]
               * p_group[:, :, None] * within[:, None, :]).astype(h.dtype)
    y = jnp.zeros_like(h)
    for grp in range(N_GROUPS):
        hid = (jax.nn.silu(jnp.einsum('td,edf->tef', h, w_gate[grp]))
               * jnp.einsum('td,edf->tef', h, w_up[grp])) * combine[:, grp, :, None]
        y = y + jnp.einsum('tef,efd->td', hid, w_down[grp])
    return y


def setup_inputs(seed: int = 0) -> dict:
    key = jax.random.key(seed)
    ks = jax.random.split(key, 16)
    f32 = jnp.float32
    D = D_MODEL
    x = jax.random.normal(ks[0], (BATCH, SEQ, D), f32)
    attn_norm = 1.0 + 0.02 * jax.random.normal(ks[1], (DEPTH, D), f32)
    w_in = jax.random.normal(ks[2], (DEPTH, D, IN_PROJ_WIDTH), f32) * D ** -0.5
    conv_w = jax.random.normal(ks[3], (DEPTH, GDN_CONV, 3 * GDN_WIDTH), f32) * GDN_CONV ** -0.5
    A_log = jnp.log(jax.random.uniform(ks[4], (DEPTH, GDN_HEADS), f32, 1.0, 16.0))
    dt = jnp.exp(jax.random.uniform(ks[5], (DEPTH, GDN_HEADS), f32, np.log(1e-3), np.log(1e-1)))
    dt_bias = dt + jnp.log(-jnp.expm1(-dt))
    gdn_norm = 1.0 + 0.02 * jax.random.normal(ks[6], (DEPTH, GDN_HEAD_DIM), f32)
    w_out = jax.random.normal(ks[7], (DEPTH, MIX_WIDTH, D), f32) * MIX_WIDTH ** -0.5
    ffn_norm = 1.0 + 0.02 * jax.random.normal(ks[8], (DEPTH, D), f32)
    w_group = jax.random.normal(ks[9], (DEPTH, D, N_GROUPS), f32) * D ** -0.5
    w_router = jax.random.normal(ks[10], (DEPTH, D, N_GROUPS * EXPERTS_PER_GROUP), f32) * D ** -0.5
    w_gate = jax.random.normal(ks[11], (DEPTH, N_GROUPS, EXPERTS_PER_GROUP, D, EXPERT_FF), f32) * D ** -0.5
    w_up = jax.random.normal(ks[12], (DEPTH, N_GROUPS, EXPERTS_PER_GROUP, D, EXPERT_FF), f32) * D ** -0.5
    w_down = jax.random.normal(ks[13], (DEPTH, N_GROUPS, EXPERTS_PER_GROUP, EXPERT_FF, D), f32) * EXPERT_FF ** -0.5
    final_norm = 1.0 + 0.02 * jax.random.normal(ks[14], (D,), f32)
    return {'x': x, 'attn_norm': attn_norm, 'w_in': w_in, 'conv_w': conv_w, 'A_log': A_log,
            'dt_bias': dt_bias, 'gdn_norm': gdn_norm, 'w_out': w_out, 'ffn_norm': ffn_norm,
            'w_group': w_group, 'w_router': w_router, 'w_gate': w_gate, 'w_up': w_up,
            'w_down': w_down, 'final_norm': final_norm}


def reference(x, attn_norm, w_in, conv_w, A_log, dt_bias, gdn_norm, w_out, ffn_norm,
              w_group, w_router, w_gate, w_up, w_down, final_norm):
    B, S, D = x.shape
    offsets = np.cumsum(IN_SPLITS)[:-1].tolist()
    f32 = jnp.float32
    for l in range(DEPTH):
        h = rmsnorm(x, attn_norm[l])
        proj = jnp.einsum('bsd,de->bse', h, w_in[l])
        mq, mk, mv, gqkv, gz, ga, gb = jnp.split(proj, offsets, axis=-1)
        to_a = lambda t: t.reshape(B, S, MOBA_HEADS, MOBA_HEAD_DIM)
        y_a = moba_attention(to_a(mq), to_a(mk), to_a(mv)).reshape(B, S, MOBA_WIDTH)
        gqkv = jax.nn.silu(causal_depthwise_conv(gqkv, conv_w[l]))
        gq, gk, gv = jnp.split(gqkv, 3, axis=-1)
        to_b = lambda t: t.reshape(B, S, GDN_HEADS, GDN_HEAD_DIM).astype(f32)
        gq = l2norm(to_b(gq)) * GDN_HEAD_DIM ** -0.5
        gk = l2norm(to_b(gk))
        log_decay = -jnp.exp(A_log[l].astype(f32)) * jax.nn.softplus(ga.astype(f32) + dt_bias[l].astype(f32))
        beta = jax.nn.sigmoid(gb.astype(f32))
        o_b = gated_delta_rule(gq, gk, to_b(gv), log_decay, beta)
        o_b = rmsnorm(o_b, gdn_norm[l]) * jax.nn.silu(to_b(gz))
        y_b = o_b.reshape(B, S, GDN_WIDTH).astype(x.dtype)
        mix = jnp.concatenate([y_a, y_b], axis=-1)
        x = x + jnp.einsum('bse,ed->bsd', mix, w_out[l])
        h = rmsnorm(x, ffn_norm[l]).reshape(B * S, D)
        y = hierarchical_moe(h, w_group[l], w_router[l], w_gate[l], w_up[l], w_down[l])
        x = x + y.reshape(B, S, D)
    return rmsnorm(x, final_norm)
```

```python
import functools

import jax
import jax.numpy as jnp
from jax import lax
from jax.experimental import pallas as pl
from jax.experimental.pallas import tpu as pltpu

F32 = jnp.float32
BF16 = jnp.bfloat16

LANES = 128
MOBA_HEAD_DIM = 64
MOBA_BLOCK = 256
MOBA_TOPK = 3
GDN_HEAD_DIM = 128
GDN_CONV = 4
GDN_CHUNK = 256
N_GROUPS = 4
EXPERTS_PER_GROUP = 8
N_EXPERTS = N_GROUPS * EXPERTS_PER_GROUP
RMS_EPS = 1e-6
NEG = -1e30
GATE_ROWS = 16
VMEM_LIMIT = 48 * 1024 * 1024


def _dot(a, b):
    return jnp.dot(a, b, preferred_element_type=F32)


def _dot_nt(a, b):
    return lax.dot_general(a, b, (((1,), (1,)), ((), ())), preferred_element_type=F32)


def _sigmoid(x):
    return 1.0 / (1.0 + jnp.exp(-x))


def _silu(x):
    return x * _sigmoid(x)


def _rms(x, gain):
    return x * lax.rsqrt(jnp.mean(x * x, axis=-1, keepdims=True) + RMS_EPS) * gain


def _inproj_kernel(x_ref, gain_ref, wm_ref, wg_ref, wab_ref, om_ref, og_ref, oab_ref):
    h = _rms(x_ref[...], gain_ref[...]).astype(BF16)
    om_ref[...] = _dot(h, wm_ref[...]).astype(BF16)
    og_ref[...] = _dot(h, wg_ref[...])
    oab_ref[...] = _dot(h, wab_ref[...])


def _inproj(x2d, gain, wm, wg, wab, tm):
    T, D = x2d.shape
    nm, ng = wm.shape[1], wg.shape[1]
    return pl.pallas_call(
        _inproj_kernel,
        grid=(T // tm,),
        in_specs=[
            pl.BlockSpec((tm, D), lambda i: (i, 0)),
            pl.BlockSpec((1, D), lambda i: (0, 0)),
            pl.BlockSpec((D, nm), lambda i: (0, 0)),
            pl.BlockSpec((D, ng), lambda i: (0, 0)),
            pl.BlockSpec((D, LANES), lambda i: (0, 0)),
        ],
        out_specs=[
            pl.BlockSpec((tm, nm), lambda i: (i, 0)),
            pl.BlockSpec((tm, ng), lambda i: (i, 0)),
            pl.BlockSpec((tm, LANES), lambda i: (i, 0)),
        ],
        out_shape=[
            jax.ShapeDtypeStruct((T, nm), BF16),
            jax.ShapeDtypeStruct((T, ng), F32),
            jax.ShapeDtypeStruct((T, LANES), F32),
        ],
        compiler_params=pltpu.CompilerParams(
            dimension_semantics=("arbitrary",), vmem_limit_bytes=VMEM_LIMIT),
        name="inproj",
    )(x2d, gain, wm, wg, wab)


def _moba_kernel(q_ref, k_ref, v_ref, o_ref, *, S):
    nb = S // MOBA_BLOCK
    L = MOBA_BLOCK
    q = q_ref[...]
    k = k_ref[...]
    v = v_ref[...]
    lane = lax.broadcasted_iota(jnp.int32, (1, LANES), 1)
    key_blk = lax.broadcasted_iota(jnp.int32, (S, LANES), 0) // L
    lane_s = lax.broadcasted_iota(jnp.int32, (S, LANES), 1)
    qblk = lax.broadcasted_iota(jnp.int32, (GATE_ROWS, S), 1) // L
    cidx = lax.broadcasted_iota(jnp.int32, (GATE_ROWS, S), 0)
    r_i = lax.broadcasted_iota(jnp.int32, (L, L), 0)
    c_i = lax.broadcasted_iota(jnp.int32, (L, L), 1)
    causal = c_i <= r_i

    km = jnp.mean(k.astype(F32).reshape(nb, L, LANES), axis=1)
    km = jnp.concatenate([km, jnp.zeros((GATE_ROWS - nb, LANES), F32)], axis=0).astype(BF16)

    qa, ka, hms = [], [], []
    for h in range(2):
        hm = (lane >= MOBA_HEAD_DIM * h) & (lane < MOBA_HEAD_DIM * (h + 1))
        off = MOBA_HEAD_DIM * (1 - h)
        qh = jnp.where(hm, q, jnp.zeros_like(q))
        gate = _dot_nt(km, qh)
        valid = cidx < qblk
        gate = jnp.where(valid, gate, -jnp.inf)
        rank = jnp.zeros((GATE_ROWS, S), F32)
        for i in range(nb):
            gi = gate[i:i + 1, :]
            beats = (gi > gate) | ((gi == gate) & (i < cidx))
            rank = rank + beats.astype(F32)
        sel = valid & (rank < float(MOBA_TOPK))
        pen = jnp.where(sel | (cidx >= qblk), 0.0, NEG)
        pads = [jnp.zeros((r, S), F32) for r in (off, LANES - off - GATE_ROWS)]
        pen = jnp.concatenate(([pads[0]] if off else []) + [pen, pads[1]], axis=0)
        pen_q = pen.T.astype(BF16)
        scale = jnp.asarray(MOBA_HEAD_DIM ** -0.5, BF16)
        qa.append(jnp.where(hm, q * scale, pen_q))
        onehot = ((lane_s - off) == key_blk).astype(BF16)
        ka.append(jnp.where(hm, k, onehot))
        hms.append(hm)

    for n in range(nb):
        outs = []
        for h in range(2):
            qn = qa[h][n * L:(n + 1) * L]
            s_own = jnp.where(causal, _dot_nt(qn, ka[h][n * L:(n + 1) * L]), NEG)
            m = jnp.max(s_own, axis=-1, keepdims=True)
            if n > 0:
                s_past = _dot_nt(qn, ka[h][:n * L])
                m = jnp.maximum(m, jnp.max(s_past, axis=-1, keepdims=True))
                p_past = jnp.exp(s_past - m)
                l = jnp.sum(p_past, axis=-1, keepdims=True)
                acc = _dot(p_past.astype(BF16), v[:n * L])
            p_own = jnp.exp(s_own - m)
            if n > 0:
                l = l + jnp.sum(p_own, axis=-1, keepdims=True)
                acc = acc + _dot(p_own.astype(BF16), v[n * L:(n + 1) * L])
            else:
                l = jnp.sum(p_own, axis=-1, keepdims=True)
                acc = _dot(p_own.astype(BF16), v[n * L:(n + 1) * L])
            outs.append(acc / l)
        o_ref[n * L:(n + 1) * L, :] = jnp.where(hms[0], outs[0], outs[1]).astype(BF16)


def _moba(qkv, B, S):
    W = qkv.shape[-1] // 3
    npair = W // LANES
    return pl.pallas_call(
        functools.partial(_moba_kernel, S=S),
        grid=(B, npair),
        in_specs=[
            pl.BlockSpec((None, S, LANES), lambda b, j: (b, 0, j)),
            pl.BlockSpec((None, S, LANES), lambda b, j: (b, 0, npair + j)),
            pl.BlockSpec((None, S, LANES), lambda b, j: (b, 0, 2 * npair + j)),
        ],
        out_specs=pl.BlockSpec((None, S, LANES), lambda b, j: (b, 0, j)),
        out_shape=jax.ShapeDtypeStruct((B, S, W), BF16),
        compiler_params=pltpu.CompilerParams(
            dimension_semantics=("arbitrary", "arbitrary"), vmem_limit_bytes=VMEM_LIMIT),
        name="moba",
    )(qkv, qkv, qkv)


def _gdn_kernel(gd_ref, halo_ref, ab_ref, cw_ref, prm_ref, gn_ref, o_ref, state_ref, *, heads):
    C = GDN_CHUNK
    Dh = GDN_HEAD_DIM
    W = heads * Dh
    c = pl.program_id(1)

    @pl.when(c == 0)
    def _():
        state_ref[...] = jnp.zeros_like(state_ref)

    gd = gd_ref[...]
    halo = jnp.where(c > 0, halo_ref[...], 0.0)
    ext = jnp.concatenate([halo, gd[:, :3 * W]], axis=0)
    cw = cw_ref[...]
    conv = cw[GDN_CONV - 1:GDN_CONV] * ext[8:]
    for d in range(1, GDN_CONV):
        conv = conv + cw[GDN_CONV - 1 - d:GDN_CONV - d] * pltpu.roll(ext, d, 0)[8:]
    qkv = _silu(conv)

    ab = ab_ref[...]
    prm = prm_ref[...]
    sp_in = ab + prm[1:2]
    softplus = jnp.maximum(sp_in, 0.0) + jnp.log1p(jnp.exp(-jnp.abs(sp_in)))
    g = -jnp.exp(prm[0:1]) * softplus
    beta = _sigmoid(ab)
    rows = lax.broadcasted_iota(jnp.int32, (C, LANES), 0)
    G = g
    d = 1
    while d < C:
        G = G + jnp.where(rows >= d, pltpu.roll(G, d, 0), 0.0)
        d *= 2
    GT = G.T
    G_last = G[C - 1:C, :]

    r_i = lax.broadcasted_iota(jnp.int32, (C, C), 0)
    c_i = lax.broadcasted_iota(jnp.int32, (C, C), 1)
    incl = c_i <= r_i
    strict = c_i < r_i
    rc = r_i ^ c_i
    level = jnp.full((C, C), -1, jnp.int32)
    for b in range(C.bit_length() - 1):
        level = level + (rc >= (1 << b)).astype(jnp.int32)
    gn = gn_ref[...]

    for h in range(heads):
        q = qkv[:, h * Dh:(h + 1) * Dh]
        k = qkv[:, W + h * Dh:W + (h + 1) * Dh]
        v = qkv[:, 2 * W + h * Dh:2 * W + (h + 1) * Dh]
        z = gd[:, 3 * W + h * Dh:3 * W + (h + 1) * Dh]
        Gc = G[:, h:h + 1]
        Gr = GT[h:h + 1, :]
        bc = beta[:, heads + h:heads + h + 1]
        gl = G_last[:, h:h + 1]

        q = q * lax.rsqrt(jnp.sum(q * q, axis=-1, keepdims=True) + 1e-6) * (Dh ** -0.5)
        k = k * lax.rsqrt(jnp.sum(k * k, axis=-1, keepdims=True) + 1e-6)
        kb = k.astype(BF16)
        decay = jnp.exp(jnp.where(incl, Gc - Gr, -jnp.inf))
        A = jnp.where(strict, bc * _dot_nt(kb, kb) * decay, 0.0)
        qk = _dot_nt(q.astype(BF16), kb) * decay
        gam = jnp.exp(Gc)
        X = jnp.concatenate([bc * v, (bc * gam) * k], axis=1)
        N = -jnp.where(level == 0, A, 0.0)
        for lv in range(1, C.bit_length() - 1):
            E = jnp.where(level == lv, A, 0.0)
            Nb = N.astype(BF16)
            F = E + _dot(E.astype(BF16), Nb)
            N = N - F - _dot(Nb, F.astype(BF16))
        X = X + _dot(N.astype(BF16), X.astype(BF16))
        u_v = X[:, :Dh]
        w = X[:, Dh:]

        S0 = state_ref[h]
        Sb = S0.astype(BF16)
        u = u_v - _dot(w.astype(BF16), Sb)
        ub = u.astype(BF16)
        o = _dot((q * gam).astype(BF16), Sb) + _dot(qk.astype(BF16), ub)
        k_dec = k * jnp.exp(gl - Gc)
        state_ref[h] = jnp.exp(gl) * S0 + _dot(k_dec.T.astype(BF16), ub)

        y = _rms(o, gn) * _silu(z)
        o_ref[:, h * Dh:(h + 1) * Dh] = y.astype(BF16)


def _gdn(gd, ab, conv_w, prm, gdn_norm, B, S, heads):
    C = GDN_CHUNK
    W = heads * GDN_HEAD_DIM
    hb = C // 8
    return pl.pallas_call(
        functools.partial(_gdn_kernel, heads=heads),
        grid=(B, S // C),
        in_specs=[
            pl.BlockSpec((None, C, 4 * W), lambda b, c: (b, c, 0)),
            pl.BlockSpec((None, 8, 3 * W), lambda b, c: (b, jnp.maximum(c * hb - 1, 0), 0)),
            pl.BlockSpec((None, C, LANES), lambda b, c: (b, c, 0)),
            pl.BlockSpec((GDN_CONV, 3 * W), lambda b, c: (0, 0)),
            pl.BlockSpec((8, LANES), lambda b, c: (0, 0)),
            pl.BlockSpec((1, GDN_HEAD_DIM), lambda b, c: (0, 0)),
        ],
        out_specs=pl.BlockSpec((None, C, W), lambda b, c: (b, c, 0)),
        out_shape=jax.ShapeDtypeStruct((B, S, W), BF16),
        scratch_shapes=[pltpu.VMEM((heads, GDN_HEAD_DIM, GDN_HEAD_DIM), F32)],
        compiler_params=pltpu.CompilerParams(
            dimension_semantics=("arbitrary", "arbitrary"), vmem_limit_bytes=VMEM_LIMIT),
        name="gdn",
    )(gd, gd, ab, conv_w, prm, gdn_norm)


def _outproj_kernel(ya_ref, yb_ref, x_ref, woa_ref, wob_ref, fg_ref, wr_ref,
                    x1_ref, h_ref, comb_ref):
    x1 = x_ref[...] + _dot(ya_ref[...], woa_ref[...]) + _dot(yb_ref[...], wob_ref[...])
    x1_ref[...] = x1
    h = _rms(x1, fg_ref[...]).astype(BF16)
    h_ref[...] = h
    logits = _dot(h, wr_ref[...])
    tm = logits.shape[0]
    lane = lax.broadcasted_iota(jnp.int32, (tm, LANES), 1).astype(F32)
    big = float(LANES)

    is_g = (lane >= N_EXPERTS) & (lane < N_EXPERTS + N_GROUPS)
    lg = jnp.where(is_g, logits, -jnp.inf)
    mg = jnp.max(lg, axis=-1, keepdims=True)
    g_sel = jnp.min(jnp.where(lg == mg, lane, big), axis=-1, keepdims=True) - N_EXPERTS
    p_group = 1.0 / jnp.sum(jnp.exp(lg - mg), axis=-1, keepdims=True)

    lo = g_sel * EXPERTS_PER_GROUP
    in_grp = (lane >= lo) & (lane < lo + EXPERTS_PER_GROUP)
    le = jnp.where(in_grp, logits, -jnp.inf)
    m1 = jnp.max(le, axis=-1, keepdims=True)
    i1 = jnp.min(jnp.where(le == m1, lane, big), axis=-1, keepdims=True)
    le2 = jnp.where(lane == i1, -jnp.inf, le)
    m2 = jnp.max(le2, axis=-1, keepdims=True)
    i2 = jnp.min(jnp.where(le2 == m2, lane, big), axis=-1, keepdims=True)
    se = jnp.sum(jnp.exp(le - m1), axis=-1, keepdims=True)
    p1 = 1.0 / se
    p2 = jnp.exp(m2 - m1) / se
    tot = p1 + p2
    comb_ref[...] = (jnp.where(lane == i1, p_group * (p1 / tot), 0.0)
                     + jnp.where(lane == i2, p_group * (p2 / tot), 0.0))


def _outproj(ya, yb, x2d, woa, wob, fgain, wr, tm):
    T, D = x2d.shape
    Wa, Wb = ya.shape[1], yb.shape[1]
    return pl.pallas_call(
        _outproj_kernel,
        grid=(T // tm,),
        in_specs=[
            pl.BlockSpec((tm, Wa), lambda i: (i, 0)),
            pl.BlockSpec((tm, Wb), lambda i: (i, 0)),
            pl.BlockSpec((tm, D), lambda i: (i, 0)),
            pl.BlockSpec((Wa, D), lambda i: (0, 0)),
            pl.BlockSpec((Wb, D), lambda i: (0, 0)),
            pl.BlockSpec((1, D), lambda i: (0, 0)),
            pl.BlockSpec((D, LANES), lambda i: (0, 0)),
        ],
        out_specs=[
            pl.BlockSpec((tm, D), lambda i: (i, 0)),
            pl.BlockSpec((tm, D), lambda i: (i, 0)),
            pl.BlockSpec((tm, LANES), lambda i: (i, 0)),
        ],
        out_shape=[
            jax.ShapeDtypeStruct((T, D), F32),
            jax.ShapeDtypeStruct((T, D), BF16),
            jax.ShapeDtypeStruct((T, LANES), F32),
        ],
        compiler_params=pltpu.CompilerParams(
            dimension_semantics=("arbitrary",), vmem_limit_bytes=VMEM_LIMIT),
        name="outproj_router",
    )(ya, yb, x2d, woa, wob, fgain, wr)


def _moe_kernel(h_ref, comb_ref, x1_ref, wg_ref, wu_ref, wd_ref, fn_ref, o_ref, acc_ref):
    e = pl.program_id(1)

    @pl.when(e == 0)
    def _():
        acc_ref[...] = jnp.zeros_like(acc_ref)

    h = h_ref[...]
    a = _dot(h, wg_ref[...])
    b = _dot(h, wu_ref[...])
    lane = lax.broadcasted_iota(jnp.int32, comb_ref.shape, 1)
    cw = jnp.sum(jnp.where(lane == e, comb_ref[...], 0.0), axis=-1, keepdims=True)
    hid = (_silu(a) * b) * cw
    acc_ref[...] += _dot(hid.astype(BF16), wd_ref[...])

    @pl.when(e == pl.num_programs(1) - 1)
    def _():
        o_ref[...] = _rms(x1_ref[...] + acc_ref[...], fn_ref[...])


def _moe(h, comb, x1, wg, wu, wd, fnorm, tm):
    T, D = x1.shape
    E, _, F = wg.shape
    return pl.pallas_call(
        _moe_kernel,
        grid=(T // tm, E),
        in_specs=[
            pl.BlockSpec((tm, D), lambda i, e: (i, 0)),
            pl.BlockSpec((tm, LANES), lambda i, e: (i, 0)),
            pl.BlockSpec((tm, D), lambda i, e: (i, 0)),
            pl.BlockSpec((None, D, F), lambda i, e: (e, 0, 0)),
            pl.BlockSpec((None, D, F), lambda i, e: (e, 0, 0)),
            pl.BlockSpec((None, F, D), lambda i, e: (e, 0, 0)),
            pl.BlockSpec((1, D), lambda i, e: (0, 0)),
        ],
        out_specs=pl.BlockSpec((tm, D), lambda i, e: (i, 0)),
        out_shape=jax.ShapeDtypeStruct((T, D), F32),
        scratch_shapes=[pltpu.VMEM((tm, D), F32)],
        compiler_params=pltpu.CompilerParams(
            dimension_semantics=("arbitrary", "arbitrary"), vmem_limit_bytes=VMEM_LIMIT),
        name="moe",
    )(h, comb, x1, wg, wu, wd, fnorm)


def _pick_tile(T, pref):
    tm = min(pref, T)
    while T % tm:
        tm //= 2
    return tm


def kernel(x, attn_norm, w_in, conv_w, A_log, dt_bias, gdn_norm, w_out, ffn_norm,
           w_group, w_router, w_gate, w_up, w_down, final_norm):
    B, S, D = x.shape
    T = B * S
    depth = w_in.shape[0]
    heads = A_log.shape[1]
    Wb = heads * GDN_HEAD_DIM
    Wa = w_out.shape[1] - Wb
    assert S % MOBA_BLOCK == 0 and S // MOBA_BLOCK <= GATE_ROWS and S % GDN_CHUNK == 0
    assert Wa % LANES == 0 and 2 * heads <= LANES

    x2d = x.reshape(T, D)
    for l in range(depth):
        wl = w_in[l]
        wm = wl[:, :3 * Wa].astype(BF16)
        wg = wl[:, 3 * Wa:3 * Wa + 4 * Wb].astype(BF16)
        wab = jnp.pad(wl[:, 3 * Wa + 4 * Wb:], ((0, 0), (0, LANES - 2 * heads))).astype(BF16)
        qkv_a, gd, ab = _inproj(x2d, attn_norm[l][None, :], wm, wg, wab, _pick_tile(T, 512))

        y_a = _moba(qkv_a.reshape(B, S, 3 * Wa), B, S)

        prm = jnp.zeros((8, LANES), F32)
        prm = prm.at[0, :heads].set(A_log[l].astype(F32)).at[1, :heads].set(dt_bias[l].astype(F32))
        y_b = _gdn(gd.reshape(B, S, 4 * Wb), ab.reshape(B, S, LANES), conv_w[l].astype(F32), prm,
                   gdn_norm[l][None, :].astype(F32), B, S, heads)

        wo = w_out[l].astype(BF16)
        wr = jnp.concatenate([w_router[l], w_group[l]], axis=1)
        wr = jnp.pad(wr, ((0, 0), (0, LANES - wr.shape[1]))).astype(BF16)
        x1, h2, comb = _outproj(y_a.reshape(T, Wa), y_b.reshape(T, Wb), x2d, wo[:Wa], wo[Wa:],
                                ffn_norm[l][None, :], wr, _pick_tile(T, 512))

        F = w_gate.shape[-1]
        wge = w_gate[l].reshape(N_EXPERTS, D, F).astype(BF16)
        wue = w_up[l].reshape(N_EXPERTS, D, F).astype(BF16)
        wde = w_down[l].reshape(N_EXPERTS, F, D).astype(BF16)
        fn = final_norm[None, :] if l == depth - 1 else None
        assert fn is not None, "only the last layer may follow; depth > 1 needs an un-normed variant"
        x2d = _moe(h2, comb, x1, wge, wue, wde, fn, _pick_tile(T, 1024))
    return x2d.reshape(B, S, D)
```

```python
import functools

import jax
import jax.numpy as jnp
from jax import lax
from jax.experimental import pallas as pl
from jax.experimental.pallas import tpu as pltpu

F32 = jnp.float32
BF16 = jnp.bfloat16

LANES = 128
MOBA_HEAD_DIM = 64
MOBA_BLOCK = 256
MOBA_TOPK = 3
GDN_HEAD_DIM = 128
GDN_CONV = 4
GDN_CHUNK = 256
N_GROUPS = 4
EXPERTS_PER_GROUP = 8
N_EXPERTS = N_GROUPS * EXPERTS_PER_GROUP
RMS_EPS = 1e-6
NEG = -1e30
GATE_ROWS = 16
MOE_ROW_TILE = 512
VMEM_LIMIT = 48 * 1024 * 1024


def _dot(a, b):
    return jnp.dot(a, b, preferred_element_type=F32)


def _dot_nt(a, b):
    return lax.dot_general(a, b, (((1,), (1,)), ((), ())), preferred_element_type=F32)


def _sigmoid(x):
    return 1.0 / (1.0 + jnp.exp(-x))


def _silu(x):
    return x * _sigmoid(x)


def _rms(x, gain):
    return x * lax.rsqrt(jnp.mean(x * x, axis=-1, keepdims=True) + RMS_EPS) * gain


def _inproj_kernel(x_ref, gain_ref, wm_ref, wg_ref, wab_ref, om_ref, og_ref, oab_ref):
    h = _rms(x_ref[...], gain_ref[...]).astype(BF16)
    om_ref[...] = _dot(h, wm_ref[...]).astype(BF16)
    og_ref[...] = _dot(h, wg_ref[...])
    oab_ref[...] = _dot(h, wab_ref[...])


def _inproj(x2d, gain, wm, wg, wab, tm):
    T, D = x2d.shape
    nm, ng = wm.shape[1], wg.shape[1]
    return pl.pallas_call(
        _inproj_kernel,
        grid=(T // tm,),
        in_specs=[
            pl.BlockSpec((tm, D), lambda i: (i, 0)),
            pl.BlockSpec((1, D), lambda i: (0, 0)),
            pl.BlockSpec((D, nm), lambda i: (0, 0)),
            pl.BlockSpec((D, ng), lambda i: (0, 0)),
            pl.BlockSpec((D, LANES), lambda i: (0, 0)),
        ],
        out_specs=[
            pl.BlockSpec((tm, nm), lambda i: (i, 0)),
            pl.BlockSpec((tm, ng), lambda i: (i, 0)),
            pl.BlockSpec((tm, LANES), lambda i: (i, 0)),
        ],
        out_shape=[
            jax.ShapeDtypeStruct((T, nm), BF16),
            jax.ShapeDtypeStruct((T, ng), F32),
            jax.ShapeDtypeStruct((T, LANES), F32),
        ],
        compiler_params=pltpu.CompilerParams(
            dimension_semantics=("arbitrary",), vmem_limit_bytes=VMEM_LIMIT),
        name="inproj",
    )(x2d, gain, wm, wg, wab)


def _moba_kernel(q_ref, k_ref, v_ref, o_ref, *, S):
    nb = S // MOBA_BLOCK
    L = MOBA_BLOCK
    q = q_ref[...]
    k = k_ref[...]
    v = v_ref[...]
    lane = lax.broadcasted_iota(jnp.int32, (1, LANES), 1)
    key_blk = lax.broadcasted_iota(jnp.int32, (S, LANES), 0) // L
    lane_s = lax.broadcasted_iota(jnp.int32, (S, LANES), 1)
    qblk = lax.broadcasted_iota(jnp.int32, (GATE_ROWS, S), 1) // L
    cidx = lax.broadcasted_iota(jnp.int32, (GATE_ROWS, S), 0)
    r_i = lax.broadcasted_iota(jnp.int32, (L, L), 0)
    c_i = lax.broadcasted_iota(jnp.int32, (L, L), 1)
    causal = c_i <= r_i

    km = jnp.mean(k.astype(F32).reshape(nb, L, LANES), axis=1)
    km = jnp.concatenate([km, jnp.zeros((GATE_ROWS - nb, LANES), F32)], axis=0).astype(BF16)

    qa, ka, hms = [], [], []
    for h in range(2):
        hm = (lane >= MOBA_HEAD_DIM * h) & (lane < MOBA_HEAD_DIM * (h + 1))
        off = MOBA_HEAD_DIM * (1 - h)
        qh = jnp.where(hm, q, jnp.zeros_like(q))
        gate = _dot_nt(km, qh)
        valid = cidx < qblk
        gate = jnp.where(valid, gate, -jnp.inf)
        rank = jnp.zeros((GATE_ROWS, S), F32)
        for i in range(nb):
            gi = gate[i:i + 1, :]
            beats = (gi > gate) | ((gi == gate) & (i < cidx))
            rank = rank + beats.astype(F32)
        sel = valid & (rank < float(MOBA_TOPK))
        pen = jnp.where(sel | (cidx >= qblk), 0.0, NEG)
        pads = [jnp.zeros((r, S), F32) for r in (off, LANES - off - GATE_ROWS)]
        pen = jnp.concatenate(([pads[0]] if off else []) + [pen, pads[1]], axis=0)
        pen_q = pen.T.astype(BF16)
        scale = jnp.asarray(MOBA_HEAD_DIM ** -0.5, BF16)
        qa.append(jnp.where(hm, q * scale, pen_q))
        onehot = ((lane_s - off) == key_blk).astype(BF16)
        ka.append(jnp.where(hm, k, onehot))
        hms.append(hm)

    for n in range(nb):
        outs = []
        for h in range(2):
            qn = qa[h][n * L:(n + 1) * L]
            s_own = jnp.where(causal, _dot_nt(qn, ka[h][n * L:(n + 1) * L]), NEG)
            m = jnp.max(s_own, axis=-1, keepdims=True)
            if n > 0:
                s_past = _dot_nt(qn, ka[h][:n * L])
                m = jnp.maximum(m, jnp.max(s_past, axis=-1, keepdims=True))
                p_past = jnp.exp(s_past - m)
                l = jnp.sum(p_past, axis=-1, keepdims=True)
                acc = _dot(p_past.astype(BF16), v[:n * L])
            p_own = jnp.exp(s_own - m)
            if n > 0:
                l = l + jnp.sum(p_own, axis=-1, keepdims=True)
                acc = acc + _dot(p_own.astype(BF16), v[n * L:(n + 1) * L])
            else:
                l = jnp.sum(p_own, axis=-1, keepdims=True)
                acc = _dot(p_own.astype(BF16), v[n * L:(n + 1) * L])
            outs.append(acc / l)
        o_ref[n * L:(n + 1) * L, :] = jnp.where(hms[0], outs[0], outs[1]).astype(BF16)


def _moba(qkv, B, S):
    W = qkv.shape[-1] // 3
    npair = W // LANES
    return pl.pallas_call(
        functools.partial(_moba_kernel, S=S),
        grid=(B, npair),
        in_specs=[
            pl.BlockSpec((None, S, LANES), lambda b, j: (b, 0, j)),
            pl.BlockSpec((None, S, LANES), lambda b, j: (b, 0, npair + j)),
            pl.BlockSpec((None, S, LANES), lambda b, j: (b, 0, 2 * npair + j)),
        ],
        out_specs=pl.BlockSpec((None, S, LANES), lambda b, j: (b, 0, j)),
        out_shape=jax.ShapeDtypeStruct((B, S, W), BF16),
        compiler_params=pltpu.CompilerParams(
            dimension_semantics=("arbitrary", "arbitrary"), vmem_limit_bytes=VMEM_LIMIT),
        name="moba",
    )(qkv, qkv, qkv)


def _gdn_kernel(gd_ref, halo_ref, ab_ref, cw_ref, prm_ref, gn_ref, o_ref, state_ref, *, heads):
    C = GDN_CHUNK
    Dh = GDN_HEAD_DIM
    W = heads * Dh
    c = pl.program_id(1)

    @pl.when(c == 0)
    def _():
        state_ref[...] = jnp.zeros_like(state_ref)

    gd = gd_ref[...]
    halo = jnp.where(c > 0, halo_ref[...], 0.0)
    ext = jnp.concatenate([halo, gd[:, :3 * W]], axis=0)
    cw = cw_ref[...]
    conv = cw[GDN_CONV - 1:GDN_CONV] * ext[8:]
    for d in range(1, GDN_CONV):
        conv = conv + cw[GDN_CONV - 1 - d:GDN_CONV - d] * pltpu.roll(ext, d, 0)[8:]
    qkv = _silu(conv)

    ab = ab_ref[...]
    prm = prm_ref[...]
    sp_in = ab + prm[1:2]
    softplus = jnp.maximum(sp_in, 0.0) + jnp.log1p(jnp.exp(-jnp.abs(sp_in)))
    g = -jnp.exp(prm[0:1]) * softplus
    beta = _sigmoid(ab)
    rows = lax.broadcasted_iota(jnp.int32, (C, LANES), 0)
    G = g
    d = 1
    while d < C:
        G = G + jnp.where(rows >= d, pltpu.roll(G, d, 0), 0.0)
        d *= 2
    GT = G.T
    G_last = G[C - 1:C, :]

    r_i = lax.broadcasted_iota(jnp.int32, (C, C), 0)
    c_i = lax.broadcasted_iota(jnp.int32, (C, C), 1)
    incl = c_i <= r_i
    strict = c_i < r_i
    rc = r_i ^ c_i
    level = jnp.full((C, C), -1, jnp.int32)
    for b in range(C.bit_length() - 1):
        level = level + (rc >= (1 << b)).astype(jnp.int32)
    gn = gn_ref[...]

    for h in range(heads):
        q = qkv[:, h * Dh:(h + 1) * Dh]
        k = qkv[:, W + h * Dh:W + (h + 1) * Dh]
        v = qkv[:, 2 * W + h * Dh:2 * W + (h + 1) * Dh]
        z = gd[:, 3 * W + h * Dh:3 * W + (h + 1) * Dh]
        Gc = G[:, h:h + 1]
        Gr = GT[h:h + 1, :]
        bc = beta[:, heads + h:heads + h + 1]
        gl = G_last[:, h:h + 1]

        q = q * lax.rsqrt(jnp.sum(q * q, axis=-1, keepdims=True) + 1e-6) * (Dh ** -0.5)
        k = k * lax.rsqrt(jnp.sum(k * k, axis=-1, keepdims=True) + 1e-6)
        kb = k.astype(BF16)
        decay = jnp.exp(jnp.where(incl, Gc - Gr, -jnp.inf))
        A = jnp.where(strict, bc * _dot_nt(kb, kb) * decay, 0.0)
        qk = _dot_nt(q.astype(BF16), kb) * decay
        gam = jnp.exp(Gc)
        X = jnp.concatenate([bc * v, (bc * gam) * k], axis=1)
        N = -jnp.where(level == 0, A, 0.0)
        for lv in range(1, C.bit_length() - 1):
            E = jnp.where(level == lv, A, 0.0)
            Nb = N.astype(BF16)
            F = E + _dot(E.astype(BF16), Nb)
            N = N - F - _dot(Nb, F.astype(BF16))
        X = X + _dot(N.astype(BF16), X.astype(BF16))
        u_v = X[:, :Dh]
        w = X[:, Dh:]

        S0 = state_ref[h]
        Sb = S0.astype(BF16)
        u = u_v - _dot(w.astype(BF16), Sb)
        ub = u.astype(BF16)
        o = _dot((q * gam).astype(BF16), Sb) + _dot(qk.astype(BF16), ub)
        k_dec = k * jnp.exp(gl - Gc)
        state_ref[h] = jnp.exp(gl) * S0 + _dot(k_dec.T.astype(BF16), ub)

        y = _rms(o, gn) * _silu(z)
        o_ref[:, h * Dh:(h + 1) * Dh] = y.astype(BF16)


def _gdn(gd, ab, conv_w, prm, gdn_norm, B, S, heads):
    C = GDN_CHUNK
    W = heads * GDN_HEAD_DIM
    hb = C // 8
    return pl.pallas_call(
        functools.partial(_gdn_kernel, heads=heads),
        grid=(B, S // C),
        in_specs=[
            pl.BlockSpec((None, C, 4 * W), lambda b, c: (b, c, 0)),
            pl.BlockSpec((None, 8, 3 * W), lambda b, c: (b, jnp.maximum(c * hb - 1, 0), 0)),
            pl.BlockSpec((None, C, LANES), lambda b, c: (b, c, 0)),
            pl.BlockSpec((GDN_CONV, 3 * W), lambda b, c: (0, 0)),
            pl.BlockSpec((8, LANES), lambda b, c: (0, 0)),
            pl.BlockSpec((1, GDN_HEAD_DIM), lambda b, c: (0, 0)),
        ],
        out_specs=pl.BlockSpec((None, C, W), lambda b, c: (b, c, 0)),
        out_shape=jax.ShapeDtypeStruct((B, S, W), BF16),
        scratch_shapes=[pltpu.VMEM((heads, GDN_HEAD_DIM, GDN_HEAD_DIM), F32)],
        compiler_params=pltpu.CompilerParams(
            dimension_semantics=("arbitrary", "arbitrary"), vmem_limit_bytes=VMEM_LIMIT),
        name="gdn",
    )(gd, gd, ab, conv_w, prm, gdn_norm)


R_E1, R_E2, R_C1, R_C2, R_RANK1, R_RANK2 = range(6)


def _outproj_kernel(ya_ref, yb_ref, x_ref, woa_ref, wob_ref, fg_ref, wr_ref,
                    x1_ref, route_ref, cnt_ref, carry_ref):
    @pl.when(pl.program_id(0) == 0)
    def _():
        carry_ref[...] = jnp.zeros_like(carry_ref)

    x1 = x_ref[...] + _dot(ya_ref[...], woa_ref[...]) + _dot(yb_ref[...], wob_ref[...])
    x1_ref[...] = x1
    h = _rms(x1, fg_ref[...]).astype(BF16)
    logits = _dot(h, wr_ref[...])
    tm = logits.shape[0]
    lane = lax.broadcasted_iota(jnp.int32, (tm, LANES), 1).astype(F32)
    big = float(LANES)

    is_g = (lane >= N_EXPERTS) & (lane < N_EXPERTS + N_GROUPS)
    lg = jnp.where(is_g, logits, -jnp.inf)
    mg = jnp.max(lg, axis=-1, keepdims=True)
    g_sel = jnp.min(jnp.where(lg == mg, lane, big), axis=-1, keepdims=True) - N_EXPERTS
    p_group = 1.0 / jnp.sum(jnp.exp(lg - mg), axis=-1, keepdims=True)

    lo = g_sel * EXPERTS_PER_GROUP
    in_grp = (lane >= lo) & (lane < lo + EXPERTS_PER_GROUP)
    le = jnp.where(in_grp, logits, -jnp.inf)
    m1 = jnp.max(le, axis=-1, keepdims=True)
    i1 = jnp.min(jnp.where(le == m1, lane, big), axis=-1, keepdims=True)
    le2 = jnp.where(lane == i1, -jnp.inf, le)
    m2 = jnp.max(le2, axis=-1, keepdims=True)
    i2 = jnp.min(jnp.where(le2 == m2, lane, big), axis=-1, keepdims=True)
    se = jnp.sum(jnp.exp(le - m1), axis=-1, keepdims=True)
    p1 = 1.0 / se
    p2 = jnp.exp(m2 - m1) / se
    tot = p1 + p2
    c1 = p_group * (p1 / tot)
    c2 = p_group * (p2 / tot)

    hot = ((lane == i1) | (lane == i2)).astype(BF16)
    r_i = lax.broadcasted_iota(jnp.int32, (tm, tm), 0)
    c_i = lax.broadcasted_iota(jnp.int32, (tm, tm), 1)
    before = (c_i < r_i).astype(BF16)
    seen = _dot(before, hot) + carry_ref[0:1, :]
    rank1 = jnp.sum(jnp.where(lane == i1, seen, 0.0), axis=-1, keepdims=True)
    rank2 = jnp.sum(jnp.where(lane == i2, seen, 0.0), axis=-1, keepdims=True)
    carry = carry_ref[0:1, :] + jnp.sum(hot.astype(F32), axis=0, keepdims=True)
    carry_ref[...] = jnp.broadcast_to(carry, carry_ref.shape)
    cnt_ref[...] = jnp.broadcast_to(carry, cnt_ref.shape)

    rec = jnp.zeros((tm, LANES), F32)
    for slot, val in ((R_E1, i1), (R_E2, i2), (R_C1, c1), (R_C2, c2), (R_RANK1, rank1), (R_RANK2, rank2)):
        rec = jnp.where(lane == float(slot), val, rec)
    route_ref[...] = rec


def _outproj(ya, yb, x2d, woa, wob, fgain, wr, tm):
    T, D = x2d.shape
    Wa, Wb = ya.shape[1], yb.shape[1]
    return pl.pallas_call(
        _outproj_kernel,
        grid=(T // tm,),
        in_specs=[
            pl.BlockSpec((tm, Wa), lambda i: (i, 0)),
            pl.BlockSpec((tm, Wb), lambda i: (i, 0)),
            pl.BlockSpec((tm, D), lambda i: (i, 0)),
            pl.BlockSpec((Wa, D), lambda i: (0, 0)),
            pl.BlockSpec((Wb, D), lambda i: (0, 0)),
            pl.BlockSpec((1, D), lambda i: (0, 0)),
            pl.BlockSpec((D, LANES), lambda i: (0, 0)),
        ],
        out_specs=[
            pl.BlockSpec((tm, D), lambda i: (i, 0)),
            pl.BlockSpec((tm, LANES), lambda i: (i, 0)),
            pl.BlockSpec((8, LANES), lambda i: (0, 0)),
        ],
        out_shape=[
            jax.ShapeDtypeStruct((T, D), F32),
            jax.ShapeDtypeStruct((T, LANES), F32),
            jax.ShapeDtypeStruct((8, LANES), F32),
        ],
        scratch_shapes=[pltpu.VMEM((8, LANES), F32)],
        compiler_params=pltpu.CompilerParams(
            dimension_semantics=("arbitrary",), vmem_limit_bytes=VMEM_LIMIT),
        name="outproj_router",
    )(ya, yb, x2d, woa, wob, fgain, wr)


def _dispatch_kernel(pos_ref, x_ref, xs_in_ref, xs_ref, sem):
    del xs_in_ref
    tm = x_ref.shape[0]

    def issue(r, carry):
        for slot in range(2):
            p = pos_ref[0, slot * tm + r]
            pltpu.make_async_copy(x_ref.at[pl.ds(r, 1)], xs_ref.at[pl.ds(p, 1)], sem).start()
        return carry

    lax.fori_loop(0, tm, issue, 0, unroll=8)
    for _ in range(2):
        pltpu.make_async_copy(x_ref, xs_ref.at[pl.ds(0, tm)], sem).wait()


def _dispatch(pos, h, n_rows, tm):
    T, Dp = h.shape
    xs0 = jnp.zeros((n_rows, Dp), h.dtype)
    return pl.pallas_call(
        _dispatch_kernel,
        grid=(T // tm,),
        in_specs=[
            pl.BlockSpec((None, 1, 2 * tm), lambda i: (i, 0, 0), memory_space=pltpu.SMEM),
            pl.BlockSpec((tm, Dp), lambda i: (i, 0)),
            pl.BlockSpec(memory_space=pl.ANY),
        ],
        out_specs=pl.BlockSpec(memory_space=pl.ANY),
        out_shape=jax.ShapeDtypeStruct((n_rows, Dp), h.dtype),
        scratch_shapes=[pltpu.SemaphoreType.DMA],
        input_output_aliases={2: 0},
        compiler_params=pltpu.CompilerParams(
            dimension_semantics=("arbitrary",), vmem_limit_bytes=VMEM_LIMIT),
        name="moe_dispatch",
    )(pos, h, xs0)


def _experts_kernel(te_ref, tv_ref, xs_ref, fg_ref, wg_ref, wu_ref, wd_ref, y_ref):
    del te_ref
    i = pl.program_id(0)

    @pl.when(tv_ref[i] > 0)
    def _():
        h = _rms(xs_ref[...], fg_ref[...]).astype(BF16)
        a = _dot(h, wg_ref[...])
        b = _dot(h, wu_ref[...])
        y_ref[...] = _dot((_silu(a) * b).astype(BF16), wd_ref[...])

    @pl.when(tv_ref[i] == 0)
    def _():
        y_ref[...] = jnp.zeros_like(y_ref)


def _experts(tile_expert, tile_valid, xs, fgain, wg, wu, wd, rt):
    n_rows, Dp = xs.shape
    E, D, F = wg.shape
    return pl.pallas_call(
        _experts_kernel,
        grid_spec=pltpu.PrefetchScalarGridSpec(
            num_scalar_prefetch=2,
            grid=(n_rows // rt,),
            in_specs=[
                pl.BlockSpec((rt, Dp), lambda i, te, tv: (i, 0)),
                pl.BlockSpec((1, D), lambda i, te, tv: (0, 0)),
                pl.BlockSpec((None, D, F), lambda i, te, tv: (te[i], 0, 0)),
                pl.BlockSpec((None, D, F), lambda i, te, tv: (te[i], 0, 0)),
                pl.BlockSpec((None, F, D), lambda i, te, tv: (te[i], 0, 0)),
            ],
            out_specs=pl.BlockSpec((rt, D), lambda i, te, tv: (i, 0)),
        ),
        out_shape=jax.ShapeDtypeStruct((n_rows, D), F32),
        compiler_params=pltpu.CompilerParams(
            dimension_semantics=("arbitrary",), vmem_limit_bytes=VMEM_LIMIT),
        name="moe_experts",
    )(tile_expert, tile_valid, xs, fgain, wg, wu, wd)


def _combine_kernel(pos_ref, x1_ref, route_ref, fn_ref, y_ref, o_ref, buf_ref, sem):
    tm = x1_ref.shape[0]

    def issue(r, carry):
        for slot in range(2):
            p = pos_ref[0, slot * tm + r]
            pltpu.make_async_copy(y_ref.at[pl.ds(p, 1)], buf_ref.at[slot, pl.ds(r, 1)], sem).start()
        return carry

    lax.fori_loop(0, tm, issue, 0, unroll=8)
    for slot in range(2):
        pltpu.make_async_copy(y_ref.at[pl.ds(0, tm)], buf_ref.at[slot], sem).wait()
    route = route_ref[...]
    c1 = route[:, R_C1:R_C1 + 1]
    c2 = route[:, R_C2:R_C2 + 1]
    o_ref[...] = _rms(x1_ref[...] + c1 * buf_ref[0] + c2 * buf_ref[1], fn_ref[...])


def _combine(pos, x1, route, fnorm, y, tm):
    T, D = x1.shape
    return pl.pallas_call(
        _combine_kernel,
        grid=(T // tm,),
        in_specs=[
            pl.BlockSpec((None, 1, 2 * tm), lambda i: (i, 0, 0), memory_space=pltpu.SMEM),
            pl.BlockSpec((tm, D), lambda i: (i, 0)),
            pl.BlockSpec((tm, LANES), lambda i: (i, 0)),
            pl.BlockSpec((1, D), lambda i: (0, 0)),
            pl.BlockSpec(memory_space=pl.ANY),
        ],
        out_specs=pl.BlockSpec((tm, D), lambda i: (i, 0)),
        out_shape=jax.ShapeDtypeStruct((T, D), F32),
        scratch_shapes=[pltpu.VMEM((2, tm, D), F32), pltpu.SemaphoreType.DMA],
        compiler_params=pltpu.CompilerParams(
            dimension_semantics=("arbitrary",), vmem_limit_bytes=VMEM_LIMIT),
        name="moe_combine",
    )(pos, x1, route, fnorm, y)


def _pick_tile(T, pref):
    tm = min(pref, T)
    while T % tm:
        tm //= 2
    return tm


def kernel(x, attn_norm, w_in, conv_w, A_log, dt_bias, gdn_norm, w_out, ffn_norm,
           w_group, w_router, w_gate, w_up, w_down, final_norm):
    B, S, D = x.shape
    T = B * S
    depth = w_in.shape[0]
    heads = A_log.shape[1]
    Wb = heads * GDN_HEAD_DIM
    Wa = w_out.shape[1] - Wb
    assert S % MOBA_BLOCK == 0 and S // MOBA_BLOCK <= GATE_ROWS and S % GDN_CHUNK == 0
    assert Wa % LANES == 0 and 2 * heads <= LANES
    assert depth == 1, "the final RMSNorm is fused into the layer's last kernel"

    x2d = x.reshape(T, D)
    for l in range(depth):
        wl = w_in[l]
        wm = wl[:, :3 * Wa].astype(BF16)
        wg = wl[:, 3 * Wa:3 * Wa + 4 * Wb].astype(BF16)
        wab = jnp.pad(wl[:, 3 * Wa + 4 * Wb:], ((0, 0), (0, LANES - 2 * heads))).astype(BF16)
        qkv_a, gd, ab = _inproj(x2d, attn_norm[l][None, :], wm, wg, wab, _pick_tile(T, 512))

        y_a = _moba(qkv_a.reshape(B, S, 3 * Wa), B, S)

        prm = jnp.zeros((8, LANES), F32)
        prm = prm.at[0, :heads].set(A_log[l].astype(F32)).at[1, :heads].set(dt_bias[l].astype(F32))
        y_b = _gdn(gd.reshape(B, S, 4 * Wb), ab.reshape(B, S, LANES), conv_w[l].astype(F32), prm,
                   gdn_norm[l][None, :].astype(F32), B, S, heads)

        wo = w_out[l].astype(BF16)
        wr = jnp.concatenate([w_router[l], w_group[l]], axis=1)
        wr = jnp.pad(wr, ((0, 0), (0, LANES - wr.shape[1]))).astype(BF16)
        x1, route, counts = _outproj(y_a.reshape(T, Wa), y_b.reshape(T, Wb), x2d, wo[:Wa], wo[Wa:],
                                         ffn_norm[l][None, :], wr, _pick_tile(T, 512))

        rt = MOE_ROW_TILE
        n_tiles = -(-2 * T // rt) + N_EXPERTS
        cnt = counts[0, :N_EXPERTS].astype(jnp.int32)
        seg = -(-cnt // rt) * rt
        seg_end = jnp.cumsum(seg)
        seg_start = seg_end - seg
        tile_row = jnp.arange(n_tiles, dtype=jnp.int32) * rt
        tile_expert = jnp.minimum(jnp.sum(tile_row[:, None] >= seg_end[None, :], axis=1), N_EXPERTS - 1)
        tile_valid = (tile_row < seg_end[-1]).astype(jnp.int32)
        e12 = route[:, R_E1:R_E2 + 1].astype(jnp.int32)
        rank12 = route[:, R_RANK1:R_RANK2 + 1].astype(jnp.int32)
        pos12 = jnp.take(seg_start, e12) + rank12

        def tiled_pos(tm):
            return pos12.reshape(T // tm, tm, 2).transpose(0, 2, 1).reshape(T // tm, 1, 2 * tm)

        F = w_gate.shape[-1]
        wge = w_gate[l].reshape(N_EXPERTS, D, F).astype(BF16)
        wue = w_up[l].reshape(N_EXPERTS, D, F).astype(BF16)
        wde = w_down[l].reshape(N_EXPERTS, F, D).astype(BF16)
        tmd = _pick_tile(T, 512)
        xs = _dispatch(tiled_pos(tmd), x1, n_tiles * rt, tmd)
        ys = _experts(tile_expert.astype(jnp.int32), tile_valid, xs, ffn_norm[l][None, :],
                      wge, wue, wde, rt)
        tmc = _pick_tile(T, 256)
        x2d = _combine(tiled_pos(tmc), x1, route, final_norm[None, :], ys, tmc)
    return x2d.reshape(B, S, D)
```

```python
import functools

import jax
import jax.numpy as jnp
from jax import lax
from jax.experimental import pallas as pl
from jax.experimental.pallas import tpu as pltpu

F32 = jnp.float32
BF16 = jnp.bfloat16

LANES = 128
MOBA_HEAD_DIM = 64
MOBA_BLOCK = 256
MOBA_TOPK = 3
MOBA_Q_SCALE = MOBA_HEAD_DIM ** -0.5 * 1.4426950408889634
GDN_HEAD_DIM = 128
GDN_CONV = 4
GDN_CHUNK = 256
N_GROUPS = 4
EXPERTS_PER_GROUP = 8
N_EXPERTS = N_GROUPS * EXPERTS_PER_GROUP
RMS_EPS = 1e-6
NEG = -1e30
GATE_ROWS = 16
MOE_ROW_TILE = 512
VMEM_LIMIT = 48 * 1024 * 1024


def _dot(a, b):
    return jnp.dot(a, b, preferred_element_type=F32)


def _dot_nt(a, b):
    return lax.dot_general(a, b, (((1,), (1,)), ((), ())), preferred_element_type=F32)


def _sigmoid(x):
    return 1.0 / (1.0 + jnp.exp(-x))


def _silu(x):
    return x * _sigmoid(x)


def _rms(x, gain):
    return x * lax.rsqrt(jnp.mean(x * x, axis=-1, keepdims=True) + RMS_EPS) * gain


def _inproj_kernel(x_ref, gain_ref, wm_ref, wg_ref, wab_ref, om_ref, og_ref, oab_ref):
    h = _rms(x_ref[...], gain_ref[...]).astype(BF16)
    om = _dot(h, wm_ref[...])
    nq = om.shape[1] // 3
    om_ref[:, :nq] = (om[:, :nq] * MOBA_Q_SCALE).astype(BF16)
    om_ref[:, nq:] = om[:, nq:].astype(BF16)
    og_ref[...] = _dot(h, wg_ref[...])
    oab_ref[...] = _dot(h, wab_ref[...])


def _inproj(x2d, gain, wm, wg, wab, tm):
    T, D = x2d.shape
    nm, ng = wm.shape[1], wg.shape[1]
    return pl.pallas_call(
        _inproj_kernel,
        grid=(T // tm,),
        in_specs=[
            pl.BlockSpec((tm, D), lambda i: (i, 0)),
            pl.BlockSpec((1, D), lambda i: (0, 0)),
            pl.BlockSpec((D, nm), lambda i: (0, 0)),
            pl.BlockSpec((D, ng), lambda i: (0, 0)),
            pl.BlockSpec((D, LANES), lambda i: (0, 0)),
        ],
        out_specs=[
            pl.BlockSpec((tm, nm), lambda i: (i, 0)),
            pl.BlockSpec((tm, ng), lambda i: (i, 0)),
            pl.BlockSpec((tm, LANES), lambda i: (i, 0)),
        ],
        out_shape=[
            jax.ShapeDtypeStruct((T, nm), BF16),
            jax.ShapeDtypeStruct((T, ng), F32),
            jax.ShapeDtypeStruct((T, LANES), F32),
        ],
        compiler_params=pltpu.CompilerParams(
            dimension_semantics=("arbitrary",), vmem_limit_bytes=VMEM_LIMIT),
        name="inproj",
    )(x2d, gain, wm, wg, wab)


def _moba_kernel(q_ref, k_ref, v_ref, o_ref, *, S):
    nb = S // MOBA_BLOCK
    L = MOBA_BLOCK
    q = q_ref[...]
    k = k_ref[...]
    v = v_ref[...]
    lane = lax.broadcasted_iota(jnp.int32, (1, LANES), 1)
    key_blk = lax.broadcasted_iota(jnp.int32, (S, LANES), 0) // L
    lane_s = lax.broadcasted_iota(jnp.int32, (S, LANES), 1)
    qblk = lax.broadcasted_iota(jnp.int32, (GATE_ROWS, S), 1) // L
    cidx = lax.broadcasted_iota(jnp.int32, (GATE_ROWS, S), 0)
    r_i = lax.broadcasted_iota(jnp.int32, (L, L), 0)
    c_i = lax.broadcasted_iota(jnp.int32, (L, L), 1)
    causal = c_i <= r_i

    km = jnp.mean(k.astype(F32).reshape(nb, L, LANES), axis=1)
    km = jnp.concatenate([km, jnp.zeros((GATE_ROWS - nb, LANES), F32)], axis=0).astype(BF16)

    qa, ka, va, hms = [], [], [], []
    for h in range(2):
        hm = (lane >= MOBA_HEAD_DIM * h) & (lane < MOBA_HEAD_DIM * (h + 1))
        off = MOBA_HEAD_DIM * (1 - h)
        qh = jnp.where(hm, q, jnp.zeros_like(q))
        gate = _dot_nt(km, qh)
        valid = cidx < qblk
        gate = jnp.where(valid, gate, -jnp.inf)
        rank = jnp.zeros((GATE_ROWS, S), F32)
        for i in range(nb):
            gi = gate[i:i + 1, :]
            beats = (gi > gate) | ((gi == gate) & (i < cidx))
            rank = rank + beats.astype(F32)
        sel = valid & (rank < float(MOBA_TOPK))
        pen = jnp.where(sel | (cidx >= qblk), 0.0, NEG)
        pads = [jnp.zeros((r, S), F32) for r in (off, LANES - off - GATE_ROWS)]
        pen = jnp.concatenate(([pads[0]] if off else []) + [pen, pads[1]], axis=0)
        pen_q = pen.T.astype(BF16)
        qa.append(jnp.where(hm, q, pen_q))
        onehot = ((lane_s - off) == key_blk).astype(BF16)
        ka.append(jnp.where(hm, k, onehot))
        va.append(jnp.where(hm, v, jnp.ones_like(v)))
        hms.append(hm)

    for n in range(nb):
        outs = []
        for h in range(2):
            qn = qa[h][n * L:(n + 1) * L]
            s_own = jnp.where(causal, _dot_nt(qn, ka[h][n * L:(n + 1) * L]), NEG)
            m = jnp.max(s_own, axis=-1, keepdims=True)
            if n > 0:
                s_past = _dot_nt(qn, ka[h][:n * L])
                m = jnp.maximum(m, jnp.max(s_past, axis=-1, keepdims=True))
                acc = _dot(jnp.exp2(s_past - m).astype(BF16), va[h][:n * L])
            p_own = jnp.exp2(s_own - m).astype(BF16)
            if n > 0:
                acc = acc + _dot(p_own, va[h][n * L:(n + 1) * L])
            else:
                acc = _dot(p_own, va[h][n * L:(n + 1) * L])
            outs.append(acc / pltpu.roll(acc, MOBA_HEAD_DIM, 1))
        o_ref[n * L:(n + 1) * L, :] = jnp.where(hms[0], outs[0], outs[1]).astype(BF16)


def _moba(qkv, B, S):
    W = qkv.shape[-1] // 3
    npair = W // LANES
    return pl.pallas_call(
        functools.partial(_moba_kernel, S=S),
        grid=(B, npair),
        in_specs=[
            pl.BlockSpec((None, S, LANES), lambda b, j: (b, 0, j)),
            pl.BlockSpec((None, S, LANES), lambda b, j: (b, 0, npair + j)),
            pl.BlockSpec((None, S, LANES), lambda b, j: (b, 0, 2 * npair + j)),
        ],
        out_specs=pl.BlockSpec((None, S, LANES), lambda b, j: (b, 0, j)),
        out_shape=jax.ShapeDtypeStruct((B, S, W), BF16),
        compiler_params=pltpu.CompilerParams(
            dimension_semantics=("arbitrary", "arbitrary"), vmem_limit_bytes=VMEM_LIMIT),
        name="moba",
    )(qkv, qkv, qkv)


def _gdn_kernel(gd_ref, halo_ref, ab_ref, cw_ref, prm_ref, gn_ref, o_ref, state_ref, *, heads):
    C = GDN_CHUNK
    Dh = GDN_HEAD_DIM
    W = heads * Dh
    c = pl.program_id(1)

    @pl.when(c == 0)
    def _():
        state_ref[...] = jnp.zeros_like(state_ref)

    gd = gd_ref[...]
    halo = jnp.where(c > 0, halo_ref[...], 0.0)
    ext = jnp.concatenate([halo, gd[:, :3 * W]], axis=0)
    cw = cw_ref[...]
    conv = cw[GDN_CONV - 1:GDN_CONV] * ext[8:]
    for d in range(1, GDN_CONV):
        conv = conv + cw[GDN_CONV - 1 - d:GDN_CONV - d] * pltpu.roll(ext, d, 0)[8:]
    qkv = _silu(conv)

    ab = ab_ref[...]
    prm = prm_ref[...]
    sp_in = ab + prm[1:2]
    softplus = jnp.maximum(sp_in, 0.0) + jnp.log1p(jnp.exp(-jnp.abs(sp_in)))
    g = -jnp.exp(prm[0:1]) * softplus
    beta = _sigmoid(ab)
    rows = lax.broadcasted_iota(jnp.int32, (C, LANES), 0)
    G = g
    d = 1
    while d < C:
        G = G + jnp.where(rows >= d, pltpu.roll(G, d, 0), 0.0)
        d *= 2
    GT = G.T
    G_last = G[C - 1:C, :]

    r_i = lax.broadcasted_iota(jnp.int32, (C, C), 0)
    c_i = lax.broadcasted_iota(jnp.int32, (C, C), 1)
    incl = c_i <= r_i
    strict = c_i < r_i
    rc = r_i ^ c_i
    level = jnp.full((C, C), -1, jnp.int32)
    for b in range(C.bit_length() - 1):
        level = level + (rc >= (1 << b)).astype(jnp.int32)
    level_b = level.astype(F32).astype(BF16)
    gn = gn_ref[...]

    H = range(heads)
    q = [qkv[:, h * Dh:(h + 1) * Dh] for h in H]
    k = [qkv[:, W + h * Dh:W + (h + 1) * Dh] for h in H]
    v = [qkv[:, 2 * W + h * Dh:2 * W + (h + 1) * Dh] for h in H]
    Gc = [G[:, h:h + 1] for h in H]
    bc = [beta[:, heads + h:heads + h + 1] for h in H]
    gl = [G_last[:, h:h + 1] for h in H]

    q = [t * lax.rsqrt(jnp.sum(t * t, axis=-1, keepdims=True) + 1e-6) * (Dh ** -0.5) for t in q]
    k = [t * lax.rsqrt(jnp.sum(t * t, axis=-1, keepdims=True) + 1e-6) for t in k]
    kb = [t.astype(BF16) for t in k]
    decay = [jnp.exp(jnp.where(incl, Gc[h] - GT[h:h + 1, :], -jnp.inf)) for h in H]
    A = [jnp.where(strict, bc[h] * _dot_nt(kb[h], kb[h]) * decay[h], 0.0) for h in H]
    qk = [(_dot_nt(q[h].astype(BF16), kb[h]) * decay[h]).astype(BF16) for h in H]
    gam = [jnp.exp(t) for t in Gc]
    X = [jnp.concatenate([bc[h] * v[h], (bc[h] * gam[h]) * k[h]], axis=1) for h in H]
    Ab = [t.astype(BF16) for t in A]
    zero = jnp.zeros((C, C), BF16)
    Tm = [jnp.where(level_b == -1.0, jnp.ones((C, C), BF16), jnp.where(level_b == 0.0, -t, zero))
          for t in Ab]
    for lv in range(1, C.bit_length() - 1):
        at_lv = level_b == float(lv)
        E = [jnp.where(at_lv, t, zero) for t in Ab]
        F = [_dot(E[h], Tm[h]).astype(BF16) for h in H]
        Tm = [Tm[h] - _dot(Tm[h], F[h]).astype(BF16) for h in H]
    X = [X[h] + _dot(jnp.where(level_b == -1.0, zero, Tm[h]), X[h].astype(BF16)) for h in H]

    S0 = [state_ref[h] for h in H]
    Sb = [t.astype(BF16) for t in S0]
    ub = [(X[h][:, :Dh] - _dot(X[h][:, Dh:].astype(BF16), Sb[h])).astype(BF16) for h in H]
    o = [_dot((q[h] * gam[h]).astype(BF16), Sb[h]) + _dot(qk[h], ub[h]) for h in H]
    k_dec = [(k[h] * jnp.exp(gl[h] - Gc[h])).T.astype(BF16) for h in H]
    for h in H:
        state_ref[h] = jnp.exp(gl[h]) * S0[h] + _dot(k_dec[h], ub[h])
    for h in H:
        z = gd[:, 3 * W + h * Dh:3 * W + (h + 1) * Dh]
        o_ref[:, h * Dh:(h + 1) * Dh] = (_rms(o[h], gn) * _silu(z)).astype(BF16)


def _gdn(gd, ab, conv_w, prm, gdn_norm, B, S, heads):
    C = GDN_CHUNK
    W = heads * GDN_HEAD_DIM
    hb = C // 8
    return pl.pallas_call(
        functools.partial(_gdn_kernel, heads=heads),
        grid=(B, S // C),
        in_specs=[
            pl.BlockSpec((None, C, 4 * W), lambda b, c: (b, c, 0)),
            pl.BlockSpec((None, 8, 3 * W), lambda b, c: (b, jnp.maximum(c * hb - 1, 0), 0)),
            pl.BlockSpec((None, C, LANES), lambda b, c: (b, c, 0)),
            pl.BlockSpec((GDN_CONV, 3 * W), lambda b, c: (0, 0)),
            pl.BlockSpec((8, LANES), lambda b, c: (0, 0)),
            pl.BlockSpec((1, GDN_HEAD_DIM), lambda b, c: (0, 0)),
        ],
        out_specs=pl.BlockSpec((None, C, W), lambda b, c: (b, c, 0)),
        out_shape=jax.ShapeDtypeStruct((B, S, W), BF16),
        scratch_shapes=[pltpu.VMEM((heads, GDN_HEAD_DIM, GDN_HEAD_DIM), F32)],
        compiler_params=pltpu.CompilerParams(
            dimension_semantics=("arbitrary", "arbitrary"), vmem_limit_bytes=VMEM_LIMIT),
        name="gdn",
    )(gd, gd, ab, conv_w, prm, gdn_norm)


R_E1, R_E2, R_C1, R_C2, R_RANK1, R_RANK2 = range(6)


def _outproj_kernel(ya_ref, yb_ref, x_ref, woa_ref, wob_ref, fg_ref, wr_ref,
                    x1_ref, route_ref, cnt_ref, carry_ref):
    @pl.when(pl.program_id(0) == 0)
    def _():
        carry_ref[...] = jnp.zeros_like(carry_ref)

    x1 = x_ref[...] + _dot(ya_ref[...], woa_ref[...]) + _dot(yb_ref[...], wob_ref[...])
    x1_ref[...] = x1
    h = _rms(x1, fg_ref[...]).astype(BF16)
    logits = _dot(h, wr_ref[...])
    tm = logits.shape[0]
    lane = lax.broadcasted_iota(jnp.int32, (tm, LANES), 1).astype(F32)
    big = float(LANES)

    is_g = (lane >= N_EXPERTS) & (lane < N_EXPERTS + N_GROUPS)
    lg = jnp.where(is_g, logits, -jnp.inf)
    mg = jnp.max(lg, axis=-1, keepdims=True)
    g_sel = jnp.min(jnp.where(lg == mg, lane, big), axis=-1, keepdims=True) - N_EXPERTS
    p_group = 1.0 / jnp.sum(jnp.exp(lg - mg), axis=-1, keepdims=True)

    lo = g_sel * EXPERTS_PER_GROUP
    in_grp = (lane >= lo) & (lane < lo + EXPERTS_PER_GROUP)
    le = jnp.where(in_grp, logits, -jnp.inf)
    m1 = jnp.max(le, axis=-1, keepdims=True)
    i1 = jnp.min(jnp.where(le == m1, lane, big), axis=-1, keepdims=True)
    le2 = jnp.where(lane == i1, -jnp.inf, le)
    m2 = jnp.max(le2, axis=-1, keepdims=True)
    i2 = jnp.min(jnp.where(le2 == m2, lane, big), axis=-1, keepdims=True)
    se = jnp.sum(jnp.exp(le - m1), axis=-1, keepdims=True)
    p1 = 1.0 / se
    p2 = jnp.exp(m2 - m1) / se
    tot = p1 + p2
    c1 = p_group * (p1 / tot)
    c2 = p_group * (p2 / tot)

    hot = ((lane == i1) | (lane == i2)).astype(BF16)
    r_i = lax.broadcasted_iota(jnp.int32, (tm, tm), 0)
    c_i = lax.broadcasted_iota(jnp.int32, (tm, tm), 1)
    before = (c_i < r_i).astype(BF16)
    seen = _dot(before, hot) + carry_ref[0:1, :]
    rank1 = jnp.sum(jnp.where(lane == i1, seen, 0.0), axis=-1, keepdims=True)
    rank2 = jnp.sum(jnp.where(lane == i2, seen, 0.0), axis=-1, keepdims=True)
    carry = carry_ref[0:1, :] + jnp.sum(hot.astype(F32), axis=0, keepdims=True)
    carry_ref[...] = jnp.broadcast_to(carry, carry_ref.shape)
    cnt_ref[...] = jnp.broadcast_to(carry, cnt_ref.shape)

    rec = jnp.zeros((tm, LANES), F32)
    for slot, val in ((R_E1, i1), (R_E2, i2), (R_C1, c1), (R_C2, c2), (R_RANK1, rank1), (R_RANK2, rank2)):
        rec = jnp.where(lane == float(slot), val, rec)
    route_ref[...] = rec


def _outproj(ya, yb, x2d, woa, wob, fgain, wr, tm):
    T, D = x2d.shape
    Wa, Wb = ya.shape[1], yb.shape[1]
    return pl.pallas_call(
        _outproj_kernel,
        grid=(T // tm,),
        in_specs=[
            pl.BlockSpec((tm, Wa), lambda i: (i, 0)),
            pl.BlockSpec((tm, Wb), lambda i: (i, 0)),
            pl.BlockSpec((tm, D), lambda i: (i, 0)),
            pl.BlockSpec((Wa, D), lambda i: (0, 0)),
            pl.BlockSpec((Wb, D), lambda i: (0, 0)),
            pl.BlockSpec((1, D), lambda i: (0, 0)),
            pl.BlockSpec((D, LANES), lambda i: (0, 0)),
        ],
        out_specs=[
            pl.BlockSpec((tm, D), lambda i: (i, 0)),
            pl.BlockSpec((tm, LANES), lambda i: (i, 0)),
            pl.BlockSpec((8, LANES), lambda i: (0, 0)),
        ],
        out_shape=[
            jax.ShapeDtypeStruct((T, D), F32),
            jax.ShapeDtypeStruct((T, LANES), F32),
            jax.ShapeDtypeStruct((8, LANES), F32),
        ],
        scratch_shapes=[pltpu.VMEM((8, LANES), F32)],
        compiler_params=pltpu.CompilerParams(
            dimension_semantics=("arbitrary",), vmem_limit_bytes=VMEM_LIMIT),
        name="outproj_router",
    )(ya, yb, x2d, woa, wob, fgain, wr)


def _dispatch_kernel(tail_ref, pos_ref, x_ref, xs_ref, zero_ref, sem):
    tm = x_ref.shape[0]
    rt = zero_ref.shape[0]

    @pl.when(pl.program_id(0) == 0)
    def _():
        zero_ref[...] = jnp.zeros_like(zero_ref)
        fills = [pltpu.make_async_copy(zero_ref, xs_ref.at[pl.ds(pl.multiple_of(tail_ref[e], 8), rt)], sem)
                 for e in range(N_EXPERTS)]
        for f in fills:
            f.start()
        for f in fills:
            f.wait()

        def fill_tile(j, carry):
            f = pltpu.make_async_copy(zero_ref, xs_ref.at[pl.ds(pl.multiple_of(j * rt, 8), rt)], sem)
            f.start()
            f.wait()
            return carry

        lax.fori_loop(tail_ref[N_EXPERTS] // rt, xs_ref.shape[0] // rt, fill_tile, 0)

    def issue(r, carry):
        for slot in range(2):
            p = pos_ref[0, slot * tm + r]
            pltpu.make_async_copy(x_ref.at[pl.ds(r, 1)], xs_ref.at[pl.ds(p, 1)], sem).start(priority=slot)
        return carry

    lax.fori_loop(0, tm, issue, 0, unroll=8)
    for _ in range(2):
        pltpu.make_async_copy(x_ref, xs_ref.at[pl.ds(0, tm)], sem).wait()


def _dispatch(tails, pos, h, n_rows, tm, rt):
    T, Dp = h.shape
    return pl.pallas_call(
        _dispatch_kernel,
        grid_spec=pltpu.PrefetchScalarGridSpec(
            num_scalar_prefetch=1,
            grid=(T // tm,),
            in_specs=[
                pl.BlockSpec((None, 1, 2 * tm), lambda i, tl: (i, 0, 0), memory_space=pltpu.SMEM),
                pl.BlockSpec((tm, Dp), lambda i, tl: (i, 0)),
            ],
            out_specs=pl.BlockSpec(memory_space=pl.ANY),
            scratch_shapes=[pltpu.VMEM((rt, Dp), h.dtype), pltpu.SemaphoreType.DMA],
        ),
        out_shape=jax.ShapeDtypeStruct((n_rows + rt, Dp), h.dtype),
        compiler_params=pltpu.CompilerParams(
            dimension_semantics=("arbitrary",), vmem_limit_bytes=VMEM_LIMIT),
        name="moe_dispatch",
    )(tails, pos, h)


def _experts_kernel(te_ref, tv_ref, xs_ref, fg_ref, wg_ref, wu_ref, wd_ref, y_ref):
    del te_ref
    i = pl.program_id(0)

    @pl.when(tv_ref[i] > 0)
    def _():
        h = _rms(xs_ref[...], fg_ref[...]).astype(BF16)
        a = _dot(h, wg_ref[...])
        b = _dot(h, wu_ref[...])
        y_ref[...] = _dot((_silu(a) * b).astype(BF16), wd_ref[...])

    @pl.when(tv_ref[i] == 0)
    def _():
        y_ref[...] = jnp.zeros_like(y_ref)


def _experts(tile_expert, tile_valid, xs, fgain, wg, wu, wd, rt):
    n_rows = tile_expert.shape[0] * rt
    Dp = xs.shape[1]
    E, D, F = wg.shape
    return pl.pallas_call(
        _experts_kernel,
        grid_spec=pltpu.PrefetchScalarGridSpec(
            num_scalar_prefetch=2,
            grid=(n_rows // rt,),
            in_specs=[
                pl.BlockSpec((rt, Dp), lambda i, te, tv: (i * tv[i], 0)),
                pl.BlockSpec((1, D), lambda i, te, tv: (0, 0)),
                pl.BlockSpec((None, D, F), lambda i, te, tv: (te[i], 0, 0)),
                pl.BlockSpec((None, D, F), lambda i, te, tv: (te[i], 0, 0)),
                pl.BlockSpec((None, F, D), lambda i, te, tv: (te[i], 0, 0)),
            ],
            out_specs=pl.BlockSpec((rt, D), lambda i, te, tv: (i, 0)),
        ),
        out_shape=jax.ShapeDtypeStruct((n_rows, D), F32),
        compiler_params=pltpu.CompilerParams(
            dimension_semantics=("arbitrary",), vmem_limit_bytes=VMEM_LIMIT),
        name="moe_experts",
    )(tile_expert, tile_valid, xs, fgain, wg, wu, wd)


def _combine_kernel(pos_ref, pos_next_ref, x1_ref, route_ref, fn_ref, y_ref, o_ref, buf_ref, sem):
    tm = x1_ref.shape[0]
    i = pl.program_id(0)
    ring = i % 2

    def row_copy(p_ref, r, slot, rg):
        p = p_ref[0, slot * tm + r]
        return pltpu.make_async_copy(y_ref.at[pl.ds(p, 1)], buf_ref.at[rg, slot, pl.ds(r, 1)], sem.at[rg])

    def issue(p_ref, rg):
        def body(r, carry):
            for slot in range(2):
                row_copy(p_ref, r, slot, rg).start(priority=slot)
            return carry
        lax.fori_loop(0, tm, body, 0, unroll=8)

    @pl.when(i == 0)
    def _():
        issue(pos_ref, ring)

    @pl.when(i + 1 < pl.num_programs(0))
    def _():
        issue(pos_next_ref, 1 - ring)

    for slot in range(2):
        pltpu.make_async_copy(y_ref.at[pl.ds(0, tm)], buf_ref.at[ring, slot], sem.at[ring]).wait()
    route = route_ref[...]
    c1 = route[:, R_C1:R_C1 + 1]
    c2 = route[:, R_C2:R_C2 + 1]
    o_ref[...] = _rms(x1_ref[...] + c1 * buf_ref[ring, 0] + c2 * buf_ref[ring, 1], fn_ref[...])


def _combine(pos, x1, route, fnorm, y, tm):
    T, D = x1.shape
    n = T // tm
    return pl.pallas_call(
        _combine_kernel,
        grid=(n,),
        in_specs=[
            pl.BlockSpec((None, 1, 2 * tm), lambda i: (i, 0, 0), memory_space=pltpu.SMEM),
            pl.BlockSpec((None, 1, 2 * tm), lambda i: (jnp.minimum(i + 1, n - 1), 0, 0),
                         memory_space=pltpu.SMEM),
            pl.BlockSpec((tm, D), lambda i: (i, 0)),
            pl.BlockSpec((tm, LANES), lambda i: (i, 0)),
            pl.BlockSpec((1, D), lambda i: (0, 0)),
            pl.BlockSpec(memory_space=pl.ANY),
        ],
        out_specs=pl.BlockSpec((tm, D), lambda i: (i, 0)),
        out_shape=jax.ShapeDtypeStruct((T, D), F32),
        scratch_shapes=[pltpu.VMEM((2, 2, tm, D), F32), pltpu.SemaphoreType.DMA((2,))],
        compiler_params=pltpu.CompilerParams(
            dimension_semantics=("arbitrary",), vmem_limit_bytes=VMEM_LIMIT),
        name="moe_combine",
    )(pos, pos, x1, route, fnorm, y)


def _pick_tile(T, pref):
    tm = min(pref, T)
    while T % tm:
        tm //= 2
    return tm


def kernel(x, attn_norm, w_in, conv_w, A_log, dt_bias, gdn_norm, w_out, ffn_norm,
           w_group, w_router, w_gate, w_up, w_down, final_norm):
    B, S, D = x.shape
    T = B * S
    depth = w_in.shape[0]
    heads = A_log.shape[1]
    Wb = heads * GDN_HEAD_DIM
    Wa = w_out.shape[1] - Wb
    assert S % MOBA_BLOCK == 0 and S // MOBA_BLOCK <= GATE_ROWS and S % GDN_CHUNK == 0
    assert Wa % LANES == 0 and 2 * heads <= LANES
    assert depth == 1, "the final RMSNorm is fused into the layer's last kernel"

    x2d = x.reshape(T, D)
    for l in range(depth):
        wl = w_in[l]
        wm = wl[:, :3 * Wa].astype(BF16)
        wg = wl[:, 3 * Wa:3 * Wa + 4 * Wb].astype(BF16)
        wab = jnp.pad(wl[:, 3 * Wa + 4 * Wb:], ((0, 0), (0, LANES - 2 * heads))).astype(BF16)
        qkv_a, gd, ab = _inproj(x2d, attn_norm[l][None, :], wm, wg, wab, _pick_tile(T, 512))

        y_a = _moba(qkv_a.reshape(B, S, 3 * Wa), B, S)

        prm = jnp.zeros((8, LANES), F32)
        prm = prm.at[0, :heads].set(A_log[l].astype(F32)).at[1, :heads].set(dt_bias[l].astype(F32))
        y_b = _gdn(gd.reshape(B, S, 4 * Wb), ab.reshape(B, S, LANES), conv_w[l].astype(F32), prm,
                   gdn_norm[l][None, :].astype(F32), B, S, heads)

        wo = w_out[l].astype(BF16)
        wr = jnp.concatenate([w_router[l], w_group[l]], axis=1)
        wr = jnp.pad(wr, ((0, 0), (0, LANES - wr.shape[1]))).astype(BF16)
        x1, route, counts = _outproj(y_a.reshape(T, Wa), y_b.reshape(T, Wb), x2d, wo[:Wa], wo[Wa:],
                                         ffn_norm[l][None, :], wr, _pick_tile(T, 512))

        rt = MOE_ROW_TILE
        n_tiles = -(-2 * T // rt) + N_EXPERTS
        cnt = counts[0, :N_EXPERTS].astype(jnp.int32)
        seg = -(-cnt // rt) * rt
        seg_end = jnp.cumsum(seg)
        seg_start = seg_end - seg
        tile_row = jnp.arange(n_tiles, dtype=jnp.int32) * rt
        tile_expert = jnp.minimum(jnp.sum(tile_row[:, None] >= seg_end[None, :], axis=1), N_EXPERTS - 1)
        tile_valid = (tile_row < seg_end[-1]).astype(jnp.int32)
        e12 = route[:, R_E1:R_E2 + 1].astype(jnp.int32)
        rank12 = route[:, R_RANK1:R_RANK2 + 1].astype(jnp.int32)
        e_ids = jnp.arange(N_EXPERTS, dtype=jnp.int32)
        pos12 = rank12 + jnp.sum(jnp.where(e12[..., None] == e_ids, seg_start, 0), axis=-1)

        def tiled_pos(tm):
            return pos12.reshape(T // tm, tm, 2).transpose(0, 2, 1).reshape(T // tm, 1, 2 * tm)

        F = w_gate.shape[-1]
        wge = w_gate[l].reshape(N_EXPERTS, D, F).astype(BF16)
        wue = w_up[l].reshape(N_EXPERTS, D, F).astype(BF16)
        wde = w_down[l].reshape(N_EXPERTS, F, D).astype(BF16)
        tmd = _pick_tile(T, 512)
        tails = jnp.concatenate([(seg_start + cnt) // 8 * 8, seg_end[-1:]])
        xs = _dispatch(tails, tiled_pos(tmd), x1, n_tiles * rt, tmd, rt)
        ys = _experts(tile_expert.astype(jnp.int32), tile_valid, xs, ffn_norm[l][None, :],
                      wge, wue, wde, rt)
        tmc = _pick_tile(T, 256)
        x2d = _combine(tiled_pos(tmc), x1, route, final_norm[None, :], ys, tmc)
    return x2d.reshape(B, S, D)
```

```python
import functools

import jax
import jax.numpy as jnp
from jax import lax
from jax.experimental import pallas as pl
from jax.experimental.pallas import tpu as pltpu

F32 = jnp.float32
BF16 = jnp.bfloat16

LANES = 128
MOBA_HEAD_DIM = 64
MOBA_BLOCK = 256
MOBA_TOPK = 3
MOBA_Q_SCALE = MOBA_HEAD_DIM ** -0.5 * 1.4426950408889634
GDN_HEAD_DIM = 128
GDN_CONV = 4
GDN_CHUNK = 256
N_GROUPS = 4
EXPERTS_PER_GROUP = 8
N_EXPERTS = N_GROUPS * EXPERTS_PER_GROUP
RMS_EPS = 1e-6
NEG = -1e30
GATE_ROWS = 16
MOE_ROW_TILE = 512
ROUTER_SUBTILE = 128
VMEM_LIMIT = 48 * 1024 * 1024


def _dot(a, b):
    return jnp.dot(a, b, preferred_element_type=F32)


def _dot_nt(a, b):
    return lax.dot_general(a, b, (((1,), (1,)), ((), ())), preferred_element_type=F32)


def _sigmoid(x):
    return 1.0 / (1.0 + jnp.exp(-x))


def _silu(x):
    return x * _sigmoid(x)


def _rms(x, gain):
    return x * lax.rsqrt(jnp.mean(x * x, axis=-1, keepdims=True) + RMS_EPS) * gain


def _inproj_kernel(x_ref, gain_ref, w_ref, cw_ref, om_ref, og_ref, oab_ref, prev_ref, *, tiles_per_seq):
    nm, ng = om_ref.shape[1], og_ref.shape[1]
    tm = x_ref.shape[0]

    @pl.when(pl.program_id(0) % tiles_per_seq == 0)
    def _():
        prev_ref[...] = jnp.zeros_like(prev_ref)

    h = _rms(x_ref[...], gain_ref[...]).astype(BF16)
    og = _dot(h, w_ref[:, nm:nm + ng])
    om = _dot(h, w_ref[:, :nm])
    nq = nm // 3
    om_ref[:, :nq] = (om[:, :nq] * MOBA_Q_SCALE).astype(BF16)
    om_ref[:, nq:] = om[:, nq:].astype(BF16)
    oab_ref[...] = _dot(h, w_ref[:, nm + ng:])

    W = ng // 4
    raw = og[:, :3 * W]
    ext = jnp.concatenate([prev_ref[...], raw], axis=0)
    prev_ref[...] = raw[tm - 8:, :]
    cw = cw_ref[...]
    conv = cw[GDN_CONV - 1:GDN_CONV] * raw
    for d in range(1, GDN_CONV):
        conv = conv + cw[GDN_CONV - 1 - d:GDN_CONV - d] * pltpu.roll(ext, d, 0)[8:]
    qkv = _silu(conv)
    Dh = GDN_HEAD_DIM
    for j in range(2 * W // Dh):
        t = qkv[:, j * Dh:(j + 1) * Dh]
        t = t * lax.rsqrt(jnp.sum(t * t, axis=-1, keepdims=True) + 1e-6)
        og_ref[:, j * Dh:(j + 1) * Dh] = t * (Dh ** -0.5) if j < W // Dh else t
    og_ref[:, 2 * W:3 * W] = qkv[:, 2 * W:]
    og_ref[:, 3 * W:] = og[:, 3 * W:]


def _inproj(x2d, gain, w_all, conv_w, nm, ng, tm, seq_len):
    T, D = x2d.shape
    assert w_all.shape[1] == nm + ng + LANES and nm % LANES == 0 and ng % LANES == 0
    assert seq_len % tm == 0 and tm >= 8
    return pl.pallas_call(
        functools.partial(_inproj_kernel, tiles_per_seq=seq_len // tm),
        grid=(T // tm,),
        in_specs=[
            pl.BlockSpec((tm, D), lambda i: (i, 0)),
            pl.BlockSpec((1, D), lambda i: (0, 0)),
            pl.BlockSpec((D, nm + ng + LANES), lambda i: (0, 0)),
            pl.BlockSpec((GDN_CONV, 3 * ng // 4), lambda i: (0, 0)),
        ],
        out_specs=[
            pl.BlockSpec((tm, nm), lambda i: (i, 0)),
            pl.BlockSpec((tm, ng), lambda i: (i, 0)),
            pl.BlockSpec((tm, LANES), lambda i: (i, 0)),
        ],
        out_shape=[
            jax.ShapeDtypeStruct((T, nm), BF16),
            jax.ShapeDtypeStruct((T, ng), F32),
            jax.ShapeDtypeStruct((T, LANES), F32),
        ],
        scratch_shapes=[pltpu.VMEM((8, 3 * ng // 4), F32)],
        compiler_params=pltpu.CompilerParams(
            dimension_semantics=("arbitrary",), vmem_limit_bytes=VMEM_LIMIT),
        name="inproj",
    )(x2d, gain, w_all, conv_w)


def _moba_kernel(q_ref, k_ref, v_ref, o_ref, *, S):
    nb = S // MOBA_BLOCK
    L = MOBA_BLOCK
    q = q_ref[...]
    k = k_ref[...]
    v = v_ref[...]
    lane = lax.broadcasted_iota(jnp.int32, (1, LANES), 1)
    key_blk = lax.broadcasted_iota(jnp.int32, (S, LANES), 0) // L
    lane_s = lax.broadcasted_iota(jnp.int32, (S, LANES), 1)
    qblk = lax.broadcasted_iota(jnp.int32, (GATE_ROWS, S), 1) // L
    cidx = lax.broadcasted_iota(jnp.int32, (GATE_ROWS, S), 0)
    r_i = lax.broadcasted_iota(jnp.int32, (L, L), 0)
    c_i = lax.broadcasted_iota(jnp.int32, (L, L), 1)
    causal = c_i <= r_i

    km = jnp.mean(k.astype(F32).reshape(nb, L, LANES), axis=1)
    km = jnp.concatenate([km, jnp.zeros((GATE_ROWS - nb, LANES), F32)], axis=0).astype(BF16)

    qa, ka, va, hms = [], [], [], []
    for h in range(2):
        hm = (lane >= MOBA_HEAD_DIM * h) & (lane < MOBA_HEAD_DIM * (h + 1))
        off = MOBA_HEAD_DIM * (1 - h)
        qh = jnp.where(hm, q, jnp.zeros_like(q))
        gate = _dot_nt(km, qh)
        valid = cidx < qblk
        gate = jnp.where(valid, gate, -jnp.inf)
        rank = jnp.zeros((GATE_ROWS, S), F32)
        for i in range(nb):
            gi = gate[i:i + 1, :]
            beats = (gi > gate) | ((gi == gate) & (i < cidx))
            rank = rank + beats.astype(F32)
        sel = valid & (rank < float(MOBA_TOPK))
        pen = jnp.where(sel | (cidx >= qblk), 0.0, NEG)
        pads = [jnp.zeros((r, S), F32) for r in (off, LANES - off - GATE_ROWS)]
        pen = jnp.concatenate(([pads[0]] if off else []) + [pen, pads[1]], axis=0)
        pen_q = pen.T.astype(BF16)
        qa.append(jnp.where(hm, q, pen_q))
        onehot = ((lane_s - off) == key_blk).astype(BF16)
        ka.append(jnp.where(hm, k, onehot))
        va.append(jnp.where(hm, v, jnp.ones_like(v)))
        hms.append(hm)

    HH = range(2)
    for n in range(nb):
        s = [_dot_nt(qa[h][n * L:(n + 1) * L], ka[h][:(n + 1) * L]) for h in HH]
        s_own = [jnp.where(causal, s[h][:, n * L:], NEG) for h in HH]
        m = [jnp.max(s_own[h], axis=-1, keepdims=True) for h in HH]
        if n > 0:
            m = [jnp.maximum(m[h], jnp.max(s[h][:, :n * L], axis=-1, keepdims=True)) for h in HH]
            p = [jnp.concatenate([jnp.exp2(s[h][:, :n * L] - m[h]), jnp.exp2(s_own[h] - m[h])],
                                 axis=1).astype(BF16) for h in HH]
        else:
            p = [jnp.exp2(s_own[h] - m[h]).astype(BF16) for h in HH]
        acc = [_dot(p[h], va[h][:(n + 1) * L]) for h in HH]
        outs = [acc[h] / pltpu.roll(acc[h], MOBA_HEAD_DIM, 1) for h in HH]
        o_ref[n * L:(n + 1) * L, :] = jnp.where(hms[0], outs[0], outs[1]).astype(BF16)


def _moba(qkv, B, S):
    W = qkv.shape[-1] // 3
    npair = W // LANES
    return pl.pallas_call(
        functools.partial(_moba_kernel, S=S),
        grid=(B, npair),
        in_specs=[
            pl.BlockSpec((None, S, LANES), lambda b, j: (b, 0, j)),
            pl.BlockSpec((None, S, LANES), lambda b, j: (b, 0, npair + j)),
            pl.BlockSpec((None, S, LANES), lambda b, j: (b, 0, 2 * npair + j)),
        ],
        out_specs=pl.BlockSpec((None, S, LANES), lambda b, j: (b, 0, j)),
        out_shape=jax.ShapeDtypeStruct((B, S, W), BF16),
        compiler_params=pltpu.CompilerParams(
            dimension_semantics=("arbitrary", "arbitrary"), vmem_limit_bytes=VMEM_LIMIT),
        name="moba",
    )(qkv, qkv, qkv)


def _gdn_kernel(gd_ref, ab_ref, prm_ref, gn_ref, o_ref, state_ref, *, heads):
    C = GDN_CHUNK
    Dh = GDN_HEAD_DIM
    W = heads * Dh
    c = pl.program_id(1)

    @pl.when(c == 0)
    def _():
        state_ref[...] = jnp.zeros_like(state_ref)

    gd = gd_ref[...]

    ab = ab_ref[...]
    prm = prm_ref[...]
    sp_in = ab + prm[1:2]
    softplus = jnp.maximum(sp_in, 0.0) + jnp.log1p(jnp.exp(-jnp.abs(sp_in)))
    g = -jnp.exp(prm[0:1]) * softplus
    beta = _sigmoid(ab)
    rows = lax.broadcasted_iota(jnp.int32, (C, LANES), 0)
    G = g
    d = 1
    while d < C:
        G = G + jnp.where(rows >= d, pltpu.roll(G, d, 0), 0.0)
        d *= 2
    GT = G.T
    G_last = G[C - 1:C, :]

    r_i = lax.broadcasted_iota(jnp.int32, (C, C), 0)
    c_i = lax.broadcasted_iota(jnp.int32, (C, C), 1)
    incl = c_i <= r_i
    strict = c_i < r_i
    rc = r_i ^ c_i
    level = jnp.full((C, C), -1, jnp.int32)
    for b in range(C.bit_length() - 1):
        level = level + (rc >= (1 << b)).astype(jnp.int32)
    level_b = level.astype(F32).astype(BF16)
    gn = gn_ref[...]

    H = range(heads)
    q = [gd[:, h * Dh:(h + 1) * Dh] for h in H]
    k = [gd[:, W + h * Dh:W + (h + 1) * Dh] for h in H]
    v = [gd[:, 2 * W + h * Dh:2 * W + (h + 1) * Dh] for h in H]
    Gc = [G[:, h:h + 1] for h in H]
    bc = [beta[:, heads + h:heads + h + 1] for h in H]
    gl = [G_last[:, h:h + 1] for h in H]

    kb = [t.astype(BF16) for t in k]
    decay = [jnp.exp(jnp.where(incl, Gc[h] - GT[h:h + 1, :], -jnp.inf)) for h in H]
    A = [jnp.where(strict, bc[h] * _dot_nt(kb[h], kb[h]) * decay[h], 0.0) for h in H]
    qk = [(_dot_nt(q[h].astype(BF16), kb[h]) * decay[h]).astype(BF16) for h in H]
    gam = [jnp.exp(t) for t in Gc]
    X = [jnp.concatenate([bc[h] * v[h], (bc[h] * gam[h]) * k[h]], axis=1) for h in H]
    Ab = [t.astype(BF16) for t in A]
    zero = jnp.zeros((C, C), BF16)
    Tm = [jnp.where(level_b == -1.0, jnp.ones((C, C), BF16), jnp.where(level_b == 0.0, -t, zero))
          for t in Ab]
    for lv in range(1, C.bit_length() - 1):
        at_lv = level_b == float(lv)
        E = [jnp.where(at_lv, t, zero) for t in Ab]
        F = [_dot(E[h], Tm[h]).astype(BF16) for h in H]
        Tm = [Tm[h] - _dot(Tm[h], F[h]).astype(BF16) for h in H]
    X = [X[h] + _dot(jnp.where(level_b == -1.0, zero, Tm[h]), X[h].astype(BF16)) for h in H]

    S0 = [state_ref[h] for h in H]
    Sb = [t.astype(BF16) for t in S0]
    ub = [(X[h][:, :Dh] - _dot(X[h][:, Dh:].astype(BF16), Sb[h])).astype(BF16) for h in H]
    o = [_dot((q[h] * gam[h]).astype(BF16), Sb[h]) + _dot(qk[h], ub[h]) for h in H]
    k_dec = [(k[h] * jnp.exp(gl[h] - Gc[h])).T.astype(BF16) for h in H]
    for h in H:
        state_ref[h] = jnp.exp(gl[h]) * S0[h] + _dot(k_dec[h], ub[h])
    for h in H:
        z = gd[:, 3 * W + h * Dh:3 * W + (h + 1) * Dh]
        o_ref[:, h * Dh:(h + 1) * Dh] = (_rms(o[h], gn) * _silu(z)).astype(BF16)


def _gdn(gd, ab, prm, gdn_norm, B, S, heads):
    C = GDN_CHUNK
    W = heads * GDN_HEAD_DIM
    return pl.pallas_call(
        functools.partial(_gdn_kernel, heads=heads),
        grid=(B, S // C),
        in_specs=[
            pl.BlockSpec((None, C, 4 * W), lambda b, c: (b, c, 0)),
            pl.BlockSpec((None, C, LANES), lambda b, c: (b, c, 0)),
            pl.BlockSpec((8, LANES), lambda b, c: (0, 0)),
            pl.BlockSpec((1, GDN_HEAD_DIM), lambda b, c: (0, 0)),
        ],
        out_specs=pl.BlockSpec((None, C, W), lambda b, c: (b, c, 0)),
        out_shape=jax.ShapeDtypeStruct((B, S, W), BF16),
        scratch_shapes=[pltpu.VMEM((heads, GDN_HEAD_DIM, GDN_HEAD_DIM), F32)],
        compiler_params=pltpu.CompilerParams(
            dimension_semantics=("arbitrary", "arbitrary"), vmem_limit_bytes=VMEM_LIMIT),
        name="gdn",
    )(gd, ab, prm, gdn_norm)


R_E1, R_E2, R_C1, R_C2, R_RANK1, R_RANK2 = range(6)


def _outproj_kernel(ya_ref, yb_ref, x_ref, wo_ref, fg_ref, wr_ref,
                    x1_ref, route_ref, cnt_ref, carry_ref, wo16_ref):
    @pl.when(pl.program_id(0) == 0)
    def _():
        carry_ref[...] = jnp.zeros_like(carry_ref)
        wo16_ref[...] = wo_ref[...].astype(BF16)

    wa = ya_ref.shape[1]
    ts = ROUTER_SUBTILE
    subs = [pl.ds(j * ts, ts) for j in range(x_ref.shape[0] // ts)]
    each = lambda f, *lists: [f(*args) for args in zip(*lists)]
    rmax = lambda t: jnp.max(t, axis=-1, keepdims=True)
    rmin = lambda t: jnp.min(t, axis=-1, keepdims=True)
    rsum = lambda t: jnp.sum(t, axis=-1, keepdims=True)

    x1 = [x_ref[sl, :] + _dot(ya_ref[sl, :], wo16_ref[:wa, :]) + _dot(yb_ref[sl, :], wo16_ref[wa:, :])
          for sl in subs]
    for sl, t in zip(subs, x1):
        x1_ref[sl, :] = t
    h = each(lambda t: _rms(t, fg_ref[...]).astype(BF16), x1)
    logits = each(lambda t: _dot(t, wr_ref[...]), h)
    lane = lax.broadcasted_iota(jnp.int32, (ts, LANES), 1).astype(F32)
    big = float(LANES)

    is_g = (lane >= N_EXPERTS) & (lane < N_EXPERTS + N_GROUPS)
    lg = each(lambda t: jnp.where(is_g, t, -jnp.inf), logits)
    mg = each(rmax, lg)
    g_sel = each(lambda a, b: rmin(jnp.where(a == b, lane, big)) - N_EXPERTS, lg, mg)
    p_group = each(lambda a, b: 1.0 / rsum(jnp.exp(a - b)), lg, mg)

    le = each(lambda t, g: jnp.where((lane >= g * EXPERTS_PER_GROUP) & (lane < (g + 1) * EXPERTS_PER_GROUP),
                                     t, -jnp.inf), logits, g_sel)
    m1 = each(rmax, le)
    i1 = each(lambda a, b: rmin(jnp.where(a == b, lane, big)), le, m1)
    le2 = each(lambda a, i: jnp.where(lane == i, -jnp.inf, a), le, i1)
    m2 = each(rmax, le2)
    i2 = each(lambda a, b: rmin(jnp.where(a == b, lane, big)), le2, m2)
    se = each(lambda a, b: rsum(jnp.exp(a - b)), le, m1)
    p1 = each(lambda s: 1.0 / s, se)
    p2 = each(lambda a, b, s: jnp.exp(a - b) / s, m2, m1, se)
    c1 = each(lambda pg, a, b: pg * (a / (a + b)), p_group, p1, p2)
    c2 = each(lambda pg, a, b: pg * (b / (a + b)), p_group, p1, p2)

    hot = each(lambda a, b: ((lane == a) | (lane == b)).astype(BF16), i1, i2)
    r_i = lax.broadcasted_iota(jnp.int32, (ts, ts), 0)
    c_i = lax.broadcasted_iota(jnp.int32, (ts, ts), 1)
    before = (c_i < r_i).astype(BF16)
    within = each(lambda t: _dot(before, t), hot)
    totals = each(lambda t: jnp.sum(t.astype(F32), axis=0, keepdims=True), hot)
    base = [carry_ref[0:1, :]]
    for t in totals:
        base.append(base[-1] + t)
    seen = each(lambda a, b: a + b, within, base[:-1])
    rank1 = each(lambda i, s: rsum(jnp.where(lane == i, s, 0.0)), i1, seen)
    rank2 = each(lambda i, s: rsum(jnp.where(lane == i, s, 0.0)), i2, seen)
    carry_ref[...] = jnp.broadcast_to(base[-1], carry_ref.shape)
    cnt_ref[...] = jnp.broadcast_to(base[-1], cnt_ref.shape)

    for j, sl in enumerate(subs):
        rec = jnp.zeros((ts, LANES), F32)
        for slot, val in ((R_E1, i1), (R_E2, i2), (R_C1, c1), (R_C2, c2), (R_RANK1, rank1), (R_RANK2, rank2)):
            rec = jnp.where(lane == float(slot), val[j], rec)
        route_ref[sl, :] = rec


def _outproj(ya, yb, x2d, wo, fgain, wr, tm):
    T, D = x2d.shape
    Wa, Wb = ya.shape[1], yb.shape[1]
    return pl.pallas_call(
        _outproj_kernel,
        grid=(T // tm,),
        in_specs=[
            pl.BlockSpec((tm, Wa), lambda i: (i, 0)),
            pl.BlockSpec((tm, Wb), lambda i: (i, 0)),
            pl.BlockSpec((tm, D), lambda i: (i, 0)),
            pl.BlockSpec((Wa + Wb, D), lambda i: (0, 0)),
            pl.BlockSpec((1, D), lambda i: (0, 0)),
            pl.BlockSpec((D, LANES), lambda i: (0, 0)),
        ],
        out_specs=[
            pl.BlockSpec((tm, D), lambda i: (i, 0)),
            pl.BlockSpec((tm, LANES), lambda i: (i, 0)),
            pl.BlockSpec((8, LANES), lambda i: (0, 0)),
        ],
        out_shape=[
            jax.ShapeDtypeStruct((T, D), F32),
            jax.ShapeDtypeStruct((T, LANES), F32),
            jax.ShapeDtypeStruct((8, LANES), F32),
        ],
        scratch_shapes=[pltpu.VMEM((8, LANES), F32), pltpu.VMEM((Wa + Wb, D), BF16)],
        compiler_params=pltpu.CompilerParams(
            dimension_semantics=("arbitrary",), vmem_limit_bytes=VMEM_LIMIT),
        name="outproj_router",
    )(ya, yb, x2d, wo, fgain, wr)


def _dispatch_kernel(tail_ref, pos_ref, x_ref, xs_ref, zero_ref, sem):
    tm = x_ref.shape[0]
    rt = zero_ref.shape[0]

    @pl.when(pl.program_id(0) == 0)
    def _():
        zero_ref[...] = jnp.zeros_like(zero_ref)

        def tile_fill(j):
            return pltpu.make_async_copy(zero_ref, xs_ref.at[pl.ds(pl.multiple_of(j * rt, 8), rt)], sem)

        def for_each_fill(act):
            for e in range(N_EXPERTS):
                pl.when(tail_ref[e] >= 0)(lambda e=e: act(tile_fill(tail_ref[e])))
            lax.fori_loop(tail_ref[N_EXPERTS], xs_ref.shape[0] // rt, lambda j, c: (act(tile_fill(j)), c)[1], 0)

        for_each_fill(lambda f: f.start())
        for_each_fill(lambda f: f.wait())

    def issue(r, carry):
        for slot in range(2):
            p = pos_ref[0, slot * tm + r]
            pltpu.make_async_copy(x_ref.at[pl.ds(r, 1)], xs_ref.at[pl.ds(p, 1)], sem).start()
        return carry

    lax.fori_loop(0, tm, issue, 0, unroll=8)
    for _ in range(2):
        pltpu.make_async_copy(x_ref, xs_ref.at[pl.ds(0, tm)], sem).wait()


def _dispatch(tails, pos, h, n_rows, tm, rt):
    T, Dp = h.shape
    return pl.pallas_call(
        _dispatch_kernel,
        grid_spec=pltpu.PrefetchScalarGridSpec(
            num_scalar_prefetch=1,
            grid=(T // tm,),
            in_specs=[
                pl.BlockSpec((None, 1, 2 * tm), lambda i, tl: (i, 0, 0), memory_space=pltpu.SMEM),
                pl.BlockSpec((tm, Dp), lambda i, tl: (i, 0)),
            ],
            out_specs=pl.BlockSpec(memory_space=pl.ANY),
            scratch_shapes=[pltpu.VMEM((rt, Dp), h.dtype), pltpu.SemaphoreType.DMA],
        ),
        out_shape=jax.ShapeDtypeStruct((n_rows, Dp), h.dtype),
        compiler_params=pltpu.CompilerParams(
            dimension_semantics=("arbitrary",), vmem_limit_bytes=VMEM_LIMIT),
        name="moe_dispatch",
    )(tails, pos, h)


def _experts_kernel(te_ref, tv_ref, xs_ref, fg_ref, wg_ref, wu_ref, wd_ref, y_ref):
    del te_ref
    i = pl.program_id(0)

    @pl.when(tv_ref[i] > 0)
    def _():
        h = _rms(xs_ref[...], fg_ref[...]).astype(BF16)
        a = _dot(h, wg_ref[...].astype(BF16))
        b = _dot(h, wu_ref[...].astype(BF16))
        y_ref[...] = _dot((_silu(a) * b).astype(BF16), wd_ref[...].astype(BF16))

    @pl.when(tv_ref[i] == 0)
    def _():
        y_ref[...] = jnp.zeros_like(y_ref)


def _experts(tile_expert, tile_valid, xs, fgain, wg, wu, wd, rt):
    n_rows = tile_expert.shape[0] * rt
    Dp = xs.shape[1]
    E, D, F = wg.shape
    return pl.pallas_call(
        _experts_kernel,
        grid_spec=pltpu.PrefetchScalarGridSpec(
            num_scalar_prefetch=2,
            grid=(n_rows // rt,),
            in_specs=[
                pl.BlockSpec((rt, Dp), lambda i, te, tv: (i * tv[i], 0)),
                pl.BlockSpec((1, D), lambda i, te, tv: (0, 0)),
                pl.BlockSpec((None, D, F), lambda i, te, tv: (te[i], 0, 0)),
                pl.BlockSpec((None, D, F), lambda i, te, tv: (te[i], 0, 0)),
                pl.BlockSpec((None, F, D), lambda i, te, tv: (te[i], 0, 0)),
            ],
            out_specs=pl.BlockSpec((rt, D), lambda i, te, tv: (i, 0)),
        ),
        out_shape=jax.ShapeDtypeStruct((n_rows, D), F32),
        compiler_params=pltpu.CompilerParams(
            dimension_semantics=("arbitrary",), vmem_limit_bytes=VMEM_LIMIT),
        name="moe_experts",
    )(tile_expert, tile_valid, xs, fgain, wg, wu, wd)


def _combine_kernel(pos_ref, pos_next_ref, x1_ref, route_ref, fn_ref, y_ref, o_ref, buf_ref, sem):
    tm = x1_ref.shape[0]
    i = pl.program_id(0)
    ring = i % 2

    def row_copy(p_ref, r, slot, rg):
        p = p_ref[0, slot * tm + r]
        return pltpu.make_async_copy(y_ref.at[pl.ds(p, 1)], buf_ref.at[rg, slot, pl.ds(r, 1)], sem.at[rg])

    def issue(p_ref, rg):
        def body(r, carry):
            for slot in range(2):
                row_copy(p_ref, r, slot, rg).start(priority=slot)
            return carry
        lax.fori_loop(0, tm, body, 0, unroll=8)

    @pl.when(i == 0)
    def _():
        issue(pos_ref, ring)

    @pl.when(i + 1 < pl.num_programs(0))
    def _():
        issue(pos_next_ref, 1 - ring)

    for slot in range(2):
        pltpu.make_async_copy(y_ref.at[pl.ds(0, tm)], buf_ref.at[ring, slot], sem.at[ring]).wait()
    route = route_ref[...]
    c1 = route[:, R_C1:R_C1 + 1]
    c2 = route[:, R_C2:R_C2 + 1]
    o_ref[...] = _rms(x1_ref[...] + c1 * buf_ref[ring, 0] + c2 * buf_ref[ring, 1], fn_ref[...])


def _combine(pos, x1, route, fnorm, y, tm):
    T, D = x1.shape
    n = T // tm
    return pl.pallas_call(
        _combine_kernel,
        grid=(n,),
        in_specs=[
            pl.BlockSpec((None, 1, 2 * tm), lambda i: (i, 0, 0), memory_space=pltpu.SMEM),
            pl.BlockSpec((None, 1, 2 * tm), lambda i: (jnp.minimum(i + 1, n - 1), 0, 0),
                         memory_space=pltpu.SMEM),
            pl.BlockSpec((tm, D), lambda i: (i, 0)),
            pl.BlockSpec((tm, LANES), lambda i: (i, 0)),
            pl.BlockSpec((1, D), lambda i: (0, 0)),
            pl.BlockSpec(memory_space=pl.ANY),
        ],
        out_specs=pl.BlockSpec((tm, D), lambda i: (i, 0)),
        out_shape=jax.ShapeDtypeStruct((T, D), F32),
        scratch_shapes=[pltpu.VMEM((2, 2, tm, D), F32), pltpu.SemaphoreType.DMA((2,))],
        compiler_params=pltpu.CompilerParams(
            dimension_semantics=("arbitrary",), vmem_limit_bytes=VMEM_LIMIT),
        name="moe_combine",
    )(pos, pos, x1, route, fnorm, y)


def _pick_tile(T, pref):
    tm = min(pref, T)
    while T % tm:
        tm //= 2
    return tm


def kernel(x, attn_norm, w_in, conv_w, A_log, dt_bias, gdn_norm, w_out, ffn_norm,
           w_group, w_router, w_gate, w_up, w_down, final_norm):
    B, S, D = x.shape
    T = B * S
    depth = w_in.shape[0]
    heads = A_log.shape[1]
    Wb = heads * GDN_HEAD_DIM
    Wa = w_out.shape[1] - Wb
    assert S % MOBA_BLOCK == 0 and S // MOBA_BLOCK <= GATE_ROWS and S % GDN_CHUNK == 0
    assert Wa % LANES == 0 and 2 * heads <= LANES
    assert depth == 1, "the final RMSNorm is fused into the layer's last kernel"

    x2d = x.reshape(T, D)
    for l in range(depth):
        w_all = jnp.pad(w_in[l], ((0, 0), (0, LANES - 2 * heads))).astype(BF16)
        qkv_a, gd, ab = _inproj(x2d, attn_norm[l][None, :], w_all, conv_w[l].astype(F32),
                                3 * Wa, 4 * Wb, _pick_tile(S, 512), S)

        y_a = _moba(qkv_a.reshape(B, S, 3 * Wa), B, S)

        prm = jnp.zeros((8, LANES), F32)
        prm = prm.at[0, :heads].set(A_log[l].astype(F32)).at[1, :heads].set(dt_bias[l].astype(F32))
        y_b = _gdn(gd.reshape(B, S, 4 * Wb), ab.reshape(B, S, LANES), prm,
                   gdn_norm[l][None, :].astype(F32), B, S, heads)

        wr = jnp.concatenate([w_router[l], w_group[l]], axis=1)
        wr = jnp.pad(wr, ((0, 0), (0, LANES - wr.shape[1]))).astype(BF16)
        x1, route, counts = _outproj(y_a.reshape(T, Wa), y_b.reshape(T, Wb), x2d, w_out[l].astype(F32),
                                     ffn_norm[l][None, :], wr, _pick_tile(T, 512))

        rt = MOE_ROW_TILE
        n_tiles = -(-2 * T // rt) + N_EXPERTS
        cnt = counts[0, :N_EXPERTS].astype(jnp.int32)
        seg = -(-cnt // rt) * rt
        seg_end = jnp.cumsum(seg)
        seg_start = seg_end - seg
        tile_row = jnp.arange(n_tiles, dtype=jnp.int32) * rt
        tile_expert = jnp.minimum(jnp.sum(tile_row[:, None] >= seg_end[None, :], axis=1), N_EXPERTS - 1)
        tile_valid = (tile_row < seg_end[-1]).astype(jnp.int32)
        e12 = route[:, R_E1:R_E2 + 1].astype(jnp.int32)
        rank12 = route[:, R_RANK1:R_RANK2 + 1].astype(jnp.int32)
        e_ids = jnp.arange(N_EXPERTS, dtype=jnp.int32)
        pos12 = rank12 + jnp.sum(jnp.where(e12[..., None] == e_ids, seg_start, 0), axis=-1)

        def tiled_pos(tm):
            return pos12.reshape(T // tm, tm, 2).transpose(0, 2, 1).reshape(T // tm, 1, 2 * tm)

        F = w_gate.shape[-1]
        wge = w_gate[l].reshape(N_EXPERTS, D, F)
        wue = w_up[l].reshape(N_EXPERTS, D, F)
        wde = w_down[l].reshape(N_EXPERTS, F, D)
        tmd = _pick_tile(T, 512)
        tails = jnp.concatenate([jnp.where(seg > 0, seg_end // rt - 1, -1), seg_end[-1:] // rt])
        xs = _dispatch(tails, tiled_pos(tmd), x1, n_tiles * rt, tmd, rt)
        ys = _experts(tile_expert.astype(jnp.int32), tile_valid, xs, ffn_norm[l][None, :],
                      wge, wue, wde, rt)
        tmc = _pick_tile(T, 256)
        x2d = _combine(tiled_pos(tmc), x1, route, final_norm[None, :], ys, tmc)
    return x2d.reshape(B, S, D)
```

```python
import functools

import jax
import jax.numpy as jnp
from jax import lax
from jax.experimental import pallas as pl
from jax.experimental.pallas import tpu as pltpu

F32 = jnp.float32
BF16 = jnp.bfloat16

LANES = 128
MOBA_HEAD_DIM = 64
MOBA_BLOCK = 256
MOBA_TOPK = 3
MOBA_Q_SCALE = MOBA_HEAD_DIM ** -0.5 * 1.4426950408889634
GDN_HEAD_DIM = 128
GDN_CONV = 4
GDN_CHUNK = 256
N_GROUPS = 4
EXPERTS_PER_GROUP = 8
N_EXPERTS = N_GROUPS * EXPERTS_PER_GROUP
RMS_EPS = 1e-6
NEG = -1e30
GATE_ROWS = 16
MOE_ROW_TILE = 512
ROUTER_SUBTILE = 128
VMEM_LIMIT = 48 * 1024 * 1024


def _dot(a, b):
    return jnp.dot(a, b, preferred_element_type=F32)


def _dot_nt(a, b):
    return lax.dot_general(a, b, (((1,), (1,)), ((), ())), preferred_element_type=F32)


def _sigmoid(x):
    return 1.0 / (1.0 + jnp.exp(-x))


def _silu(x):
    return x * _sigmoid(x)


def _rms(x, gain):
    return x * lax.rsqrt(jnp.mean(x * x, axis=-1, keepdims=True) + RMS_EPS) * gain


def _inproj_kernel(x_ref, gain_ref, w_ref, cw_ref, om_ref, og_ref, oab_ref, prev_ref, *, tiles_per_seq):
    nm, ng = om_ref.shape[1], og_ref.shape[1]
    tm = x_ref.shape[0]

    @pl.when(pl.program_id(0) % tiles_per_seq == 0)
    def _():
        prev_ref[...] = jnp.zeros_like(prev_ref)

    h = _rms(x_ref[...], gain_ref[...]).astype(BF16)
    og = _dot(h, w_ref[:, nm:nm + ng])
    om = _dot(h, w_ref[:, :nm])
    nq = nm // 3
    om_ref[:, :nq] = (om[:, :nq] * MOBA_Q_SCALE).astype(BF16)
    om_ref[:, nq:] = om[:, nq:].astype(BF16)
    oab_ref[...] = _dot(h, w_ref[:, nm + ng:])

    W = ng // 4
    raw = og[:, :3 * W]
    ext = jnp.concatenate([prev_ref[...], raw], axis=0)
    prev_ref[...] = raw[tm - 8:, :]
    cw = cw_ref[...]
    conv = cw[GDN_CONV - 1:GDN_CONV] * raw
    for d in range(1, GDN_CONV):
        conv = conv + cw[GDN_CONV - 1 - d:GDN_CONV - d] * pltpu.roll(ext, d, 0)[8:]
    qkv = _silu(conv)
    Dh = GDN_HEAD_DIM
    for j in range(2 * W // Dh):
        t = qkv[:, j * Dh:(j + 1) * Dh]
        t = t * lax.rsqrt(jnp.sum(t * t, axis=-1, keepdims=True) + 1e-6)
        og_ref[:, j * Dh:(j + 1) * Dh] = t * (Dh ** -0.5) if j < W // Dh else t
    og_ref[:, 2 * W:3 * W] = qkv[:, 2 * W:]
    og_ref[:, 3 * W:] = og[:, 3 * W:]


def _inproj(x2d, gain, w_all, conv_w, nm, ng, tm, seq_len):
    T, D = x2d.shape
    assert w_all.shape[1] == nm + ng + LANES and nm % LANES == 0 and ng % LANES == 0
    assert seq_len % tm == 0 and tm >= 8
    return pl.pallas_call(
        functools.partial(_inproj_kernel, tiles_per_seq=seq_len // tm),
        grid=(T // tm,),
        in_specs=[
            pl.BlockSpec((tm, D), lambda i: (i, 0)),
            pl.BlockSpec((1, D), lambda i: (0, 0)),
            pl.BlockSpec((D, nm + ng + LANES), lambda i: (0, 0)),
            pl.BlockSpec((GDN_CONV, 3 * ng // 4), lambda i: (0, 0)),
        ],
        out_specs=[
            pl.BlockSpec((tm, nm), lambda i: (i, 0)),
            pl.BlockSpec((tm, ng), lambda i: (i, 0)),
            pl.BlockSpec((tm, LANES), lambda i: (i, 0)),
        ],
        out_shape=[
            jax.ShapeDtypeStruct((T, nm), BF16),
            jax.ShapeDtypeStruct((T, ng), F32),
            jax.ShapeDtypeStruct((T, LANES), F32),
        ],
        scratch_shapes=[pltpu.VMEM((8, 3 * ng // 4), F32)],
        compiler_params=pltpu.CompilerParams(
            dimension_semantics=("arbitrary",), vmem_limit_bytes=VMEM_LIMIT),
        name="inproj",
    )(x2d, gain, w_all, conv_w)


def _moba_kernel(q_ref, k_ref, v_ref, o_ref, *, S):
    nb = S // MOBA_BLOCK
    L = MOBA_BLOCK
    q = q_ref[...]
    k = k_ref[...]
    v = v_ref[...]
    lane = lax.broadcasted_iota(jnp.int32, (1, LANES), 1)
    key_blk = lax.broadcasted_iota(jnp.int32, (S, LANES), 0) // L
    lane_s = lax.broadcasted_iota(jnp.int32, (S, LANES), 1)
    qblk = lax.broadcasted_iota(jnp.int32, (GATE_ROWS, S), 1) // L
    cidx = lax.broadcasted_iota(jnp.int32, (GATE_ROWS, S), 0)
    r_i = lax.broadcasted_iota(jnp.int32, (L, L), 0)
    c_i = lax.broadcasted_iota(jnp.int32, (L, L), 1)
    causal = c_i <= r_i

    km = jnp.mean(k.astype(F32).reshape(nb, L, LANES), axis=1)
    km = jnp.concatenate([km, jnp.zeros((GATE_ROWS - nb, LANES), F32)], axis=0).astype(BF16)

    qa, ka, va, hms = [], [], [], []
    for h in range(2):
        hm = (lane >= MOBA_HEAD_DIM * h) & (lane < MOBA_HEAD_DIM * (h + 1))
        off = MOBA_HEAD_DIM * (1 - h)
        qh = jnp.where(hm, q, jnp.zeros_like(q))
        gate = _dot_nt(km, qh)
        valid = cidx < qblk
        gate = jnp.where(valid, gate, -jnp.inf)
        rank = jnp.zeros((GATE_ROWS, S), F32)
        for i in range(nb):
            gi = gate[i:i + 1, :]
            beats = (gi > gate) | ((gi == gate) & (i < cidx))
            rank = rank + beats.astype(F32)
        sel = valid & (rank < float(MOBA_TOPK))
        pen = jnp.where(sel | (cidx >= qblk), 0.0, NEG)
        pads = [jnp.zeros((r, S), F32) for r in (off, LANES - off - GATE_ROWS)]
        pen = jnp.concatenate(([pads[0]] if off else []) + [pen, pads[1]], axis=0)
        pen_q = pen.T.astype(BF16)
        qa.append(jnp.where(hm, q, pen_q))
        onehot = ((lane_s - off) == key_blk).astype(BF16)
        ka.append(jnp.where(hm, k, onehot))
        va.append(jnp.where(hm, v, jnp.ones_like(v)))
        hms.append(hm)

    HH = range(2)
    for n in range(nb):
        s = [_dot_nt(qa[h][n * L:(n + 1) * L], ka[h][:(n + 1) * L]) for h in HH]
        s_own = [jnp.where(causal, s[h][:, n * L:], NEG) for h in HH]
        m = [jnp.max(s_own[h], axis=-1, keepdims=True) for h in HH]
        if n > 0:
            m = [jnp.maximum(m[h], jnp.max(s[h][:, :n * L], axis=-1, keepdims=True)) for h in HH]
            p = [jnp.concatenate([jnp.exp2(s[h][:, :n * L] - m[h]), jnp.exp2(s_own[h] - m[h])],
                                 axis=1).astype(BF16) for h in HH]
        else:
            p = [jnp.exp2(s_own[h] - m[h]).astype(BF16) for h in HH]
        acc = [_dot(p[h], va[h][:(n + 1) * L]) for h in HH]
        outs = [acc[h] / pltpu.roll(acc[h], MOBA_HEAD_DIM, 1) for h in HH]
        o_ref[n * L:(n + 1) * L, :] = jnp.where(hms[0], outs[0], outs[1]).astype(BF16)


def _moba(qkv, B, S):
    W = qkv.shape[-1] // 3
    npair = W // LANES
    return pl.pallas_call(
        functools.partial(_moba_kernel, S=S),
        grid=(B, npair),
        in_specs=[
            pl.BlockSpec((None, S, LANES), lambda b, j: (b, 0, j)),
            pl.BlockSpec((None, S, LANES), lambda b, j: (b, 0, npair + j)),
            pl.BlockSpec((None, S, LANES), lambda b, j: (b, 0, 2 * npair + j)),
        ],
        out_specs=pl.BlockSpec((None, S, LANES), lambda b, j: (b, 0, j)),
        out_shape=jax.ShapeDtypeStruct((B, S, W), BF16),
        compiler_params=pltpu.CompilerParams(
            dimension_semantics=("arbitrary", "arbitrary"), vmem_limit_bytes=VMEM_LIMIT),
        name="moba",
    )(qkv, qkv, qkv)


def _gdn_kernel(gd_ref, ab_ref, prm_ref, gn_ref, o_ref, state_ref, *, heads):
    C = GDN_CHUNK
    Dh = GDN_HEAD_DIM
    W = heads * Dh
    c = pl.program_id(1)

    @pl.when(c == 0)
    def _():
        state_ref[...] = jnp.zeros_like(state_ref)

    gd = gd_ref[...]

    ab = ab_ref[...]
    prm = prm_ref[...]
    sp_in = ab + prm[1:2]
    softplus = jnp.maximum(sp_in, 0.0) + jnp.log1p(jnp.exp(-jnp.abs(sp_in)))
    g = -jnp.exp(prm[0:1]) * softplus
    beta = _sigmoid(ab)
    rows = lax.broadcasted_iota(jnp.int32, (C, LANES), 0)
    G = g
    d = 1
    while d < C:
        G = G + jnp.where(rows >= d, pltpu.roll(G, d, 0), 0.0)
        d *= 2
    GT = G.T
    G_last = G[C - 1:C, :]

    r_i = lax.broadcasted_iota(jnp.int32, (C, C), 0)
    c_i = lax.broadcasted_iota(jnp.int32, (C, C), 1)
    incl = c_i <= r_i
    strict = c_i < r_i
    rc = r_i ^ c_i
    level = jnp.full((C, C), -1, jnp.int32)
    for b in range(C.bit_length() - 1):
        level = level + (rc >= (1 << b)).astype(jnp.int32)
    level_b = level.astype(F32).astype(BF16)
    gn = gn_ref[...]

    H = range(heads)
    q = [gd[:, h * Dh:(h + 1) * Dh] for h in H]
    k = [gd[:, W + h * Dh:W + (h + 1) * Dh] for h in H]
    v = [gd[:, 2 * W + h * Dh:2 * W + (h + 1) * Dh] for h in H]
    Gc = [G[:, h:h + 1] for h in H]
    bc = [beta[:, heads + h:heads + h + 1] for h in H]
    gl = [G_last[:, h:h + 1] for h in H]

    kb = [t.astype(BF16) for t in k]
    decay = [jnp.exp(jnp.where(incl, Gc[h] - GT[h:h + 1, :], -jnp.inf)) for h in H]
    A = [jnp.where(strict, bc[h] * _dot_nt(kb[h], kb[h]) * decay[h], 0.0) for h in H]
    qk = [(_dot_nt(q[h].astype(BF16), kb[h]) * decay[h]).astype(BF16) for h in H]
    gam = [jnp.exp(t) for t in Gc]
    X = [jnp.concatenate([bc[h] * v[h], (bc[h] * gam[h]) * k[h]], axis=1) for h in H]
    Ab = [t.astype(BF16) for t in A]
    zero = jnp.zeros((C, C), BF16)
    Tm = [jnp.where(level_b == -1.0, jnp.ones((C, C), BF16), jnp.where(level_b == 0.0, -t, zero))
          for t in Ab]
    for lv in range(1, C.bit_length() - 1):
        at_lv = level_b == float(lv)
        E = [jnp.where(at_lv, t, zero) for t in Ab]
        F = [_dot(E[h], Tm[h]).astype(BF16) for h in H]
        Tm = [Tm[h] - _dot(Tm[h], F[h]).astype(BF16) for h in H]
    X = [X[h] + _dot(jnp.where(level_b == -1.0, zero, Tm[h]), X[h].astype(BF16)) for h in H]

    S0 = [state_ref[h] for h in H]
    Sb = [t.astype(BF16) for t in S0]
    ub = [(X[h][:, :Dh] - _dot(X[h][:, Dh:].astype(BF16), Sb[h])).astype(BF16) for h in H]
    o = [_dot((q[h] * gam[h]).astype(BF16), Sb[h]) + _dot(qk[h], ub[h]) for h in H]
    k_dec = [(k[h] * jnp.exp(gl[h] - Gc[h])).T.astype(BF16) for h in H]
    for h in H:
        state_ref[h] = jnp.exp(gl[h]) * S0[h] + _dot(k_dec[h], ub[h])
    for h in H:
        z = gd[:, 3 * W + h * Dh:3 * W + (h + 1) * Dh]
        o_ref[:, h * Dh:(h + 1) * Dh] = (_rms(o[h], gn) * _silu(z)).astype(BF16)


def _gdn(gd, ab, prm, gdn_norm, B, S, heads):
    C = GDN_CHUNK
    W = heads * GDN_HEAD_DIM
    return pl.pallas_call(
        functools.partial(_gdn_kernel, heads=heads),
        grid=(B, S // C),
        in_specs=[
            pl.BlockSpec((None, C, 4 * W), lambda b, c: (b, c, 0)),
            pl.BlockSpec((None, C, LANES), lambda b, c: (b, c, 0)),
            pl.BlockSpec((8, LANES), lambda b, c: (0, 0)),
            pl.BlockSpec((1, GDN_HEAD_DIM), lambda b, c: (0, 0)),
        ],
        out_specs=pl.BlockSpec((None, C, W), lambda b, c: (b, c, 0)),
        out_shape=jax.ShapeDtypeStruct((B, S, W), BF16),
        scratch_shapes=[pltpu.VMEM((heads, GDN_HEAD_DIM, GDN_HEAD_DIM), F32)],
        compiler_params=pltpu.CompilerParams(
            dimension_semantics=("arbitrary", "arbitrary"), vmem_limit_bytes=VMEM_LIMIT),
        name="gdn",
    )(gd, ab, prm, gdn_norm)


R_E1, R_E2, R_C1, R_C2, R_RANK1, R_RANK2 = range(6)


def _outproj_kernel(ya_ref, yb_ref, x_ref, wo_ref, fg_ref, wr_ref,
                    x1_ref, route_ref, route_t_ref, cnt_ref, carry_ref, wo16_ref):
    @pl.when(pl.program_id(0) == 0)
    def _():
        carry_ref[...] = jnp.zeros_like(carry_ref)
        wo16_ref[...] = wo_ref[...].astype(BF16)

    wa = ya_ref.shape[1]
    ts = ROUTER_SUBTILE
    subs = [pl.ds(j * ts, ts) for j in range(x_ref.shape[0] // ts)]
    each = lambda f, *lists: [f(*args) for args in zip(*lists)]
    rmax = lambda t: jnp.max(t, axis=-1, keepdims=True)
    rmin = lambda t: jnp.min(t, axis=-1, keepdims=True)
    rsum = lambda t: jnp.sum(t, axis=-1, keepdims=True)

    x1 = [x_ref[sl, :] + _dot(ya_ref[sl, :], wo16_ref[:wa, :]) + _dot(yb_ref[sl, :], wo16_ref[wa:, :])
          for sl in subs]
    for sl, t in zip(subs, x1):
        x1_ref[sl, :] = t
    h = each(lambda t: _rms(t, fg_ref[...]).astype(BF16), x1)
    logits = each(lambda t: _dot(t, wr_ref[...]), h)
    lane = lax.broadcasted_iota(jnp.int32, (ts, LANES), 1).astype(F32)
    big = float(LANES)

    is_g = (lane >= N_EXPERTS) & (lane < N_EXPERTS + N_GROUPS)
    lg = each(lambda t: jnp.where(is_g, t, -jnp.inf), logits)
    mg = each(rmax, lg)
    g_sel = each(lambda a, b: rmin(jnp.where(a == b, lane, big)) - N_EXPERTS, lg, mg)
    p_group = each(lambda a, b: 1.0 / rsum(jnp.exp(a - b)), lg, mg)

    le = each(lambda t, g: jnp.where((lane >= g * EXPERTS_PER_GROUP) & (lane < (g + 1) * EXPERTS_PER_GROUP),
                                     t, -jnp.inf), logits, g_sel)
    m1 = each(rmax, le)
    i1 = each(lambda a, b: rmin(jnp.where(a == b, lane, big)), le, m1)
    le2 = each(lambda a, i: jnp.where(lane == i, -jnp.inf, a), le, i1)
    m2 = each(rmax, le2)
    i2 = each(lambda a, b: rmin(jnp.where(a == b, lane, big)), le2, m2)
    se = each(lambda a, b: rsum(jnp.exp(a - b)), le, m1)
    p1 = each(lambda s: 1.0 / s, se)
    p2 = each(lambda a, b, s: jnp.exp(a - b) / s, m2, m1, se)
    c1 = each(lambda pg, a, b: pg * (a / (a + b)), p_group, p1, p2)
    c2 = each(lambda pg, a, b: pg * (b / (a + b)), p_group, p1, p2)

    hot = each(lambda a, b: ((lane == a) | (lane == b)).astype(BF16), i1, i2)
    r_i = lax.broadcasted_iota(jnp.int32, (ts, ts), 0)
    c_i = lax.broadcasted_iota(jnp.int32, (ts, ts), 1)
    before = (c_i < r_i).astype(BF16)
    within = each(lambda t: _dot(before, t), hot)
    totals = each(lambda t: jnp.sum(t.astype(F32), axis=0, keepdims=True), hot)
    base = [carry_ref[0:1, :]]
    for t in totals:
        base.append(base[-1] + t)
    seen = each(lambda a, b: a + b, within, base[:-1])
    rank1 = each(lambda i, s: rsum(jnp.where(lane == i, s, 0.0)), i1, seen)
    rank2 = each(lambda i, s: rsum(jnp.where(lane == i, s, 0.0)), i2, seen)
    carry_ref[...] = jnp.broadcast_to(base[-1], carry_ref.shape)
    cnt_ref[...] = jnp.broadcast_to(base[-1], cnt_ref.shape)

    for j, sl in enumerate(subs):
        rec = jnp.zeros((ts, LANES), F32)
        for slot, val in ((R_E1, i1), (R_E2, i2), (R_C1, c1), (R_C2, c2), (R_RANK1, rank1), (R_RANK2, rank2)):
            rec = jnp.where(lane == float(slot), val[j], rec)
        route_ref[sl, :] = rec
        route_t_ref[:, sl] = rec.T[0:8, :]


def _outproj(ya, yb, x2d, wo, fgain, wr, tm):
    T, D = x2d.shape
    Wa, Wb = ya.shape[1], yb.shape[1]
    return pl.pallas_call(
        _outproj_kernel,
        grid=(T // tm,),
        in_specs=[
            pl.BlockSpec((tm, Wa), lambda i: (i, 0)),
            pl.BlockSpec((tm, Wb), lambda i: (i, 0)),
            pl.BlockSpec((tm, D), lambda i: (i, 0)),
            pl.BlockSpec((Wa + Wb, D), lambda i: (0, 0)),
            pl.BlockSpec((1, D), lambda i: (0, 0)),
            pl.BlockSpec((D, LANES), lambda i: (0, 0)),
        ],
        out_specs=[
            pl.BlockSpec((tm, D), lambda i: (i, 0)),
            pl.BlockSpec((tm, LANES), lambda i: (i, 0)),
            pl.BlockSpec((8, tm), lambda i: (0, i)),
            pl.BlockSpec((8, LANES), lambda i: (0, 0)),
        ],
        out_shape=[
            jax.ShapeDtypeStruct((T, D), F32),
            jax.ShapeDtypeStruct((T, LANES), F32),
            jax.ShapeDtypeStruct((8, T), F32),
            jax.ShapeDtypeStruct((8, LANES), F32),
        ],
        scratch_shapes=[pltpu.VMEM((8, LANES), F32), pltpu.VMEM((Wa + Wb, D), BF16)],
        compiler_params=pltpu.CompilerParams(
            dimension_semantics=("arbitrary",), vmem_limit_bytes=VMEM_LIMIT),
        name="outproj_router",
    )(ya, yb, x2d, wo, fgain, wr)


def _dispatch_kernel(tail_ref, pos_ref, x_ref, xs_ref, zero_ref, sem):
    tm = x_ref.shape[0]
    rt = zero_ref.shape[0]

    @pl.when(pl.program_id(0) == 0)
    def _():
        zero_ref[...] = jnp.zeros_like(zero_ref)

        def tile_fill(j):
            return pltpu.make_async_copy(zero_ref, xs_ref.at[pl.ds(pl.multiple_of(j * rt, 8), rt)], sem)

        def for_each_fill(act):
            for e in range(N_EXPERTS):
                pl.when(tail_ref[e] >= 0)(lambda e=e: act(tile_fill(tail_ref[e])))
            lax.fori_loop(tail_ref[N_EXPERTS], xs_ref.shape[0] // rt, lambda j, c: (act(tile_fill(j)), c)[1], 0)

        for_each_fill(lambda f: f.start())
        for_each_fill(lambda f: f.wait())

    def issue(r, carry):
        for slot in range(2):
            p = pos_ref[0, slot * tm + r]
            pltpu.make_async_copy(x_ref.at[pl.ds(r, 1)], xs_ref.at[pl.ds(p, 1)], sem).start()
        return carry

    lax.fori_loop(0, tm, issue, 0, unroll=8)
    for _ in range(2):
        pltpu.make_async_copy(x_ref, xs_ref.at[pl.ds(0, tm)], sem).wait()


def _dispatch(tails, pos, h, n_rows, tm, rt):
    T, Dp = h.shape
    return pl.pallas_call(
        _dispatch_kernel,
        grid_spec=pltpu.PrefetchScalarGridSpec(
            num_scalar_prefetch=1,
            grid=(T // tm,),
            in_specs=[
                pl.BlockSpec((None, 1, 2 * tm), lambda i, tl: (i, 0, 0), memory_space=pltpu.SMEM),
                pl.BlockSpec((tm, Dp), lambda i, tl: (i, 0)),
            ],
            out_specs=pl.BlockSpec(memory_space=pl.ANY),
            scratch_shapes=[pltpu.VMEM((rt, Dp), h.dtype), pltpu.SemaphoreType.DMA],
        ),
        out_shape=jax.ShapeDtypeStruct((n_rows, Dp), h.dtype),
        compiler_params=pltpu.CompilerParams(
            dimension_semantics=("arbitrary",), vmem_limit_bytes=VMEM_LIMIT),
        name="moe_dispatch",
    )(tails, pos, h)


def _experts_kernel(te_ref, tv_ref, xs_ref, fg_ref, wg_ref, wu_ref, wd_ref, y_ref):
    del te_ref
    i = pl.program_id(0)

    @pl.when(tv_ref[i] > 0)
    def _():
        h = _rms(xs_ref[...], fg_ref[...]).astype(BF16)
        a = _dot(h, wg_ref[...].astype(BF16))
        b = _dot(h, wu_ref[...].astype(BF16))
        y_ref[...] = _dot((_silu(a) * b).astype(BF16), wd_ref[...].astype(BF16))

    @pl.when(tv_ref[i] == 0)
    def _():
        y_ref[...] = jnp.zeros_like(y_ref)


def _experts(tile_expert, tile_valid, xs, fgain, wg, wu, wd, rt):
    n_rows = tile_expert.shape[0] * rt
    Dp = xs.shape[1]
    E, D, F = wg.shape
    return pl.pallas_call(
        _experts_kernel,
        grid_spec=pltpu.PrefetchScalarGridSpec(
            num_scalar_prefetch=2,
            grid=(n_rows // rt,),
            in_specs=[
                pl.BlockSpec((rt, Dp), lambda i, te, tv: (i * tv[i], 0)),
                pl.BlockSpec((1, D), lambda i, te, tv: (0, 0)),
                pl.BlockSpec((None, D, F), lambda i, te, tv: (te[i], 0, 0)),
                pl.BlockSpec((None, D, F), lambda i, te, tv: (te[i], 0, 0)),
                pl.BlockSpec((None, F, D), lambda i, te, tv: (te[i], 0, 0)),
            ],
            out_specs=pl.BlockSpec((rt, D), lambda i, te, tv: (i, 0)),
        ),
        out_shape=jax.ShapeDtypeStruct((n_rows, D), F32),
        compiler_params=pltpu.CompilerParams(
            dimension_semantics=("arbitrary",), vmem_limit_bytes=VMEM_LIMIT),
        name="moe_experts",
    )(tile_expert, tile_valid, xs, fgain, wg, wu, wd)


def _combine_kernel(pos_ref, pos_next_ref, x1_ref, route_ref, fn_ref, y_ref, o_ref, buf_ref, sem):
    tm = x1_ref.shape[0]
    i = pl.program_id(0)
    ring = i % 2

    def row_copy(p_ref, r, slot, rg):
        p = p_ref[0, slot * tm + r]
        return pltpu.make_async_copy(y_ref.at[pl.ds(p, 1)], buf_ref.at[rg, slot, pl.ds(r, 1)], sem.at[rg])

    def issue(p_ref, rg):
        def body(r, carry):
            for slot in range(2):
                row_copy(p_ref, r, slot, rg).start(priority=slot)
            return carry
        lax.fori_loop(0, tm, body, 0, unroll=8)

    @pl.when(i == 0)
    def _():
        issue(pos_ref, ring)

    @pl.when(i + 1 < pl.num_programs(0))
    def _():
        issue(pos_next_ref, 1 - ring)

    for slot in range(2):
        pltpu.make_async_copy(y_ref.at[pl.ds(0, tm)], buf_ref.at[ring, slot], sem.at[ring]).wait()
    route = route_ref[...]
    c1 = route[:, R_C1:R_C1 + 1]
    c2 = route[:, R_C2:R_C2 + 1]
    o_ref[...] = _rms(x1_ref[...] + c1 * buf_ref[ring, 0] + c2 * buf_ref[ring, 1], fn_ref[...])


def _combine(pos, x1, route, fnorm, y, tm):
    T, D = x1.shape
    n = T // tm
    return pl.pallas_call(
        _combine_kernel,
        grid=(n,),
        in_specs=[
            pl.BlockSpec((None, 1, 2 * tm), lambda i: (i, 0, 0), memory_space=pltpu.SMEM),
            pl.BlockSpec((None, 1, 2 * tm), lambda i: (jnp.minimum(i + 1, n - 1), 0, 0),
                         memory_space=pltpu.SMEM),
            pl.BlockSpec((tm, D), lambda i: (i, 0)),
            pl.BlockSpec((tm, LANES), lambda i: (i, 0)),
            pl.BlockSpec((1, D), lambda i: (0, 0)),
            pl.BlockSpec(memory_space=pl.ANY),
        ],
        out_specs=pl.BlockSpec((tm, D), lambda i: (i, 0)),
        out_shape=jax.ShapeDtypeStruct((T, D), F32),
        scratch_shapes=[pltpu.VMEM((2, 2, tm, D), F32), pltpu.SemaphoreType.DMA((2,))],
        compiler_params=pltpu.CompilerParams(
            dimension_semantics=("arbitrary",), vmem_limit_bytes=VMEM_LIMIT),
        name="moe_combine",
    )(pos, pos, x1, route, fnorm, y)


def _pick_tile(T, pref):
    tm = min(pref, T)
    while T % tm:
        tm //= 2
    return tm


def kernel(x, attn_norm, w_in, conv_w, A_log, dt_bias, gdn_norm, w_out, ffn_norm,
           w_group, w_router, w_gate, w_up, w_down, final_norm):
    B, S, D = x.shape
    T = B * S
    depth = w_in.shape[0]
    heads = A_log.shape[1]
    Wb = heads * GDN_HEAD_DIM
    Wa = w_out.shape[1] - Wb
    assert S % MOBA_BLOCK == 0 and S // MOBA_BLOCK <= GATE_ROWS and S % GDN_CHUNK == 0
    assert Wa % LANES == 0 and 2 * heads <= LANES
    assert depth == 1, "the final RMSNorm is fused into the layer's last kernel"

    x2d = x.reshape(T, D)
    for l in range(depth):
        w_all = jnp.pad(w_in[l], ((0, 0), (0, LANES - 2 * heads))).astype(BF16)
        qkv_a, gd, ab = _inproj(x2d, attn_norm[l][None, :], w_all, conv_w[l].astype(F32),
                                3 * Wa, 4 * Wb, _pick_tile(S, 512), S)

        y_a = _moba(qkv_a.reshape(B, S, 3 * Wa), B, S)

        prm = jnp.zeros((8, LANES), F32)
        prm = prm.at[0, :heads].set(A_log[l].astype(F32)).at[1, :heads].set(dt_bias[l].astype(F32))
        y_b = _gdn(gd.reshape(B, S, 4 * Wb), ab.reshape(B, S, LANES), prm,
                   gdn_norm[l][None, :].astype(F32), B, S, heads)

        wr = jnp.concatenate([w_router[l], w_group[l]], axis=1)
        wr = jnp.pad(wr, ((0, 0), (0, LANES - wr.shape[1]))).astype(BF16)
        x1, route, route_t, counts = _outproj(y_a.reshape(T, Wa), y_b.reshape(T, Wb), x2d,
                                              w_out[l].astype(F32), ffn_norm[l][None, :], wr, _pick_tile(T, 512))

        rt = MOE_ROW_TILE
        n_tiles = -(-2 * T // rt) + N_EXPERTS
        cnt = counts[0, :N_EXPERTS].astype(jnp.int32)
        seg = -(-cnt // rt) * rt
        seg_end = jnp.cumsum(seg)
        seg_start = seg_end - seg
        tile_row = jnp.arange(n_tiles, dtype=jnp.int32) * rt
        tile_expert = jnp.minimum(jnp.sum(tile_row[:, None] >= seg_end[None, :], axis=1), N_EXPERTS - 1)
        tile_valid = (tile_row < seg_end[-1]).astype(jnp.int32)
        e_ids = jnp.arange(N_EXPERTS, dtype=jnp.int32)[:, None]

        def dest_rows(e_lane, rank_lane):
            e = route_t[e_lane].astype(jnp.int32)
            return route_t[rank_lane].astype(jnp.int32) + jnp.sum(
                jnp.where(e[None, :] == e_ids, seg_start[:, None], 0), axis=0)

        pos1, pos2 = dest_rows(R_E1, R_RANK1), dest_rows(R_E2, R_RANK2)

        def tiled_pos(tm):
            return jnp.concatenate([pos1.reshape(T // tm, 1, tm), pos2.reshape(T // tm, 1, tm)], axis=2)

        F = w_gate.shape[-1]
        wge = w_gate[l].reshape(N_EXPERTS, D, F)
        wue = w_up[l].reshape(N_EXPERTS, D, F)
        wde = w_down[l].reshape(N_EXPERTS, F, D)
        tmd = _pick_tile(T, 512)
        tails = jnp.concatenate([jnp.where(seg > 0, seg_end // rt - 1, -1), seg_end[-1:] // rt])
        xs = _dispatch(tails, tiled_pos(tmd), x1, n_tiles * rt, tmd, rt)
        ys = _experts(tile_expert.astype(jnp.int32), tile_valid, xs, ffn_norm[l][None, :],
                      wge, wue, wde, rt)
        tmc = _pick_tile(T, 512)
        x2d = _combine(tiled_pos(tmc), x1, route, final_norm[None, :], ys, tmc)
    return x2d.reshape(B, S, D)
```

```python
import functools

import jax
import jax.numpy as jnp
from jax import lax
from jax.experimental import pallas as pl
from jax.experimental.pallas import tpu as pltpu

F32 = jnp.float32
BF16 = jnp.bfloat16

LANES = 128
MOBA_HEAD_DIM = 64
MOBA_BLOCK = 256
MOBA_TOPK = 3
MOBA_Q_SCALE = MOBA_HEAD_DIM ** -0.5 * 1.4426950408889634
GDN_HEAD_DIM = 128
GDN_CONV = 4
GDN_CHUNK = 256
GDN_BATCH_ROWS = 2
N_GROUPS = 4
EXPERTS_PER_GROUP = 8
N_EXPERTS = N_GROUPS * EXPERTS_PER_GROUP
RMS_EPS = 1e-6
NEG = -1e30
GATE_ROWS = 16
MOE_ROW_TILE = 512
ROUTER_SUBTILE = 128
VMEM_LIMIT = 48 * 1024 * 1024


def _dot(a, b):
    return jnp.dot(a, b, preferred_element_type=F32)


def _dot_nt(a, b):
    return lax.dot_general(a, b, (((1,), (1,)), ((), ())), preferred_element_type=F32)


def _sigmoid(x):
    return 1.0 / (1.0 + jnp.exp(-x))


def _silu(x):
    return x * _sigmoid(x)


def _rms(x, gain):
    return x * lax.rsqrt(jnp.mean(x * x, axis=-1, keepdims=True) + RMS_EPS) * gain


def _inproj_kernel(x_ref, gain_ref, w_ref, cw_ref, om_ref, og_ref, oab_ref, prev_ref, *, tiles_per_seq):
    nm, ng = om_ref.shape[1], og_ref.shape[1]
    tm = x_ref.shape[0]

    @pl.when(pl.program_id(0) % tiles_per_seq == 0)
    def _():
        prev_ref[...] = jnp.zeros_like(prev_ref)

    h = _rms(x_ref[...], gain_ref[...]).astype(BF16)
    og = _dot(h, w_ref[:, nm:nm + ng])
    om = _dot(h, w_ref[:, :nm])
    nq = nm // 3
    om_ref[:, :nq] = (om[:, :nq] * MOBA_Q_SCALE).astype(BF16)
    om_ref[:, nq:] = om[:, nq:].astype(BF16)
    oab_ref[...] = _dot(h, w_ref[:, nm + ng:])

    W = ng // 4
    raw = og[:, :3 * W]
    ext = jnp.concatenate([prev_ref[...], raw], axis=0)
    prev_ref[...] = raw[tm - 8:, :]
    cw = cw_ref[...]
    conv = cw[GDN_CONV - 1:GDN_CONV] * raw
    for d in range(1, GDN_CONV):
        conv = conv + cw[GDN_CONV - 1 - d:GDN_CONV - d] * pltpu.roll(ext, d, 0)[8:]
    qkv = _silu(conv)
    Dh = GDN_HEAD_DIM
    for j in range(2 * W // Dh):
        t = qkv[:, j * Dh:(j + 1) * Dh]
        t = t * lax.rsqrt(jnp.sum(t * t, axis=-1, keepdims=True) + 1e-6)
        og_ref[:, j * Dh:(j + 1) * Dh] = t * (Dh ** -0.5) if j < W // Dh else t
    og_ref[:, 2 * W:3 * W] = qkv[:, 2 * W:]
    og_ref[:, 3 * W:] = og[:, 3 * W:]


def _inproj(x2d, gain, w_all, conv_w, nm, ng, tm, seq_len):
    T, D = x2d.shape
    assert w_all.shape[1] == nm + ng + LANES and nm % LANES == 0 and ng % LANES == 0
    assert seq_len % tm == 0 and tm >= 8
    return pl.pallas_call(
        functools.partial(_inproj_kernel, tiles_per_seq=seq_len // tm),
        grid=(T // tm,),
        in_specs=[
            pl.BlockSpec((tm, D), lambda i: (i, 0)),
            pl.BlockSpec((1, D), lambda i: (0, 0)),
            pl.BlockSpec((D, nm + ng + LANES), lambda i: (0, 0)),
            pl.BlockSpec((GDN_CONV, 3 * ng // 4), lambda i: (0, 0)),
        ],
        out_specs=[
            pl.BlockSpec((tm, nm), lambda i: (i, 0)),
            pl.BlockSpec((tm, ng), lambda i: (i, 0)),
            pl.BlockSpec((tm, LANES), lambda i: (i, 0)),
        ],
        out_shape=[
            jax.ShapeDtypeStruct((T, nm), BF16),
            jax.ShapeDtypeStruct((T, ng), F32),
            jax.ShapeDtypeStruct((T, LANES), F32),
        ],
        scratch_shapes=[pltpu.VMEM((8, 3 * ng // 4), F32)],
        compiler_params=pltpu.CompilerParams(
            dimension_semantics=("arbitrary",), vmem_limit_bytes=VMEM_LIMIT),
        name="inproj",
    )(x2d, gain, w_all, conv_w)


def _moba_kernel(q_ref, k_ref, v_ref, o_ref, *, S):
    nb = S // MOBA_BLOCK
    L = MOBA_BLOCK
    q = q_ref[...]
    k = k_ref[...]
    v = v_ref[...]
    lane = lax.broadcasted_iota(jnp.int32, (1, LANES), 1)
    key_blk = lax.broadcasted_iota(jnp.int32, (S, LANES), 0) // L
    lane_s = lax.broadcasted_iota(jnp.int32, (S, LANES), 1)
    qblk = lax.broadcasted_iota(jnp.int32, (GATE_ROWS, S), 1) // L
    cidx = lax.broadcasted_iota(jnp.int32, (GATE_ROWS, S), 0)
    r_i = lax.broadcasted_iota(jnp.int32, (L, L), 0)
    c_i = lax.broadcasted_iota(jnp.int32, (L, L), 1)
    causal = c_i <= r_i

    km = jnp.mean(k.astype(F32).reshape(nb, L, LANES), axis=1)
    km = jnp.concatenate([km, jnp.zeros((GATE_ROWS - nb, LANES), F32)], axis=0).astype(BF16)

    qa, ka, va, hms = [], [], [], []
    for h in range(2):
        hm = (lane >= MOBA_HEAD_DIM * h) & (lane < MOBA_HEAD_DIM * (h + 1))
        off = MOBA_HEAD_DIM * (1 - h)
        qh = jnp.where(hm, q, jnp.zeros_like(q))
        gate = _dot_nt(km, qh)
        valid = cidx < qblk
        gate = jnp.where(valid, gate, -jnp.inf)
        rank = jnp.zeros((GATE_ROWS, S), F32)
        for i in range(nb):
            gi = gate[i:i + 1, :]
            beats = (gi > gate) | ((gi == gate) & (i < cidx))
            rank = rank + beats.astype(F32)
        sel = valid & (rank < float(MOBA_TOPK))
        pen = jnp.where(sel | (cidx >= qblk), 0.0, NEG)
        pads = [jnp.zeros((r, S), F32) for r in (off, LANES - off - GATE_ROWS)]
        pen = jnp.concatenate(([pads[0]] if off else []) + [pen, pads[1]], axis=0)
        pen_q = pen.T.astype(BF16)
        qa.append(jnp.where(hm, q, pen_q))
        onehot = ((lane_s - off) == key_blk).astype(BF16)
        ka.append(jnp.where(hm, k, onehot))
        va.append(jnp.where(hm, v, jnp.ones_like(v)))
        hms.append(hm)

    HH = range(2)
    for n in range(nb):
        s = [_dot_nt(qa[h][n * L:(n + 1) * L], ka[h][:(n + 1) * L]) for h in HH]
        s_own = [jnp.where(causal, s[h][:, n * L:], NEG) for h in HH]
        m = [jnp.max(s_own[h], axis=-1, keepdims=True) for h in HH]
        if n > 0:
            m = [jnp.maximum(m[h], jnp.max(s[h][:, :n * L], axis=-1, keepdims=True)) for h in HH]
            p = [jnp.concatenate([jnp.exp2(s[h][:, :n * L] - m[h]), jnp.exp2(s_own[h] - m[h])],
                                 axis=1).astype(BF16) for h in HH]
        else:
            p = [jnp.exp2(s_own[h] - m[h]).astype(BF16) for h in HH]
        acc = [_dot(p[h], va[h][:(n + 1) * L]) for h in HH]
        outs = [acc[h] / pltpu.roll(acc[h], MOBA_HEAD_DIM, 1) for h in HH]
        o_ref[n * L:(n + 1) * L, :] = jnp.where(hms[0], outs[0], outs[1]).astype(BF16)


def _moba(qkv, B, S):
    W = qkv.shape[-1] // 3
    npair = W // LANES
    return pl.pallas_call(
        functools.partial(_moba_kernel, S=S),
        grid=(B, npair),
        in_specs=[
            pl.BlockSpec((None, S, LANES), lambda b, j: (b, 0, j)),
            pl.BlockSpec((None, S, LANES), lambda b, j: (b, 0, npair + j)),
            pl.BlockSpec((None, S, LANES), lambda b, j: (b, 0, 2 * npair + j)),
        ],
        out_specs=pl.BlockSpec((None, S, LANES), lambda b, j: (b, 0, j)),
        out_shape=jax.ShapeDtypeStruct((B, S, W), BF16),
        compiler_params=pltpu.CompilerParams(
            dimension_semantics=("arbitrary", "arbitrary"), vmem_limit_bytes=VMEM_LIMIT),
        name="moba",
    )(qkv, qkv, qkv)


def _gdn_kernel(gd_ref, ab_ref, prm_ref, gn_ref, o_ref, state_ref, *, heads):
    C = GDN_CHUNK
    Dh = GDN_HEAD_DIM
    W = heads * Dh
    c = pl.program_id(1)

    @pl.when(c == 0)
    def _():
        state_ref[...] = jnp.zeros_like(state_ref)

    nbatch = gd_ref.shape[0]
    prm = prm_ref[...]
    rows = lax.broadcasted_iota(jnp.int32, (C, LANES), 0)
    gds, Gs, GTs, betas = [], [], [], []
    for bi in range(nbatch):
        gds.append(gd_ref[bi])
        ab = ab_ref[bi]
        sp_in = ab + prm[1:2]
        softplus = jnp.maximum(sp_in, 0.0) + jnp.log1p(jnp.exp(-jnp.abs(sp_in)))
        G = -jnp.exp(prm[0:1]) * softplus
        betas.append(_sigmoid(ab))
        d = 1
        while d < C:
            G = G + jnp.where(rows >= d, pltpu.roll(G, d, 0), 0.0)
            d *= 2
        Gs.append(G)
        GTs.append(G.T)

    r_i = lax.broadcasted_iota(jnp.int32, (C, C), 0)
    c_i = lax.broadcasted_iota(jnp.int32, (C, C), 1)
    incl = c_i <= r_i
    strict = c_i < r_i
    rc = r_i ^ c_i
    level = jnp.full((C, C), -1, jnp.int32)
    for b in range(C.bit_length() - 1):
        level = level + (rc >= (1 << b)).astype(jnp.int32)
    level_b = level.astype(F32).astype(BF16)
    gn = gn_ref[...]

    chains = [(bi, hd) for bi in range(nbatch) for hd in range(heads)]
    H = range(len(chains))
    q = [gds[bi][:, hd * Dh:(hd + 1) * Dh] for bi, hd in chains]
    k = [gds[bi][:, W + hd * Dh:W + (hd + 1) * Dh] for bi, hd in chains]
    v = [gds[bi][:, 2 * W + hd * Dh:2 * W + (hd + 1) * Dh] for bi, hd in chains]
    Gc = [Gs[bi][:, hd:hd + 1] for bi, hd in chains]
    Gr = [GTs[bi][hd:hd + 1, :] for bi, hd in chains]
    bc = [betas[bi][:, heads + hd:heads + hd + 1] for bi, hd in chains]
    gl = [Gs[bi][C - 1:C, hd:hd + 1] for bi, hd in chains]

    kb = [t.astype(BF16) for t in k]
    decay = [jnp.exp(jnp.where(incl, Gc[h] - Gr[h], -jnp.inf)) for h in H]
    A = [jnp.where(strict, bc[h] * _dot_nt(kb[h], kb[h]) * decay[h], 0.0) for h in H]
    qk = [(_dot_nt(q[h].astype(BF16), kb[h]) * decay[h]).astype(BF16) for h in H]
    gam = [jnp.exp(t) for t in Gc]
    X = [jnp.concatenate([bc[h] * v[h], (bc[h] * gam[h]) * k[h]], axis=1) for h in H]
    Ab = [t.astype(BF16) for t in A]
    zero = jnp.zeros((C, C), BF16)
    Tm = [jnp.where(level_b == -1.0, jnp.ones((C, C), BF16), jnp.where(level_b == 0.0, -t, zero))
          for t in Ab]
    for lv in range(1, C.bit_length() - 2):
        at_lv = level_b == float(lv)
        E = [jnp.where(at_lv, t, zero) for t in Ab]
        F = [_dot(E[h], Tm[h]).astype(BF16) for h in H]
        Tm = [Tm[h] - _dot(Tm[h], F[h]).astype(BF16) for h in H]
    hc = C // 2
    F21 = [_dot(Ab[h][hc:, :hc], Tm[h][:hc, :hc]).astype(BF16) for h in H]
    T21 = [-_dot(Tm[h][hc:, hc:], F21[h]).astype(BF16) for h in H]
    Tm = [jnp.concatenate([Tm[h][:hc, :], jnp.concatenate([T21[h], Tm[h][hc:, hc:]], axis=1)], axis=0)
          for h in H]
    X = [X[h] + _dot(jnp.where(level_b == -1.0, zero, Tm[h]), X[h].astype(BF16)) for h in H]

    S0 = [state_ref[h] for h in H]
    Sb = [t.astype(BF16) for t in S0]
    ub = [(X[h][:, :Dh] - _dot(X[h][:, Dh:].astype(BF16), Sb[h])).astype(BF16) for h in H]
    o = [_dot((q[h] * gam[h]).astype(BF16), Sb[h]) + _dot(qk[h], ub[h]) for h in H]
    k_dec = [(k[h] * jnp.exp(gl[h] - Gc[h])).T.astype(BF16) for h in H]
    for h in H:
        state_ref[h] = jnp.exp(gl[h]) * S0[h] + _dot(k_dec[h], ub[h])
    for h, (bi, hd) in enumerate(chains):
        z = gds[bi][:, 3 * W + hd * Dh:3 * W + (hd + 1) * Dh]
        o_ref[bi, :, hd * Dh:(hd + 1) * Dh] = (_rms(o[h], gn) * _silu(z)).astype(BF16)


def _gdn(gd, ab, prm, gdn_norm, B, S, heads):
    C = GDN_CHUNK
    W = heads * GDN_HEAD_DIM
    nbatch = GDN_BATCH_ROWS if B % GDN_BATCH_ROWS == 0 else 1
    return pl.pallas_call(
        functools.partial(_gdn_kernel, heads=heads),
        grid=(B // nbatch, S // C),
        in_specs=[
            pl.BlockSpec((nbatch, C, 4 * W), lambda b, c: (b, c, 0)),
            pl.BlockSpec((nbatch, C, LANES), lambda b, c: (b, c, 0)),
            pl.BlockSpec((8, LANES), lambda b, c: (0, 0)),
            pl.BlockSpec((1, GDN_HEAD_DIM), lambda b, c: (0, 0)),
        ],
        out_specs=pl.BlockSpec((nbatch, C, W), lambda b, c: (b, c, 0)),
        out_shape=jax.ShapeDtypeStruct((B, S, W), BF16),
        scratch_shapes=[pltpu.VMEM((nbatch * heads, GDN_HEAD_DIM, GDN_HEAD_DIM), F32)],
        compiler_params=pltpu.CompilerParams(
            dimension_semantics=("arbitrary", "arbitrary"), vmem_limit_bytes=VMEM_LIMIT),
        name="gdn",
    )(gd, ab, prm, gdn_norm)


R_E1, R_E2, R_C1, R_C2, R_RANK1, R_RANK2 = range(6)


def _outproj_kernel(ya_ref, yb_ref, x_ref, wo_ref, fg_ref, wr_ref,
                    x1_ref, route_ref, route_t_ref, cnt_ref, carry_ref, wo16_ref):
    @pl.when(pl.program_id(0) == 0)
    def _():
        carry_ref[...] = jnp.zeros_like(carry_ref)
        wo16_ref[...] = wo_ref[...].astype(BF16)

    wa = ya_ref.shape[1]
    ts = ROUTER_SUBTILE
    subs = [pl.ds(j * ts, ts) for j in range(x_ref.shape[0] // ts)]
    each = lambda f, *lists: [f(*args) for args in zip(*lists)]
    rmax = lambda t: jnp.max(t, axis=-1, keepdims=True)
    rmin = lambda t: jnp.min(t, axis=-1, keepdims=True)
    rsum = lambda t: jnp.sum(t, axis=-1, keepdims=True)

    x1 = [x_ref[sl, :] + _dot(ya_ref[sl, :], wo16_ref[:wa, :]) + _dot(yb_ref[sl, :], wo16_ref[wa:, :])
          for sl in subs]
    for sl, t in zip(subs, x1):
        x1_ref[sl, :] = t
    h = each(lambda t: _rms(t, fg_ref[...]).astype(BF16), x1)
    logits = each(lambda t: _dot(t, wr_ref[...]), h)
    lane = lax.broadcasted_iota(jnp.int32, (ts, LANES), 1).astype(F32)
    big = float(LANES)

    is_g = (lane >= N_EXPERTS) & (lane < N_EXPERTS + N_GROUPS)
    lg = each(lambda t: jnp.where(is_g, t, -jnp.inf), logits)
    mg = each(rmax, lg)
    g_sel = each(lambda a, b: rmin(jnp.where(a == b, lane, big)) - N_EXPERTS, lg, mg)
    p_group = each(lambda a, b: 1.0 / rsum(jnp.exp(a - b)), lg, mg)

    le = each(lambda t, g: jnp.where((lane >= g * EXPERTS_PER_GROUP) & (lane < (g + 1) * EXPERTS_PER_GROUP),
                                     t, -jnp.inf), logits, g_sel)
    m1 = each(rmax, le)
    i1 = each(lambda a, b: rmin(jnp.where(a == b, lane, big)), le, m1)
    le2 = each(lambda a, i: jnp.where(lane == i, -jnp.inf, a), le, i1)
    m2 = each(rmax, le2)
    i2 = each(lambda a, b: rmin(jnp.where(a == b, lane, big)), le2, m2)
    se = each(lambda a, b: rsum(jnp.exp(a - b)), le, m1)
    p1 = each(lambda s: 1.0 / s, se)
    p2 = each(lambda a, b, s: jnp.exp(a - b) / s, m2, m1, se)
    c1 = each(lambda pg, a, b: pg * (a / (a + b)), p_group, p1, p2)
    c2 = each(lambda pg, a, b: pg * (b / (a + b)), p_group, p1, p2)

    hot = each(lambda a, b: ((lane == a) | (lane == b)).astype(BF16), i1, i2)
    r_i = lax.broadcasted_iota(jnp.int32, (ts, ts), 0)
    c_i = lax.broadcasted_iota(jnp.int32, (ts, ts), 1)
    before = (c_i < r_i).astype(BF16)
    within = each(lambda t: _dot(before, t), hot)
    totals = each(lambda t: jnp.sum(t.astype(F32), axis=0, keepdims=True), hot)
    base = [carry_ref[0:1, :]]
    for t in totals:
        base.append(base[-1] + t)
    seen = each(lambda a, b: a + b, within, base[:-1])
    rank1 = each(lambda i, s: rsum(jnp.where(lane == i, s, 0.0)), i1, seen)
    rank2 = each(lambda i, s: rsum(jnp.where(lane == i, s, 0.0)), i2, seen)
    carry_ref[...] = jnp.broadcast_to(base[-1], carry_ref.shape)
    cnt_ref[...] = jnp.broadcast_to(base[-1], cnt_ref.shape)

    for j, sl in enumerate(subs):
        rec = jnp.zeros((ts, LANES), F32)
        for slot, val in ((R_E1, i1), (R_E2, i2), (R_C1, c1), (R_C2, c2), (R_RANK1, rank1), (R_RANK2, rank2)):
            rec = jnp.where(lane == float(slot), val[j], rec)
        route_ref[sl, :] = rec
        route_t_ref[:, sl] = rec.T[0:8, :]


def _outproj(ya, yb, x2d, wo, fgain, wr, tm):
    T, D = x2d.shape
    Wa, Wb = ya.shape[1], yb.shape[1]
    return pl.pallas_call(
        _outproj_kernel,
        grid=(T // tm,),
        in_specs=[
            pl.BlockSpec((tm, Wa), lambda i: (i, 0)),
            pl.BlockSpec((tm, Wb), lambda i: (i, 0)),
            pl.BlockSpec((tm, D), lambda i: (i, 0)),
            pl.BlockSpec((Wa + Wb, D), lambda i: (0, 0)),
            pl.BlockSpec((1, D), lambda i: (0, 0)),
            pl.BlockSpec((D, LANES), lambda i: (0, 0)),
        ],
        out_specs=[
            pl.BlockSpec((tm, D), lambda i: (i, 0)),
            pl.BlockSpec((tm, LANES), lambda i: (i, 0)),
            pl.BlockSpec((8, tm), lambda i: (0, i)),
            pl.BlockSpec((8, LANES), lambda i: (0, 0)),
        ],
        out_shape=[
            jax.ShapeDtypeStruct((T, D), F32),
            jax.ShapeDtypeStruct((T, LANES), F32),
            jax.ShapeDtypeStruct((8, T), F32),
            jax.ShapeDtypeStruct((8, LANES), F32),
        ],
        scratch_shapes=[pltpu.VMEM((8, LANES), F32), pltpu.VMEM((Wa + Wb, D), BF16)],
        compiler_params=pltpu.CompilerParams(
            dimension_semantics=("arbitrary",), vmem_limit_bytes=VMEM_LIMIT),
        name="outproj_router",
    )(ya, yb, x2d, wo, fgain, wr)


def _dispatch_kernel(tail_ref, pos_ref, x_ref, xs_ref, zero_ref, sem):
    tm = x_ref.shape[0]
    rt = zero_ref.shape[0]

    @pl.when(pl.program_id(0) == 0)
    def _():
        zero_ref[...] = jnp.zeros_like(zero_ref)

        def tile_fill(j):
            return pltpu.make_async_copy(zero_ref, xs_ref.at[pl.ds(pl.multiple_of(j * rt, 8), rt)], sem)

        def for_each_fill(act):
            for e in range(N_EXPERTS):
                pl.when(tail_ref[e] >= 0)(lambda e=e: act(tile_fill(tail_ref[e])))
            lax.fori_loop(tail_ref[N_EXPERTS], xs_ref.shape[0] // rt, lambda j, c: (act(tile_fill(j)), c)[1], 0)

        for_each_fill(lambda f: f.start())
        for_each_fill(lambda f: f.wait())

    def issue(r, carry):
        for slot in range(2):
            p = pos_ref[0, slot * tm + r]
            pltpu.make_async_copy(x_ref.at[pl.ds(r, 1)], xs_ref.at[pl.ds(p, 1)], sem).start()
        return carry

    lax.fori_loop(0, tm, issue, 0, unroll=8)
    for _ in range(2):
        pltpu.make_async_copy(x_ref, xs_ref.at[pl.ds(0, tm)], sem).wait()


def _dispatch(tails, pos, h, n_rows, tm, rt):
    T, Dp = h.shape
    return pl.pallas_call(
        _dispatch_kernel,
        grid_spec=pltpu.PrefetchScalarGridSpec(
            num_scalar_prefetch=1,
            grid=(T // tm,),
            in_specs=[
                pl.BlockSpec((None, 1, 2 * tm), lambda i, tl: (i, 0, 0), memory_space=pltpu.SMEM),
                pl.BlockSpec((tm, Dp), lambda i, tl: (i, 0)),
            ],
            out_specs=pl.BlockSpec(memory_space=pl.ANY),
            scratch_shapes=[pltpu.VMEM((rt, Dp), h.dtype), pltpu.SemaphoreType.DMA],
        ),
        out_shape=jax.ShapeDtypeStruct((n_rows, Dp), h.dtype),
        compiler_params=pltpu.CompilerParams(
            dimension_semantics=("arbitrary",), vmem_limit_bytes=VMEM_LIMIT),
        name="moe_dispatch",
    )(tails, pos, h)


def _experts_kernel(te_ref, tv_ref, xs_ref, fg_ref, wg_ref, wu_ref, wd_ref, y_ref):
    del te_ref
    i = pl.program_id(0)

    @pl.when(tv_ref[i] > 0)
    def _():
        h = _rms(xs_ref[...], fg_ref[...]).astype(BF16)
        a = _dot(h, wg_ref[...].astype(BF16))
        b = _dot(h, wu_ref[...].astype(BF16))
        y_ref[...] = _dot((_silu(a) * b).astype(BF16), wd_ref[...].astype(BF16))

    @pl.when(tv_ref[i] == 0)
    def _():
        y_ref[...] = jnp.zeros_like(y_ref)


def _experts(tile_expert, tile_valid, xs, fgain, wg, wu, wd, rt):
    n_rows = tile_expert.shape[0] * rt
    Dp = xs.shape[1]
    E, D, F = wg.shape
    return pl.pallas_call(
        _experts_kernel,
        grid_spec=pltpu.PrefetchScalarGridSpec(
            num_scalar_prefetch=2,
            grid=(n_rows // rt,),
            in_specs=[
                pl.BlockSpec((rt, Dp), lambda i, te, tv: (i * tv[i], 0)),
                pl.BlockSpec((1, D), lambda i, te, tv: (0, 0)),
                pl.BlockSpec((None, D, F), lambda i, te, tv: (te[i], 0, 0)),
                pl.BlockSpec((None, D, F), lambda i, te, tv: (te[i], 0, 0)),
                pl.BlockSpec((None, F, D), lambda i, te, tv: (te[i], 0, 0)),
            ],
            out_specs=pl.BlockSpec((rt, D), lambda i, te, tv: (i, 0)),
        ),
        out_shape=jax.ShapeDtypeStruct((n_rows, D), F32),
        compiler_params=pltpu.CompilerParams(
            dimension_semantics=("arbitrary",), vmem_limit_bytes=VMEM_LIMIT),
        name="moe_experts",
    )(tile_expert, tile_valid, xs, fgain, wg, wu, wd)


def _combine_kernel(pos_ref, pos_next_ref, x1_ref, route_ref, fn_ref, y_ref, o_ref, buf_ref, sem):
    tm = x1_ref.shape[0]
    i = pl.program_id(0)
    ring = i % 2

    def row_copy(p_ref, r, slot, rg):
        p = p_ref[0, slot * tm + r]
        return pltpu.make_async_copy(y_ref.at[pl.ds(p, 1)], buf_ref.at[rg, slot, pl.ds(r, 1)], sem.at[rg])

    def issue(p_ref, rg):
        def body(r, carry):
            for slot in range(2):
                row_copy(p_ref, r, slot, rg).start(priority=slot)
            return carry
        lax.fori_loop(0, tm, body, 0, unroll=8)

    @pl.when(i == 0)
    def _():
        issue(pos_ref, ring)

    @pl.when(i + 1 < pl.num_programs(0))
    def _():
        issue(pos_next_ref, 1 - ring)

    for slot in range(2):
        pltpu.make_async_copy(y_ref.at[pl.ds(0, tm)], buf_ref.at[ring, slot], sem.at[ring]).wait()
    route = route_ref[...]
    c1 = route[:, R_C1:R_C1 + 1]
    c2 = route[:, R_C2:R_C2 + 1]
    o_ref[...] = _rms(x1_ref[...] + c1 * buf_ref[ring, 0] + c2 * buf_ref[ring, 1], fn_ref[...])


def _combine(pos, x1, route, fnorm, y, tm):
    T, D = x1.shape
    n = T // tm
    return pl.pallas_call(
        _combine_kernel,
        grid=(n,),
        in_specs=[
            pl.BlockSpec((None, 1, 2 * tm), lambda i: (i, 0, 0), memory_space=pltpu.SMEM),
            pl.BlockSpec((None, 1, 2 * tm), lambda i: (jnp.minimum(i + 1, n - 1), 0, 0),
                         memory_space=pltpu.SMEM),
            pl.BlockSpec((tm, D), lambda i: (i, 0)),
            pl.BlockSpec((tm, LANES), lambda i: (i, 0)),
            pl.BlockSpec((1, D), lambda i: (0, 0)),
            pl.BlockSpec(memory_space=pl.ANY),
        ],
        out_specs=pl.BlockSpec((tm, D), lambda i: (i, 0)),
        out_shape=jax.ShapeDtypeStruct((T, D), F32),
        scratch_shapes=[pltpu.VMEM((2, 2, tm, D), F32), pltpu.SemaphoreType.DMA((2,))],
        compiler_params=pltpu.CompilerParams(
            dimension_semantics=("arbitrary",), vmem_limit_bytes=VMEM_LIMIT),
        name="moe_combine",
    )(pos, pos, x1, route, fnorm, y)


def _pick_tile(T, pref):
    tm = min(pref, T)
    while T % tm:
        tm //= 2
    return tm


def kernel(x, attn_norm, w_in, conv_w, A_log, dt_bias, gdn_norm, w_out, ffn_norm,
           w_group, w_router, w_gate, w_up, w_down, final_norm):
    B, S, D = x.shape
    T = B * S
    depth = w_in.shape[0]
    heads = A_log.shape[1]
    Wb = heads * GDN_HEAD_DIM
    Wa = w_out.shape[1] - Wb
    assert S % MOBA_BLOCK == 0 and S // MOBA_BLOCK <= GATE_ROWS and S % GDN_CHUNK == 0
    assert Wa % LANES == 0 and 2 * heads <= LANES
    assert depth == 1, "the final RMSNorm is fused into the layer's last kernel"

    x2d = x.reshape(T, D)
    for l in range(depth):
        w_all = jnp.pad(w_in[l], ((0, 0), (0, LANES - 2 * heads))).astype(BF16)
        qkv_a, gd, ab = _inproj(x2d, attn_norm[l][None, :], w_all, conv_w[l].astype(F32),
                                3 * Wa, 4 * Wb, _pick_tile(S, 512), S)

        y_a = _moba(qkv_a.reshape(B, S, 3 * Wa), B, S)

        prm = jnp.zeros((8, LANES), F32)
        prm = prm.at[0, :heads].set(A_log[l].astype(F32)).at[1, :heads].set(dt_bias[l].astype(F32))
        y_b = _gdn(gd.reshape(B, S, 4 * Wb), ab.reshape(B, S, LANES), prm,
                   gdn_norm[l][None, :].astype(F32), B, S, heads)

        wr = jnp.concatenate([w_router[l], w_group[l]], axis=1)
        wr = jnp.pad(wr, ((0, 0), (0, LANES - wr.shape[1]))).astype(BF16)
        x1, route, route_t, counts = _outproj(y_a.reshape(T, Wa), y_b.reshape(T, Wb), x2d,
                                              w_out[l].astype(F32), ffn_norm[l][None, :], wr, _pick_tile(T, 512))

        rt = MOE_ROW_TILE
        n_tiles = -(-2 * T // rt) + N_EXPERTS
        cnt = counts[0, :N_EXPERTS].astype(jnp.int32)
        seg = -(-cnt // rt) * rt
        seg_end = jnp.cumsum(seg)
        seg_start = seg_end - seg
        tile_row = jnp.arange(n_tiles, dtype=jnp.int32) * rt
        tile_expert = jnp.minimum(jnp.sum(tile_row[:, None] >= seg_end[None, :], axis=1), N_EXPERTS - 1)
        tile_valid = (tile_row < seg_end[-1]).astype(jnp.int32)
        e_ids = jnp.arange(N_EXPERTS, dtype=jnp.int32)[:, None]

        def dest_rows(e_lane, rank_lane):
            e = route_t[e_lane].astype(jnp.int32)
            return route_t[rank_lane].astype(jnp.int32) + jnp.sum(
                jnp.where(e[None, :] == e_ids, seg_start[:, None], 0), axis=0)

        pos1, pos2 = dest_rows(R_E1, R_RANK1), dest_rows(R_E2, R_RANK2)

        def tiled_pos(tm):
            return jnp.concatenate([pos1.reshape(T // tm, 1, tm), pos2.reshape(T // tm, 1, tm)], axis=2)

        F = w_gate.shape[-1]
        wge = w_gate[l].reshape(N_EXPERTS, D, F)
        wue = w_up[l].reshape(N_EXPERTS, D, F)
        wde = w_down[l].reshape(N_EXPERTS, F, D)
        tmd = _pick_tile(T, 512)
        tails = jnp.concatenate([jnp.where(seg > 0, seg_end // rt - 1, -1), seg_end[-1:] // rt])
        xs = _dispatch(tails, tiled_pos(tmd), x1, n_tiles * rt, tmd, rt)
        ys = _experts(tile_expert.astype(jnp.int32), tile_valid, xs, ffn_norm[l][None, :],
                      wge, wue, wde, rt)
        tmc = _pick_tile(T, 512)
        x2d = _combine(tiled_pos(tmc), x1, route, final_norm[None, :], ys, tmc)
    return x2d.reshape(B, S, D)
```

```python
import functools

import jax
import jax.numpy as jnp
from jax import lax
from jax.experimental import pallas as pl
from jax.experimental.pallas import tpu as pltpu

F32 = jnp.float32
BF16 = jnp.bfloat16

LANES = 128
MOBA_HEAD_DIM = 64
MOBA_BLOCK = 256
MOBA_TOPK = 3
MOBA_Q_SCALE = MOBA_HEAD_DIM ** -0.5 * 1.4426950408889634
GDN_HEAD_DIM = 128
GDN_CONV = 4
GDN_CHUNK = 256
GDN_BATCH_ROWS = 2
N_GROUPS = 4
EXPERTS_PER_GROUP = 8
N_EXPERTS = N_GROUPS * EXPERTS_PER_GROUP
RMS_EPS = 1e-6
NEG = -1e30
GATE_ROWS = 16
MOE_ROW_TILE = 512
ROUTER_SUBTILE = 128
VMEM_LIMIT = 48 * 1024 * 1024


def _dot(a, b):
    return jnp.dot(a, b, preferred_element_type=F32)


def _dot_nt(a, b):
    return lax.dot_general(a, b, (((1,), (1,)), ((), ())), preferred_element_type=F32)


def _sigmoid(x):
    return 1.0 / (1.0 + jnp.exp(-x))


def _silu(x):
    return x * _sigmoid(x)


def _rms(x, gain):
    return x * lax.rsqrt(jnp.mean(x * x, axis=-1, keepdims=True) + RMS_EPS) * gain


def _inproj_kernel(x_ref, gain_ref, w_ref, cw_ref, om_ref, og_ref, oab_ref, prev_ref, *, tiles_per_seq):
    nm, ng = om_ref.shape[1], og_ref.shape[1]
    tm = x_ref.shape[0]

    @pl.when(pl.program_id(0) % tiles_per_seq == 0)
    def _():
        prev_ref[...] = jnp.zeros_like(prev_ref)

    W = ng // 4
    Dh = GDN_HEAD_DIM
    cw = cw_ref[...]
    half = tm // 2
    h_top = _rms(x_ref[:half, :], gain_ref[...]).astype(BF16)
    h_bot = _rms(x_ref[half:, :], gain_ref[...]).astype(BF16)

    def gdn_tail(rows, og, halo):
        raw = og[:, :3 * W]
        ext = jnp.concatenate([halo, raw], axis=0)
        conv = cw[GDN_CONV - 1:GDN_CONV] * raw
        for d in range(1, GDN_CONV):
            conv = conv + cw[GDN_CONV - 1 - d:GDN_CONV - d] * pltpu.roll(ext, d, 0)[8:]
        qkv = _silu(conv)
        for j in range(2 * W // Dh):
            t = qkv[:, j * Dh:(j + 1) * Dh]
            t = t * lax.rsqrt(jnp.sum(t * t, axis=-1, keepdims=True) + 1e-6)
            og_ref[rows, j * Dh:(j + 1) * Dh] = t * (Dh ** -0.5) if j < W // Dh else t
        og_ref[rows, 2 * W:3 * W] = qkv[:, 2 * W:]
        og_ref[rows, 3 * W:] = og[:, 3 * W:]
        return raw[half - 8:, :]

    og_top = _dot(h_top, w_ref[:, nm:nm + ng])
    og_bot = _dot(h_bot, w_ref[:, nm:nm + ng])
    last = gdn_tail(slice(0, half), og_top, prev_ref[...])
    h = jnp.concatenate([h_top, h_bot], axis=0)
    om = _dot(h, w_ref[:, :nm])
    nq = nm // 3
    om_ref[:, :nq] = (om[:, :nq] * MOBA_Q_SCALE).astype(BF16)
    om_ref[:, nq:] = om[:, nq:].astype(BF16)
    prev_ref[...] = gdn_tail(slice(half, tm), og_bot, last)
    oab_ref[...] = _dot(h, w_ref[:, nm + ng:])


def _inproj(x2d, gain, w_all, conv_w, nm, ng, tm, seq_len):
    T, D = x2d.shape
    assert w_all.shape[1] == nm + ng + LANES and nm % LANES == 0 and ng % LANES == 0
    assert seq_len % tm == 0 and tm >= 8
    return pl.pallas_call(
        functools.partial(_inproj_kernel, tiles_per_seq=seq_len // tm),
        grid=(T // tm,),
        in_specs=[
            pl.BlockSpec((tm, D), lambda i: (i, 0)),
            pl.BlockSpec((1, D), lambda i: (0, 0)),
            pl.BlockSpec((D, nm + ng + LANES), lambda i: (0, 0)),
            pl.BlockSpec((GDN_CONV, 3 * ng // 4), lambda i: (0, 0)),
        ],
        out_specs=[
            pl.BlockSpec((tm, nm), lambda i: (i, 0)),
            pl.BlockSpec((tm, ng), lambda i: (i, 0)),
            pl.BlockSpec((tm, LANES), lambda i: (i, 0)),
        ],
        out_shape=[
            jax.ShapeDtypeStruct((T, nm), BF16),
            jax.ShapeDtypeStruct((T, ng), F32),
            jax.ShapeDtypeStruct((T, LANES), F32),
        ],
        scratch_shapes=[pltpu.VMEM((8, 3 * ng // 4), F32)],
        compiler_params=pltpu.CompilerParams(
            dimension_semantics=("arbitrary",), vmem_limit_bytes=VMEM_LIMIT),
        name="inproj",
    )(x2d, gain, w_all, conv_w)


def _moba_kernel(q_ref, k_ref, v_ref, o_ref, *, S):
    nb = S // MOBA_BLOCK
    L = MOBA_BLOCK
    q = q_ref[...]
    k = k_ref[...]
    v = v_ref[...]
    lane = lax.broadcasted_iota(jnp.int32, (1, LANES), 1)
    key_blk = lax.broadcasted_iota(jnp.int32, (S, LANES), 0) // L
    lane_s = lax.broadcasted_iota(jnp.int32, (S, LANES), 1)
    qblk = lax.broadcasted_iota(jnp.int32, (GATE_ROWS, S), 1) // L
    cidx = lax.broadcasted_iota(jnp.int32, (GATE_ROWS, S), 0)
    r_i = lax.broadcasted_iota(jnp.int32, (L, L), 0)
    c_i = lax.broadcasted_iota(jnp.int32, (L, L), 1)
    causal = c_i <= r_i

    km = jnp.mean(k.astype(F32).reshape(nb, L, LANES), axis=1)
    km = jnp.concatenate([km, jnp.zeros((GATE_ROWS - nb, LANES), F32)], axis=0).astype(BF16)

    qa, ka, va, hms = [], [], [], []
    for h in range(2):
        hm = (lane >= MOBA_HEAD_DIM * h) & (lane < MOBA_HEAD_DIM * (h + 1))
        off = MOBA_HEAD_DIM * (1 - h)
        qh = jnp.where(hm, q, jnp.zeros_like(q))
        gate = _dot_nt(km, qh)
        valid = cidx < qblk
        gate = jnp.where(valid, gate, -jnp.inf)
        rank = jnp.zeros((GATE_ROWS, S), F32)
        for i in range(nb):
            gi = gate[i:i + 1, :]
            beats = (gi > gate) | ((gi == gate) & (i < cidx))
            rank = rank + beats.astype(F32)
        sel = valid & (rank < float(MOBA_TOPK))
        pen = jnp.where(sel | (cidx >= qblk), 0.0, NEG)
        pads = [jnp.zeros((r, S), F32) for r in (off, LANES - off - GATE_ROWS)]
        pen = jnp.concatenate(([pads[0]] if off else []) + [pen, pads[1]], axis=0)
        pen_q = pen.T.astype(BF16)
        qa.append(jnp.where(hm, q, pen_q))
        onehot = ((lane_s - off) == key_blk).astype(BF16)
        ka.append(jnp.where(hm, k, onehot))
        va.append(jnp.where(hm, v, jnp.ones_like(v)))
        hms.append(hm)

    HH = range(2)
    for n in range(nb):
        s = [_dot_nt(qa[h][n * L:(n + 1) * L], ka[h][:(n + 1) * L]) for h in HH]
        s_own = [jnp.where(causal, s[h][:, n * L:], NEG) for h in HH]
        m = [jnp.max(s_own[h], axis=-1, keepdims=True) for h in HH]
        if n > 0:
            m = [jnp.maximum(m[h], jnp.max(s[h][:, :n * L], axis=-1, keepdims=True)) for h in HH]
            p = [jnp.concatenate([jnp.exp2(s[h][:, :n * L] - m[h]), jnp.exp2(s_own[h] - m[h])],
                                 axis=1).astype(BF16) for h in HH]
        else:
            p = [jnp.exp2(s_own[h] - m[h]).astype(BF16) for h in HH]
        acc = [_dot(p[h], va[h][:(n + 1) * L]) for h in HH]
        outs = [acc[h] / pltpu.roll(acc[h], MOBA_HEAD_DIM, 1) for h in HH]
        o_ref[n * L:(n + 1) * L, :] = jnp.where(hms[0], outs[0], outs[1]).astype(BF16)


def _moba(qkv, B, S):
    W = qkv.shape[-1] // 3
    npair = W // LANES
    return pl.pallas_call(
        functools.partial(_moba_kernel, S=S),
        grid=(B, npair),
        in_specs=[
            pl.BlockSpec((None, S, LANES), lambda b, j: (b, 0, j)),
            pl.BlockSpec((None, S, LANES), lambda b, j: (b, 0, npair + j)),
            pl.BlockSpec((None, S, LANES), lambda b, j: (b, 0, 2 * npair + j)),
        ],
        out_specs=pl.BlockSpec((None, S, LANES), lambda b, j: (b, 0, j)),
        out_shape=jax.ShapeDtypeStruct((B, S, W), BF16),
        compiler_params=pltpu.CompilerParams(
            dimension_semantics=("arbitrary", "arbitrary"), vmem_limit_bytes=VMEM_LIMIT),
        name="moba",
    )(qkv, qkv, qkv)


def _gdn_kernel(gd_ref, ab_ref, prm_ref, gn_ref, o_ref, state_ref, *, heads):
    C = GDN_CHUNK
    Dh = GDN_HEAD_DIM
    W = heads * Dh
    c = pl.program_id(1)

    @pl.when(c == 0)
    def _():
        state_ref[...] = jnp.zeros_like(state_ref)

    nbatch = gd_ref.shape[0]
    prm = prm_ref[...]
    rows = lax.broadcasted_iota(jnp.int32, (C, LANES), 0)
    gds, Gs, GTs, betas = [], [], [], []
    for bi in range(nbatch):
        gds.append(gd_ref[bi])
        ab = ab_ref[bi]
        sp_in = ab + prm[1:2]
        softplus = jnp.maximum(sp_in, 0.0) + jnp.log1p(jnp.exp(-jnp.abs(sp_in)))
        G = -jnp.exp(prm[0:1]) * softplus
        betas.append(_sigmoid(ab))
        d = 1
        while d < C:
            G = G + jnp.where(rows >= d, pltpu.roll(G, d, 0), 0.0)
            d *= 2
        Gs.append(G)
        GTs.append(G.T)

    r_i = lax.broadcasted_iota(jnp.int32, (C, C), 0)
    c_i = lax.broadcasted_iota(jnp.int32, (C, C), 1)
    incl = c_i <= r_i
    strict = c_i < r_i
    rc = r_i ^ c_i
    level = jnp.full((C, C), -1, jnp.int32)
    for b in range(C.bit_length() - 1):
        level = level + (rc >= (1 << b)).astype(jnp.int32)
    level_b = level.astype(F32).astype(BF16)
    gn = gn_ref[...]

    chains = [(bi, hd) for bi in range(nbatch) for hd in range(heads)]
    H = range(len(chains))
    q = [gds[bi][:, hd * Dh:(hd + 1) * Dh] for bi, hd in chains]
    k = [gds[bi][:, W + hd * Dh:W + (hd + 1) * Dh] for bi, hd in chains]
    v = [gds[bi][:, 2 * W + hd * Dh:2 * W + (hd + 1) * Dh] for bi, hd in chains]
    Gc = [Gs[bi][:, hd:hd + 1] for bi, hd in chains]
    Gr = [GTs[bi][hd:hd + 1, :] for bi, hd in chains]
    bc = [betas[bi][:, heads + hd:heads + hd + 1] for bi, hd in chains]
    gl = [Gs[bi][C - 1:C, hd:hd + 1] for bi, hd in chains]

    kb = [t.astype(BF16) for t in k]
    decay = [jnp.exp(jnp.where(incl, Gc[h] - Gr[h], -jnp.inf)) for h in H]
    A = [jnp.where(strict, bc[h] * _dot_nt(kb[h], kb[h]) * decay[h], 0.0) for h in H]
    qk = [(_dot_nt(q[h].astype(BF16), kb[h]) * decay[h]).astype(BF16) for h in H]
    gam = [jnp.exp(t) for t in Gc]
    X = [jnp.concatenate([bc[h] * v[h], (bc[h] * gam[h]) * k[h]], axis=1) for h in H]
    Ab = [t.astype(BF16) for t in A]
    zero = jnp.zeros((C, C), BF16)
    Tm = [jnp.where(level_b == -1.0, jnp.ones((C, C), BF16), jnp.where(level_b == 0.0, -t, zero))
          for t in Ab]
    for lv in range(1, C.bit_length() - 2):
        at_lv = level_b == float(lv)
        E = [jnp.where(at_lv, t, zero) for t in Ab]
        F = [_dot(E[h], Tm[h]).astype(BF16) for h in H]
        Tm = [Tm[h] - _dot(Tm[h], F[h]).astype(BF16) for h in H]
    hc = C // 2
    F21 = [_dot(Ab[h][hc:, :hc], Tm[h][:hc, :hc]).astype(BF16) for h in H]
    T21 = [-_dot(Tm[h][hc:, hc:], F21[h]).astype(BF16) for h in H]
    Tm = [jnp.concatenate([Tm[h][:hc, :], jnp.concatenate([T21[h], Tm[h][hc:, hc:]], axis=1)], axis=0)
          for h in H]
    X = [X[h] + _dot(jnp.where(level_b == -1.0, zero, Tm[h]), X[h].astype(BF16)) for h in H]

    S0 = [state_ref[h] for h in H]
    Sb = [t.astype(BF16) for t in S0]
    ub = [(X[h][:, :Dh] - _dot(X[h][:, Dh:].astype(BF16), Sb[h])).astype(BF16) for h in H]
    o = [_dot((q[h] * gam[h]).astype(BF16), Sb[h]) + _dot(qk[h], ub[h]) for h in H]
    k_dec = [(k[h] * jnp.exp(gl[h] - Gc[h])).T.astype(BF16) for h in H]
    for h in H:
        state_ref[h] = jnp.exp(gl[h]) * S0[h] + _dot(k_dec[h], ub[h])
    for h, (bi, hd) in enumerate(chains):
        z = gds[bi][:, 3 * W + hd * Dh:3 * W + (hd + 1) * Dh]
        o_ref[bi, :, hd * Dh:(hd + 1) * Dh] = (_rms(o[h], gn) * _silu(z)).astype(BF16)


def _gdn(gd, ab, prm, gdn_norm, B, S, heads):
    C = GDN_CHUNK
    W = heads * GDN_HEAD_DIM
    nbatch = GDN_BATCH_ROWS if B % GDN_BATCH_ROWS == 0 else 1
    return pl.pallas_call(
        functools.partial(_gdn_kernel, heads=heads),
        grid=(B // nbatch, S // C),
        in_specs=[
            pl.BlockSpec((nbatch, C, 4 * W), lambda b, c: (b, c, 0)),
            pl.BlockSpec((nbatch, C, LANES), lambda b, c: (b, c, 0)),
            pl.BlockSpec((8, LANES), lambda b, c: (0, 0)),
            pl.BlockSpec((1, GDN_HEAD_DIM), lambda b, c: (0, 0)),
        ],
        out_specs=pl.BlockSpec((nbatch, C, W), lambda b, c: (b, c, 0)),
        out_shape=jax.ShapeDtypeStruct((B, S, W), BF16),
        scratch_shapes=[pltpu.VMEM((nbatch * heads, GDN_HEAD_DIM, GDN_HEAD_DIM), F32)],
        compiler_params=pltpu.CompilerParams(
            dimension_semantics=("arbitrary", "arbitrary"), vmem_limit_bytes=VMEM_LIMIT),
        name="gdn",
    )(gd, ab, prm, gdn_norm)


R_E1, R_E2, R_C1, R_C2, R_RANK1, R_RANK2, R_TOKEN = range(7)


def _outproj_kernel(ya_ref, yb_ref, x_ref, wo_ref, fg_ref, wr_ref,
                    x1_ref, route_t_ref, cnt_ref, carry_ref, wo16_ref):
    @pl.when(pl.program_id(0) == 0)
    def _():
        carry_ref[...] = jnp.zeros_like(carry_ref)
        wo16_ref[...] = wo_ref[...].astype(BF16)

    wa = ya_ref.shape[1]
    ts = ROUTER_SUBTILE
    subs = [pl.ds(j * ts, ts) for j in range(x_ref.shape[0] // ts)]
    each = lambda f, *lists: [f(*args) for args in zip(*lists)]
    rmax = lambda t: jnp.max(t, axis=-1, keepdims=True)
    rmin = lambda t: jnp.min(t, axis=-1, keepdims=True)
    rsum = lambda t: jnp.sum(t, axis=-1, keepdims=True)

    x1 = [x_ref[sl, :] + _dot(ya_ref[sl, :], wo16_ref[:wa, :]) + _dot(yb_ref[sl, :], wo16_ref[wa:, :])
          for sl in subs]
    D = x_ref.shape[1]
    for sl, t in zip(subs, x1):
        x1_ref[sl, :D] = t
    h = each(lambda t: _rms(t, fg_ref[...]).astype(BF16), x1)
    logits = each(lambda t: _dot(t, wr_ref[...]), h)
    lane = lax.broadcasted_iota(jnp.int32, (ts, LANES), 1).astype(F32)
    big = float(LANES)

    is_g = (lane >= N_EXPERTS) & (lane < N_EXPERTS + N_GROUPS)
    lg = each(lambda t: jnp.where(is_g, t, -jnp.inf), logits)
    mg = each(rmax, lg)
    g_sel = each(lambda a, b: rmin(jnp.where(a == b, lane, big)) - N_EXPERTS, lg, mg)
    p_group = each(lambda a, b: 1.0 / rsum(jnp.exp(a - b)), lg, mg)

    le = each(lambda t, g: jnp.where((lane >= g * EXPERTS_PER_GROUP) & (lane < (g + 1) * EXPERTS_PER_GROUP),
                                     t, -jnp.inf), logits, g_sel)
    m1 = each(rmax, le)
    i1 = each(lambda a, b: rmin(jnp.where(a == b, lane, big)), le, m1)
    le2 = each(lambda a, i: jnp.where(lane == i, -jnp.inf, a), le, i1)
    m2 = each(rmax, le2)
    i2 = each(lambda a, b: rmin(jnp.where(a == b, lane, big)), le2, m2)
    se = each(lambda a, b: rsum(jnp.exp(a - b)), le, m1)
    p1 = each(lambda s: 1.0 / s, se)
    p2 = each(lambda a, b, s: jnp.exp(a - b) / s, m2, m1, se)
    c1 = each(lambda pg, a, b: pg * (a / (a + b)), p_group, p1, p2)
    c2 = each(lambda pg, a, b: pg * (b / (a + b)), p_group, p1, p2)

    hot = each(lambda a, b: ((lane == a) | (lane == b)).astype(BF16), i1, i2)
    r_i = lax.broadcasted_iota(jnp.int32, (ts, ts), 0)
    c_i = lax.broadcasted_iota(jnp.int32, (ts, ts), 1)
    before = (c_i < r_i).astype(BF16)
    within = each(lambda t: _dot(before, t), hot)
    totals = each(lambda t: jnp.sum(t.astype(F32), axis=0, keepdims=True), hot)
    base = [carry_ref[0:1, :]]
    for t in totals:
        base.append(base[-1] + t)
    seen = each(lambda a, b: a + b, within, base[:-1])
    rank1 = each(lambda i, s: rsum(jnp.where(lane == i, s, 0.0)), i1, seen)
    rank2 = each(lambda i, s: rsum(jnp.where(lane == i, s, 0.0)), i2, seen)
    carry_ref[...] = jnp.broadcast_to(base[-1], carry_ref.shape)
    cnt_ref[...] = jnp.broadcast_to(base[-1], cnt_ref.shape)

    row = lax.broadcasted_iota(jnp.int32, (ts, 1), 0)
    for j, sl in enumerate(subs):
        token = (pl.program_id(0) * x_ref.shape[0] + j * ts + row).astype(F32)
        rec = jnp.zeros((ts, LANES), F32)
        for slot, val in ((R_E1, i1[j]), (R_E2, i2[j]), (R_C1, c1[j]), (R_C2, c2[j]),
                          (R_RANK1, rank1[j]), (R_RANK2, rank2[j]), (R_TOKEN, token)):
            rec = jnp.where(lane == float(slot), val, rec)
        x1_ref[sl, D:] = rec
        route_t_ref[:, sl] = rec.T[0:8, :]


def _outproj(ya, yb, x2d, wo, fgain, wr, tm):
    T, D = x2d.shape
    Wa, Wb = ya.shape[1], yb.shape[1]
    return pl.pallas_call(
        _outproj_kernel,
        grid=(T // tm,),
        in_specs=[
            pl.BlockSpec((tm, Wa), lambda i: (i, 0)),
            pl.BlockSpec((tm, Wb), lambda i: (i, 0)),
            pl.BlockSpec((tm, D), lambda i: (i, 0)),
            pl.BlockSpec((Wa + Wb, D), lambda i: (0, 0)),
            pl.BlockSpec((1, D), lambda i: (0, 0)),
            pl.BlockSpec((D, LANES), lambda i: (0, 0)),
        ],
        out_specs=[
            pl.BlockSpec((tm, D + LANES), lambda i: (i, 0)),
            pl.BlockSpec((8, tm), lambda i: (0, i)),
            pl.BlockSpec((8, LANES), lambda i: (0, 0)),
        ],
        out_shape=[
            jax.ShapeDtypeStruct((T, D + LANES), F32),
            jax.ShapeDtypeStruct((8, T), F32),
            jax.ShapeDtypeStruct((8, LANES), F32),
        ],
        scratch_shapes=[pltpu.VMEM((8, LANES), F32), pltpu.VMEM((Wa + Wb, D), BF16)],
        compiler_params=pltpu.CompilerParams(
            dimension_semantics=("arbitrary",), vmem_limit_bytes=VMEM_LIMIT),
        name="outproj_router",
    )(ya, yb, x2d, wo, fgain, wr)


def _dispatch_kernel(tail_ref, pos_ref, x_ref, xs_ref, zero_ref, sem):
    tm = x_ref.shape[0]
    rt = zero_ref.shape[0]

    @pl.when(pl.program_id(0) == 0)
    def _():
        zero_ref[...] = jnp.zeros_like(zero_ref)

        def tile_fill(j):
            return pltpu.make_async_copy(zero_ref, xs_ref.at[pl.ds(pl.multiple_of(j * rt, 8), rt)], sem)

        def for_each_fill(act):
            for e in range(N_EXPERTS):
                pl.when(tail_ref[e] >= 0)(lambda e=e: act(tile_fill(tail_ref[e])))
            lax.fori_loop(tail_ref[N_EXPERTS], xs_ref.shape[0] // rt, lambda j, c: (act(tile_fill(j)), c)[1], 0)

        for_each_fill(lambda f: f.start())
        for_each_fill(lambda f: f.wait())

    def issue(r, carry):
        for slot in range(2):
            p = pos_ref[0, slot * tm + r]
            pltpu.make_async_copy(x_ref.at[pl.ds(r, 1)], xs_ref.at[pl.ds(p, 1)], sem).start()
        return carry

    lax.fori_loop(0, tm, issue, 0, unroll=8)
    for _ in range(2):
        pltpu.make_async_copy(x_ref, xs_ref.at[pl.ds(0, tm)], sem).wait()


def _dispatch(tails, pos, h, n_rows, tm, rt):
    T, Dp = h.shape
    return pl.pallas_call(
        _dispatch_kernel,
        grid_spec=pltpu.PrefetchScalarGridSpec(
            num_scalar_prefetch=1,
            grid=(T // tm,),
            in_specs=[
                pl.BlockSpec((None, 1, 2 * tm), lambda i, tl: (i, 0, 0), memory_space=pltpu.SMEM),
                pl.BlockSpec((tm, Dp), lambda i, tl: (i, 0)),
            ],
            out_specs=pl.BlockSpec(memory_space=pl.ANY),
            scratch_shapes=[pltpu.VMEM((rt, Dp), h.dtype), pltpu.SemaphoreType.DMA],
        ),
        out_shape=jax.ShapeDtypeStruct((n_rows, Dp), h.dtype),
        compiler_params=pltpu.CompilerParams(
            dimension_semantics=("arbitrary",), vmem_limit_bytes=VMEM_LIMIT),
        name="moe_dispatch",
    )(tails, pos, h)


def _experts_kernel(te_ref, nv_ref, xs_ref, fg_ref, wg_ref, wu_ref, wd_ref, out_ref,
                    y_ref, idx_v_ref, idx_s_ref, sem_y, sem_i, *, n_tokens):
    i = pl.program_id(0)
    ring = i % 2
    nv = nv_ref[i]
    D = y_ref.shape[-1]

    def wait_rows(count, rg):
        def body(r, carry):
            pltpu.make_async_copy(y_ref.at[rg, pl.ds(0, 1)], out_ref.at[pl.ds(0, 1)], sem_y.at[rg]).wait()
            return carry
        lax.fori_loop(0, count, body, 0)

    @pl.when(i >= 2)
    def _():
        wait_rows(nv_ref[jnp.maximum(i - 2, 0)], ring)

    @pl.when(nv > 0)
    def _():
        rec = xs_ref[:, D:]
        first = rec[:, R_E1:R_E1 + 1] == te_ref[i].astype(F32)
        weight = jnp.where(first, rec[:, R_C1:R_C1 + 1], rec[:, R_C2:R_C2 + 1])
        dest = rec[:, R_TOKEN:R_TOKEN + 1] + jnp.where(first, 0.0, float(n_tokens))
        idx_v_ref[...] = jnp.broadcast_to(dest, rec.shape).T[0:8, :].astype(jnp.int32)
        to_smem = pltpu.make_async_copy(idx_v_ref, idx_s_ref.at[ring], sem_i)
        to_smem.start()

        h = _rms(xs_ref[:, :D], fg_ref[...]).astype(BF16)
        a = _dot(h, wg_ref[...].astype(BF16))
        b = _dot(h, wu_ref[...].astype(BF16))
        y_ref[ring] = weight * _dot((_silu(a) * b).astype(BF16), wd_ref[...].astype(BF16))
        to_smem.wait()

        def issue(r, carry):
            pltpu.make_async_copy(y_ref.at[ring, pl.ds(r, 1)], out_ref.at[pl.ds(idx_s_ref[ring, 0, r], 1)],
                                  sem_y.at[ring]).start()
            return carry
        lax.fori_loop(0, nv, issue, 0)

    @pl.when(i == pl.num_programs(0) - 1)
    def _():
        wait_rows(nv_ref[jnp.maximum(i - 1, 0)] * (i >= 1).astype(jnp.int32), 1 - ring)
        wait_rows(nv, ring)


def _experts(tile_expert, tile_rows, xs, fgain, wg, wu, wd, rt, n_tokens):
    n_tiles = tile_expert.shape[0]
    Dp = xs.shape[1]
    E, D, F = wg.shape
    return pl.pallas_call(
        functools.partial(_experts_kernel, n_tokens=n_tokens),
        grid_spec=pltpu.PrefetchScalarGridSpec(
            num_scalar_prefetch=2,
            grid=(n_tiles,),
            in_specs=[
                pl.BlockSpec((rt, Dp), lambda i, te, nv: (i * jnp.minimum(nv[i], 1), 0)),
                pl.BlockSpec((1, D), lambda i, te, nv: (0, 0)),
                pl.BlockSpec((None, D, F), lambda i, te, nv: (te[i], 0, 0)),
                pl.BlockSpec((None, D, F), lambda i, te, nv: (te[i], 0, 0)),
                pl.BlockSpec((None, F, D), lambda i, te, nv: (te[i], 0, 0)),
            ],
            out_specs=pl.BlockSpec(memory_space=pl.ANY),
            scratch_shapes=[
                pltpu.VMEM((2, rt, D), F32),
                pltpu.VMEM((8, rt), jnp.int32),
                pltpu.SMEM((2, 8, rt), jnp.int32),
                pltpu.SemaphoreType.DMA((2,)),
                pltpu.SemaphoreType.DMA,
            ],
        ),
        out_shape=jax.ShapeDtypeStruct((2 * n_tokens, D), F32),
        compiler_params=pltpu.CompilerParams(
            dimension_semantics=("arbitrary",), vmem_limit_bytes=VMEM_LIMIT),
        name="moe_experts",
    )(tile_expert, tile_rows, xs, fgain, wg, wu, wd)


def _combine_kernel(x1_ref, y1_ref, y2_ref, fn_ref, o_ref):
    o_ref[...] = _rms(x1_ref[...] + y1_ref[...] + y2_ref[...], fn_ref[...])


def _combine(x1w, ys, fnorm, tm):
    T = x1w.shape[0]
    D = ys.shape[1]
    n = T // tm
    return pl.pallas_call(
        _combine_kernel,
        grid=(n,),
        in_specs=[
            pl.BlockSpec((tm, D), lambda i: (i, 0)),
            pl.BlockSpec((tm, D), lambda i: (i, 0)),
            pl.BlockSpec((tm, D), lambda i: (n + i, 0)),
            pl.BlockSpec((1, D), lambda i: (0, 0)),
        ],
        out_specs=pl.BlockSpec((tm, D), lambda i: (i, 0)),
        out_shape=jax.ShapeDtypeStruct((T, D), F32),
        compiler_params=pltpu.CompilerParams(
            dimension_semantics=("arbitrary",), vmem_limit_bytes=VMEM_LIMIT),
        name="moe_combine",
    )(x1w, ys, ys, fnorm)


def _pick_tile(T, pref):
    tm = min(pref, T)
    while T % tm:
        tm //= 2
    return tm


def kernel(x, attn_norm, w_in, conv_w, A_log, dt_bias, gdn_norm, w_out, ffn_norm,
           w_group, w_router, w_gate, w_up, w_down, final_norm):
    B, S, D = x.shape
    T = B * S
    depth = w_in.shape[0]
    heads = A_log.shape[1]
    Wb = heads * GDN_HEAD_DIM
    Wa = w_out.shape[1] - Wb
    assert S % MOBA_BLOCK == 0 and S // MOBA_BLOCK <= GATE_ROWS and S % GDN_CHUNK == 0
    assert Wa % LANES == 0 and 2 * heads <= LANES
    assert depth == 1, "the final RMSNorm is fused into the layer's last kernel"

    x2d = x.reshape(T, D)
    for l in range(depth):
        w_all = jnp.pad(w_in[l], ((0, 0), (0, LANES - 2 * heads))).astype(BF16)
        qkv_a, gd, ab = _inproj(x2d, attn_norm[l][None, :], w_all, conv_w[l].astype(F32),
                                3 * Wa, 4 * Wb, _pick_tile(S, 512), S)

        y_a = _moba(qkv_a.reshape(B, S, 3 * Wa), B, S)

        prm = jnp.zeros((8, LANES), F32)
        prm = prm.at[0, :heads].set(A_log[l].astype(F32)).at[1, :heads].set(dt_bias[l].astype(F32))
        y_b = _gdn(gd.reshape(B, S, 4 * Wb), ab.reshape(B, S, LANES), prm,
                   gdn_norm[l][None, :].astype(F32), B, S, heads)

        wr = jnp.concatenate([w_router[l], w_group[l]], axis=1)
        wr = jnp.pad(wr, ((0, 0), (0, LANES - wr.shape[1]))).astype(BF16)
        x1, route_t, counts = _outproj(y_a.reshape(T, Wa), y_b.reshape(T, Wb), x2d,
                                       w_out[l].astype(F32), ffn_norm[l][None, :], wr, _pick_tile(T, 512))

        rt = MOE_ROW_TILE
        n_tiles = -(-2 * T // rt) + N_EXPERTS
        cnt = counts[0, :N_EXPERTS].astype(jnp.int32)
        seg = -(-cnt // rt) * rt
        seg_end = jnp.cumsum(seg)
        seg_start = seg_end - seg
        tile_row = jnp.arange(n_tiles, dtype=jnp.int32) * rt
        tile_expert = jnp.minimum(jnp.sum(tile_row[:, None] >= seg_end[None, :], axis=1), N_EXPERTS - 1)
        tile_rows = jnp.clip((seg_start + cnt)[tile_expert] - tile_row, 0, rt)
        tile_rows = jnp.where(tile_row < seg_end[-1], tile_rows, 0).astype(jnp.int32)
        e_ids = jnp.arange(N_EXPERTS, dtype=jnp.int32)[:, None]

        def dest_rows(e_lane, rank_lane):
            e = route_t[e_lane].astype(jnp.int32)
            return route_t[rank_lane].astype(jnp.int32) + jnp.sum(
                jnp.where(e[None, :] == e_ids, seg_start[:, None], 0), axis=0)

        pos1, pos2 = dest_rows(R_E1, R_RANK1), dest_rows(R_E2, R_RANK2)

        def tiled_pos(tm):
            return jnp.concatenate([pos1.reshape(T // tm, 1, tm), pos2.reshape(T // tm, 1, tm)], axis=2)

        F = w_gate.shape[-1]
        wge = w_gate[l].reshape(N_EXPERTS, D, F)
        wue = w_up[l].reshape(N_EXPERTS, D, F)
        wde = w_down[l].reshape(N_EXPERTS, F, D)
        tmd = _pick_tile(T, 512)
        tails = jnp.concatenate([jnp.where(seg > 0, seg_end // rt - 1, -1), seg_end[-1:] // rt])
        xs = _dispatch(tails, tiled_pos(tmd), x1, n_tiles * rt, tmd, rt)
        ys = _experts(tile_expert.astype(jnp.int32), tile_rows, xs, ffn_norm[l][None, :],
                      wge, wue, wde, rt, T)
        x2d = _combine(x1, ys, final_norm[None, :], _pick_tile(T, 512))
    return x2d.reshape(B, S, D)
```

```python
import functools

import jax
import jax.numpy as jnp
from jax import lax
from jax.experimental import pallas as pl
from jax.experimental.pallas import tpu as pltpu

F32 = jnp.float32
BF16 = jnp.bfloat16

LANES = 128
MOBA_HEAD_DIM = 64
MOBA_BLOCK = 256
MOBA_TOPK = 3
MOBA_Q_SCALE = MOBA_HEAD_DIM ** -0.5 * 1.4426950408889634
GDN_HEAD_DIM = 128
GDN_CONV = 4
GDN_CHUNK = 256
GDN_BATCH_ROWS = 2
N_GROUPS = 4
EXPERTS_PER_GROUP = 8
N_EXPERTS = N_GROUPS * EXPERTS_PER_GROUP
RMS_EPS = 1e-6
NEG = -1e30
GATE_ROWS = 16
MOE_ROW_TILE = 512
ROUTER_SUBTILE = 128
VMEM_LIMIT = 48 * 1024 * 1024


def _dot(a, b):
    return jnp.dot(a, b, preferred_element_type=F32)


def _dot_nt(a, b):
    return lax.dot_general(a, b, (((1,), (1,)), ((), ())), preferred_element_type=F32)


def _sigmoid(x):
    return 1.0 / (1.0 + jnp.exp(-x))


def _silu(x):
    return x * _sigmoid(x)


def _rms(x, gain):
    return x * lax.rsqrt(jnp.mean(x * x, axis=-1, keepdims=True) + RMS_EPS) * gain


def _inproj_kernel(x_ref, gain_ref, w_ref, cw_ref, om_ref, og_ref, oab_ref, prev_ref, *, tiles_per_seq):
    nm, ng = om_ref.shape[1], og_ref.shape[1]
    tm = x_ref.shape[0]

    @pl.when(pl.program_id(0) % tiles_per_seq == 0)
    def _():
        prev_ref[...] = jnp.zeros_like(prev_ref)

    W = ng // 4
    Dh = GDN_HEAD_DIM
    cw = cw_ref[...]
    half = tm // 2
    h_top = _rms(x_ref[:half, :], gain_ref[...]).astype(BF16)
    h_bot = _rms(x_ref[half:, :], gain_ref[...]).astype(BF16)

    def gdn_tail(rows, og, halo):
        raw = og[:, :3 * W]
        ext = jnp.concatenate([halo, raw], axis=0)
        conv = cw[GDN_CONV - 1:GDN_CONV] * raw
        for d in range(1, GDN_CONV):
            conv = conv + cw[GDN_CONV - 1 - d:GDN_CONV - d] * pltpu.roll(ext, d, 0)[8:]
        qkv = _silu(conv)
        for j in range(2 * W // Dh):
            t = qkv[:, j * Dh:(j + 1) * Dh]
            t = t * lax.rsqrt(jnp.sum(t * t, axis=-1, keepdims=True) + 1e-6)
            og_ref[rows, j * Dh:(j + 1) * Dh] = t * (Dh ** -0.5) if j < W // Dh else t
        og_ref[rows, 2 * W:3 * W] = qkv[:, 2 * W:]
        og_ref[rows, 3 * W:] = og[:, 3 * W:]
        return raw[half - 8:, :]

    og_top = _dot(h_top, w_ref[:, nm:nm + ng])
    og_bot = _dot(h_bot, w_ref[:, nm:nm + ng])
    last = gdn_tail(slice(0, half), og_top, prev_ref[...])
    h = jnp.concatenate([h_top, h_bot], axis=0)
    om = _dot(h, w_ref[:, :nm])
    nq = nm // 3
    om_ref[:, :nq] = (om[:, :nq] * MOBA_Q_SCALE).astype(BF16)
    om_ref[:, nq:] = om[:, nq:].astype(BF16)
    prev_ref[...] = gdn_tail(slice(half, tm), og_bot, last)
    oab_ref[...] = _dot(h, w_ref[:, nm + ng:])


def _inproj(x2d, gain, w_all, conv_w, nm, ng, tm, seq_len):
    T, D = x2d.shape
    assert w_all.shape[1] == nm + ng + LANES and nm % LANES == 0 and ng % LANES == 0
    assert seq_len % tm == 0 and tm >= 16
    return pl.pallas_call(
        functools.partial(_inproj_kernel, tiles_per_seq=seq_len // tm),
        grid=(T // tm,),
        in_specs=[
            pl.BlockSpec((tm, D), lambda i: (i, 0)),
            pl.BlockSpec((1, D), lambda i: (0, 0)),
            pl.BlockSpec((D, nm + ng + LANES), lambda i: (0, 0)),
            pl.BlockSpec((GDN_CONV, 3 * ng // 4), lambda i: (0, 0)),
        ],
        out_specs=[
            pl.BlockSpec((tm, nm), lambda i: (i, 0)),
            pl.BlockSpec((tm, ng), lambda i: (i, 0)),
            pl.BlockSpec((tm, LANES), lambda i: (i, 0)),
        ],
        out_shape=[
            jax.ShapeDtypeStruct((T, nm), BF16),
            jax.ShapeDtypeStruct((T, ng), F32),
            jax.ShapeDtypeStruct((T, LANES), F32),
        ],
        scratch_shapes=[pltpu.VMEM((8, 3 * ng // 4), F32)],
        compiler_params=pltpu.CompilerParams(
            dimension_semantics=("arbitrary",), vmem_limit_bytes=VMEM_LIMIT),
        name="inproj",
    )(x2d, gain, w_all, conv_w)


def _moba_kernel(q_ref, k_ref, v_ref, o_ref, *, S):
    nb = S // MOBA_BLOCK
    L = MOBA_BLOCK
    q = q_ref[...]
    k = k_ref[...]
    v = v_ref[...]
    lane = lax.broadcasted_iota(jnp.int32, (1, LANES), 1)
    key_blk = lax.broadcasted_iota(jnp.int32, (S, LANES), 0) // L
    lane_s = lax.broadcasted_iota(jnp.int32, (S, LANES), 1)
    qblk = lax.broadcasted_iota(jnp.int32, (GATE_ROWS, S), 1) // L
    cidx = lax.broadcasted_iota(jnp.int32, (GATE_ROWS, S), 0)
    r_i = lax.broadcasted_iota(jnp.int32, (L, L), 0)
    c_i = lax.broadcasted_iota(jnp.int32, (L, L), 1)
    causal = c_i <= r_i

    km = jnp.mean(k.astype(F32).reshape(nb, L, LANES), axis=1)
    km = jnp.concatenate([km, jnp.zeros((GATE_ROWS - nb, LANES), F32)], axis=0).astype(BF16)

    qa, ka, va, hms = [], [], [], []
    for h in range(2):
        hm = (lane >= MOBA_HEAD_DIM * h) & (lane < MOBA_HEAD_DIM * (h + 1))
        off = MOBA_HEAD_DIM * (1 - h)
        qh = jnp.where(hm, q, jnp.zeros_like(q))
        gate = _dot_nt(km, qh)
        valid = cidx < qblk
        gate = jnp.where(valid, gate, -jnp.inf)
        rank = jnp.zeros((GATE_ROWS, S), F32)
        for i in range(nb):
            gi = gate[i:i + 1, :]
            beats = (gi > gate) | ((gi == gate) & (i < cidx))
            rank = rank + beats.astype(F32)
        sel = valid & (rank < float(MOBA_TOPK))
        pen = jnp.where(sel | (cidx >= qblk), 0.0, NEG)
        pads = [jnp.zeros((r, S), F32) for r in (off, LANES - off - GATE_ROWS)]
        pen = jnp.concatenate(([pads[0]] if off else []) + [pen, pads[1]], axis=0)
        pen_q = pen.T.astype(BF16)
        qa.append(jnp.where(hm, q, pen_q))
        onehot = ((lane_s - off) == key_blk).astype(BF16)
        ka.append(jnp.where(hm, k, onehot))
        va.append(jnp.where(hm, v, jnp.ones_like(v)))
        hms.append(hm)

    HH = range(2)
    for n in range(nb):
        s = [_dot_nt(qa[h][n * L:(n + 1) * L], ka[h][:(n + 1) * L]) for h in HH]
        s_own = [jnp.where(causal, s[h][:, n * L:], NEG) for h in HH]
        m = [jnp.max(s_own[h], axis=-1, keepdims=True) for h in HH]
        if n > 0:
            m = [jnp.maximum(m[h], jnp.max(s[h][:, :n * L], axis=-1, keepdims=True)) for h in HH]
            p = [jnp.concatenate([jnp.exp2(s[h][:, :n * L] - m[h]), jnp.exp2(s_own[h] - m[h])],
                                 axis=1).astype(BF16) for h in HH]
        else:
            p = [jnp.exp2(s_own[h] - m[h]).astype(BF16) for h in HH]
        acc = [_dot(p[h], va[h][:(n + 1) * L]) for h in HH]
        outs = [acc[h] / pltpu.roll(acc[h], MOBA_HEAD_DIM, 1) for h in HH]
        o_ref[n * L:(n + 1) * L, :] = jnp.where(hms[0], outs[0], outs[1]).astype(BF16)


def _moba(qkv, B, S):
    W = qkv.shape[-1] // 3
    npair = W // LANES
    return pl.pallas_call(
        functools.partial(_moba_kernel, S=S),
        grid=(B, npair),
        in_specs=[
            pl.BlockSpec((None, S, LANES), lambda b, j: (b, 0, j)),
            pl.BlockSpec((None, S, LANES), lambda b, j: (b, 0, npair + j)),
            pl.BlockSpec((None, S, LANES), lambda b, j: (b, 0, 2 * npair + j)),
        ],
        out_specs=pl.BlockSpec((None, S, LANES), lambda b, j: (b, 0, j)),
        out_shape=jax.ShapeDtypeStruct((B, S, W), BF16),
        compiler_params=pltpu.CompilerParams(
            dimension_semantics=("arbitrary", "arbitrary"), vmem_limit_bytes=VMEM_LIMIT),
        name="moba",
    )(qkv, qkv, qkv)


def _gdn_kernel(gd_ref, ab_ref, prm_ref, gn_ref, o_ref, state_ref, *, heads):
    C = GDN_CHUNK
    Dh = GDN_HEAD_DIM
    W = heads * Dh
    c = pl.program_id(1)

    @pl.when(c == 0)
    def _():
        state_ref[...] = jnp.zeros_like(state_ref)

    nbatch = gd_ref.shape[0]
    prm = prm_ref[...]
    rows = lax.broadcasted_iota(jnp.int32, (C, LANES), 0)
    gds, Gs, GTs, betas = [], [], [], []
    for bi in range(nbatch):
        gds.append(gd_ref[bi])
        ab = ab_ref[bi]
        sp_in = ab + prm[1:2]
        softplus = jnp.maximum(sp_in, 0.0) + jnp.log1p(jnp.exp(-jnp.abs(sp_in)))
        G = -jnp.exp(prm[0:1]) * softplus
        betas.append(_sigmoid(ab))
        d = 1
        while d < C:
            G = G + jnp.where(rows >= d, pltpu.roll(G, d, 0), 0.0)
            d *= 2
        Gs.append(G)
        GTs.append(G.T)

    r_i = lax.broadcasted_iota(jnp.int32, (C, C), 0)
    c_i = lax.broadcasted_iota(jnp.int32, (C, C), 1)
    incl = c_i <= r_i
    strict = c_i < r_i
    rc = r_i ^ c_i
    level = jnp.full((C, C), -1, jnp.int32)
    for b in range(C.bit_length() - 1):
        level = level + (rc >= (1 << b)).astype(jnp.int32)
    level_b = level.astype(F32).astype(BF16)
    gn = gn_ref[...]

    chains = [(bi, hd) for bi in range(nbatch) for hd in range(heads)]
    H = range(len(chains))
    q = [gds[bi][:, hd * Dh:(hd + 1) * Dh] for bi, hd in chains]
    k = [gds[bi][:, W + hd * Dh:W + (hd + 1) * Dh] for bi, hd in chains]
    v = [gds[bi][:, 2 * W + hd * Dh:2 * W + (hd + 1) * Dh] for bi, hd in chains]
    Gc = [Gs[bi][:, hd:hd + 1] for bi, hd in chains]
    Gr = [GTs[bi][hd:hd + 1, :] for bi, hd in chains]
    bc = [betas[bi][:, heads + hd:heads + hd + 1] for bi, hd in chains]
    gl = [Gs[bi][C - 1:C, hd:hd + 1] for bi, hd in chains]

    kb = [t.astype(BF16) for t in k]
    decay = [jnp.exp(jnp.where(incl, Gc[h] - Gr[h], -jnp.inf)) for h in H]
    A = [jnp.where(strict, bc[h] * _dot_nt(kb[h], kb[h]) * decay[h], 0.0) for h in H]
    qk = [(_dot_nt(q[h].astype(BF16), kb[h]) * decay[h]).astype(BF16) for h in H]
    gam = [jnp.exp(t) for t in Gc]
    X = [jnp.concatenate([bc[h] * v[h], (bc[h] * gam[h]) * k[h]], axis=1) for h in H]
    Ab = [t.astype(BF16) for t in A]
    zero = jnp.zeros((C, C), BF16)
    Tm = [jnp.where(level_b == -1.0, jnp.ones((C, C), BF16), jnp.where(level_b == 0.0, -t, zero))
          for t in Ab]
    for lv in range(1, C.bit_length() - 2):
        at_lv = level_b == float(lv)
        E = [jnp.where(at_lv, t, zero) for t in Ab]
        F = [_dot(E[h], Tm[h]).astype(BF16) for h in H]
        Tm = [Tm[h] - _dot(Tm[h], F[h]).astype(BF16) for h in H]
    hc = C // 2
    F21 = [_dot(Ab[h][hc:, :hc], Tm[h][:hc, :hc]).astype(BF16) for h in H]
    T21 = [-_dot(Tm[h][hc:, hc:], F21[h]).astype(BF16) for h in H]
    Tm = [jnp.concatenate([Tm[h][:hc, :], jnp.concatenate([T21[h], Tm[h][hc:, hc:]], axis=1)], axis=0)
          for h in H]
    X = [X[h] + _dot(jnp.where(level_b == -1.0, zero, Tm[h]), X[h].astype(BF16)) for h in H]

    S0 = [state_ref[h] for h in H]
    Sb = [t.astype(BF16) for t in S0]
    ub = [(X[h][:, :Dh] - _dot(X[h][:, Dh:].astype(BF16), Sb[h])).astype(BF16) for h in H]
    o = [_dot((q[h] * gam[h]).astype(BF16), Sb[h]) + _dot(qk[h], ub[h]) for h in H]
    k_dec = [(k[h] * jnp.exp(gl[h] - Gc[h])).T.astype(BF16) for h in H]
    for h in H:
        state_ref[h] = jnp.exp(gl[h]) * S0[h] + _dot(k_dec[h], ub[h])
    for h, (bi, hd) in enumerate(chains):
        z = gds[bi][:, 3 * W + hd * Dh:3 * W + (hd + 1) * Dh]
        o_ref[bi, :, hd * Dh:(hd + 1) * Dh] = (_rms(o[h], gn) * _silu(z)).astype(BF16)


def _gdn(gd, ab, prm, gdn_norm, B, S, heads):
    C = GDN_CHUNK
    W = heads * GDN_HEAD_DIM
    nbatch = GDN_BATCH_ROWS if B % GDN_BATCH_ROWS == 0 else 1
    return pl.pallas_call(
        functools.partial(_gdn_kernel, heads=heads),
        grid=(B // nbatch, S // C),
        in_specs=[
            pl.BlockSpec((nbatch, C, 4 * W), lambda b, c: (b, c, 0)),
            pl.BlockSpec((nbatch, C, LANES), lambda b, c: (b, c, 0)),
            pl.BlockSpec((8, LANES), lambda b, c: (0, 0)),
            pl.BlockSpec((1, GDN_HEAD_DIM), lambda b, c: (0, 0)),
        ],
        out_specs=pl.BlockSpec((nbatch, C, W), lambda b, c: (b, c, 0)),
        out_shape=jax.ShapeDtypeStruct((B, S, W), BF16),
        scratch_shapes=[pltpu.VMEM((nbatch * heads, GDN_HEAD_DIM, GDN_HEAD_DIM), F32)],
        compiler_params=pltpu.CompilerParams(
            dimension_semantics=("arbitrary", "arbitrary"), vmem_limit_bytes=VMEM_LIMIT),
        name="gdn",
    )(gd, ab, prm, gdn_norm)


R_E1, R_E2, R_C1, R_C2, R_RANK1, R_RANK2 = range(6)


def _outproj_kernel(ya_ref, yb_ref, x_ref, wo_ref, fg_ref, wr_ref,
                    x1_ref, route_ref, route_t_ref, cnt_ref, carry_ref, wo16_ref):
    @pl.when(pl.program_id(0) == 0)
    def _():
        carry_ref[...] = jnp.zeros_like(carry_ref)
        wo16_ref[...] = wo_ref[...].astype(BF16)

    wa = ya_ref.shape[1]
    ts = ROUTER_SUBTILE
    subs = [pl.ds(j * ts, ts) for j in range(x_ref.shape[0] // ts)]
    each = lambda f, *lists: [f(*args) for args in zip(*lists)]
    rmax = lambda t: jnp.max(t, axis=-1, keepdims=True)
    rmin = lambda t: jnp.min(t, axis=-1, keepdims=True)
    rsum = lambda t: jnp.sum(t, axis=-1, keepdims=True)

    x1 = [x_ref[sl, :] + _dot(ya_ref[sl, :], wo16_ref[:wa, :]) + _dot(yb_ref[sl, :], wo16_ref[wa:, :])
          for sl in subs]
    for sl, t in zip(subs, x1):
        x1_ref[sl, :] = t
    h = each(lambda t: _rms(t, fg_ref[...]).astype(BF16), x1)
    logits = each(lambda t: _dot(t, wr_ref[...]), h)
    lane = lax.broadcasted_iota(jnp.int32, (ts, LANES), 1).astype(F32)
    big = float(LANES)

    is_g = (lane >= N_EXPERTS) & (lane < N_EXPERTS + N_GROUPS)
    lg = each(lambda t: jnp.where(is_g, t, -jnp.inf), logits)
    mg = each(rmax, lg)
    g_sel = each(lambda a, b: rmin(jnp.where(a == b, lane, big)) - N_EXPERTS, lg, mg)
    p_group = each(lambda a, b: 1.0 / rsum(jnp.exp(a - b)), lg, mg)

    le = each(lambda t, g: jnp.where((lane >= g * EXPERTS_PER_GROUP) & (lane < (g + 1) * EXPERTS_PER_GROUP),
                                     t, -jnp.inf), logits, g_sel)
    m1 = each(rmax, le)
    i1 = each(lambda a, b: rmin(jnp.where(a == b, lane, big)), le, m1)
    le2 = each(lambda a, i: jnp.where(lane == i, -jnp.inf, a), le, i1)
    m2 = each(rmax, le2)
    i2 = each(lambda a, b: rmin(jnp.where(a == b, lane, big)), le2, m2)
    se = each(lambda a, b: rsum(jnp.exp(a - b)), le, m1)
    p1 = each(lambda s: 1.0 / s, se)
    p2 = each(lambda a, b, s: jnp.exp(a - b) / s, m2, m1, se)
    c1 = each(lambda pg, a, b: pg * (a / (a + b)), p_group, p1, p2)
    c2 = each(lambda pg, a, b: pg * (b / (a + b)), p_group, p1, p2)

    hot = each(lambda a, b: ((lane == a) | (lane == b)).astype(BF16), i1, i2)
    r_i = lax.broadcasted_iota(jnp.int32, (ts, ts), 0)
    c_i = lax.broadcasted_iota(jnp.int32, (ts, ts), 1)
    before = (c_i < r_i).astype(BF16)
    within = each(lambda t: _dot(before, t), hot)
    totals = each(lambda t: jnp.sum(t.astype(F32), axis=0, keepdims=True), hot)
    base = [carry_ref[0:1, :]]
    for t in totals:
        base.append(base[-1] + t)
    seen = each(lambda a, b: a + b, within, base[:-1])
    rank1 = each(lambda i, s: rsum(jnp.where(lane == i, s, 0.0)), i1, seen)
    rank2 = each(lambda i, s: rsum(jnp.where(lane == i, s, 0.0)), i2, seen)
    carry_ref[...] = jnp.broadcast_to(base[-1], carry_ref.shape)
    cnt_ref[...] = jnp.broadcast_to(base[-1], cnt_ref.shape)

    for j, sl in enumerate(subs):
        rec = jnp.zeros((ts, LANES), F32)
        for slot, val in ((R_E1, i1), (R_E2, i2), (R_C1, c1), (R_C2, c2), (R_RANK1, rank1), (R_RANK2, rank2)):
            rec = jnp.where(lane == float(slot), val[j], rec)
        route_ref[sl, :] = rec
        route_t_ref[:, sl] = rec.T[0:8, :]


def _outproj(ya, yb, x2d, wo, fgain, wr, tm):
    T, D = x2d.shape
    Wa, Wb = ya.shape[1], yb.shape[1]
    assert tm % ROUTER_SUBTILE == 0
    return pl.pallas_call(
        _outproj_kernel,
        grid=(T // tm,),
        in_specs=[
            pl.BlockSpec((tm, Wa), lambda i: (i, 0)),
            pl.BlockSpec((tm, Wb), lambda i: (i, 0)),
            pl.BlockSpec((tm, D), lambda i: (i, 0)),
            pl.BlockSpec((Wa + Wb, D), lambda i: (0, 0)),
            pl.BlockSpec((1, D), lambda i: (0, 0)),
            pl.BlockSpec((D, LANES), lambda i: (0, 0)),
        ],
        out_specs=[
            pl.BlockSpec((tm, D), lambda i: (i, 0)),
            pl.BlockSpec((tm, LANES), lambda i: (i, 0)),
            pl.BlockSpec((8, tm), lambda i: (0, i)),
            pl.BlockSpec((8, LANES), lambda i: (0, 0)),
        ],
        out_shape=[
            jax.ShapeDtypeStruct((T, D), F32),
            jax.ShapeDtypeStruct((T, LANES), F32),
            jax.ShapeDtypeStruct((8, T), F32),
            jax.ShapeDtypeStruct((8, LANES), F32),
        ],
        scratch_shapes=[pltpu.VMEM((8, LANES), F32), pltpu.VMEM((Wa + Wb, D), BF16)],
        compiler_params=pltpu.CompilerParams(
            dimension_semantics=("arbitrary",), vmem_limit_bytes=VMEM_LIMIT),
        name="outproj_router",
    )(ya, yb, x2d, wo, fgain, wr)


def _dispatch_kernel(tail_ref, pos_ref, x_ref, xs_ref, zero_ref, sem):
    tm = x_ref.shape[0]
    rt = zero_ref.shape[0]

    @pl.when(pl.program_id(0) == 0)
    def _():
        zero_ref[...] = jnp.zeros_like(zero_ref)

        def tile_fill(j):
            return pltpu.make_async_copy(zero_ref, xs_ref.at[pl.ds(pl.multiple_of(j * rt, 8), rt)], sem)

        def for_each_fill(act):
            for e in range(N_EXPERTS):
                pl.when(tail_ref[e] >= 0)(lambda e=e: act(tile_fill(tail_ref[e])))
            lax.fori_loop(tail_ref[N_EXPERTS], xs_ref.shape[0] // rt, lambda j, c: (act(tile_fill(j)), c)[1], 0)

        for_each_fill(lambda f: f.start())
        for_each_fill(lambda f: f.wait())

    def issue(r, carry):
        for slot in range(2):
            p = pos_ref[0, slot * tm + r]
            pltpu.make_async_copy(x_ref.at[pl.ds(r, 1)], xs_ref.at[pl.ds(p, 1)], sem).start()
        return carry

    lax.fori_loop(0, tm, issue, 0, unroll=8)
    for _ in range(2):
        pltpu.make_async_copy(x_ref, xs_ref.at[pl.ds(0, tm)], sem).wait()


def _dispatch(tails, pos, h, n_rows, tm, rt):
    T, Dp = h.shape
    return pl.pallas_call(
        _dispatch_kernel,
        grid_spec=pltpu.PrefetchScalarGridSpec(
            num_scalar_prefetch=1,
            grid=(T // tm,),
            in_specs=[
                pl.BlockSpec((None, 1, 2 * tm), lambda i, tl: (i, 0, 0), memory_space=pltpu.SMEM),
                pl.BlockSpec((tm, Dp), lambda i, tl: (i, 0)),
            ],
            out_specs=pl.BlockSpec(memory_space=pl.ANY),
            scratch_shapes=[pltpu.VMEM((rt, Dp), h.dtype), pltpu.SemaphoreType.DMA],
        ),
        out_shape=jax.ShapeDtypeStruct((n_rows, Dp), h.dtype),
        compiler_params=pltpu.CompilerParams(
            dimension_semantics=("arbitrary",), vmem_limit_bytes=VMEM_LIMIT),
        name="moe_dispatch",
    )(tails, pos, h)


def _experts_kernel(te_ref, tv_ref, xs_ref, fg_ref, wg_ref, wu_ref, wd_ref, y_ref):
    del te_ref
    i = pl.program_id(0)

    @pl.when(tv_ref[i] > 0)
    def _():
        h = _rms(xs_ref[...], fg_ref[...]).astype(BF16)
        a = _dot(h, wg_ref[...].astype(BF16))
        b = _dot(h, wu_ref[...].astype(BF16))
        y_ref[...] = _dot((_silu(a) * b).astype(BF16), wd_ref[...].astype(BF16))

    @pl.when(tv_ref[i] == 0)
    def _():
        y_ref[...] = jnp.zeros_like(y_ref)


def _experts(tile_expert, tile_valid, xs, fgain, wg, wu, wd, rt):
    n_rows = tile_expert.shape[0] * rt
    Dp = xs.shape[1]
    E, D, F = wg.shape
    return pl.pallas_call(
        _experts_kernel,
        grid_spec=pltpu.PrefetchScalarGridSpec(
            num_scalar_prefetch=2,
            grid=(n_rows // rt,),
            in_specs=[
                pl.BlockSpec((rt, Dp), lambda i, te, tv: (i * tv[i], 0)),
                pl.BlockSpec((1, D), lambda i, te, tv: (0, 0)),
                pl.BlockSpec((None, D, F), lambda i, te, tv: (te[i], 0, 0)),
                pl.BlockSpec((None, D, F), lambda i, te, tv: (te[i], 0, 0)),
                pl.BlockSpec((None, F, D), lambda i, te, tv: (te[i], 0, 0)),
            ],
            out_specs=pl.BlockSpec((rt, D), lambda i, te, tv: (i, 0)),
        ),
        out_shape=jax.ShapeDtypeStruct((n_rows, D), F32),
        compiler_params=pltpu.CompilerParams(
            dimension_semantics=("arbitrary",), vmem_limit_bytes=VMEM_LIMIT),
        name="moe_experts",
    )(tile_expert, tile_valid, xs, fgain, wg, wu, wd)


def _combine_kernel(pos_ref, pos_next_ref, x1_ref, route_ref, fn_ref, y_ref, o_ref, buf_ref, sem):
    tm = x1_ref.shape[0]
    i = pl.program_id(0)
    ring = i % 2

    def row_copy(p_ref, r, slot, rg):
        p = p_ref[0, slot * tm + r]
        return pltpu.make_async_copy(y_ref.at[pl.ds(p, 1)], buf_ref.at[rg, slot, pl.ds(r, 1)], sem.at[rg])

    def issue(p_ref, rg):
        def body(r, carry):
            for slot in range(2):
                row_copy(p_ref, r, slot, rg).start()
            return carry
        lax.fori_loop(0, tm, body, 0, unroll=8)

    @pl.when(i == 0)
    def _():
        issue(pos_ref, ring)

    @pl.when(i + 1 < pl.num_programs(0))
    def _():
        issue(pos_next_ref, 1 - ring)

    for slot in range(2):
        pltpu.make_async_copy(y_ref.at[pl.ds(0, tm)], buf_ref.at[ring, slot], sem.at[ring]).wait()
    route = route_ref[...]
    c1 = route[:, R_C1:R_C1 + 1]
    c2 = route[:, R_C2:R_C2 + 1]
    o_ref[...] = _rms(x1_ref[...] + c1 * buf_ref[ring, 0] + c2 * buf_ref[ring, 1], fn_ref[...])


def _combine(pos, x1, route, fnorm, y, tm):
    T, D = x1.shape
    n = T // tm
    return pl.pallas_call(
        _combine_kernel,
        grid=(n,),
        in_specs=[
            pl.BlockSpec((None, 1, 2 * tm), lambda i: (i, 0, 0), memory_space=pltpu.SMEM),
            pl.BlockSpec((None, 1, 2 * tm), lambda i: (jnp.minimum(i + 1, n - 1), 0, 0),
                         memory_space=pltpu.SMEM),
            pl.BlockSpec((tm, D), lambda i: (i, 0)),
            pl.BlockSpec((tm, LANES), lambda i: (i, 0)),
            pl.BlockSpec((1, D), lambda i: (0, 0)),
            pl.BlockSpec(memory_space=pl.ANY),
        ],
        out_specs=pl.BlockSpec((tm, D), lambda i: (i, 0)),
        out_shape=jax.ShapeDtypeStruct((T, D), F32),
        scratch_shapes=[pltpu.VMEM((2, 2, tm, D), F32), pltpu.SemaphoreType.DMA((2,))],
        compiler_params=pltpu.CompilerParams(
            dimension_semantics=("arbitrary",), vmem_limit_bytes=VMEM_LIMIT),
        name="moe_combine",
    )(pos, pos, x1, route, fnorm, y)


def _pick_tile(T, pref):
    tm = min(pref, T)
    while T % tm:
        tm //= 2
    return tm


def kernel(x, attn_norm, w_in, conv_w, A_log, dt_bias, gdn_norm, w_out, ffn_norm,
           w_group, w_router, w_gate, w_up, w_down, final_norm):
    B, S, D = x.shape
    T = B * S
    depth = w_in.shape[0]
    heads = A_log.shape[1]
    Wb = heads * GDN_HEAD_DIM
    Wa = w_out.shape[1] - Wb
    assert S % MOBA_BLOCK == 0 and S // MOBA_BLOCK <= GATE_ROWS and S % GDN_CHUNK == 0
    assert Wa % LANES == 0 and 2 * heads <= LANES
    assert depth == 1, "the final RMSNorm is fused into the layer's last kernel"

    x2d = x.reshape(T, D)
    for l in range(depth):
        w_all = jnp.pad(w_in[l], ((0, 0), (0, LANES - 2 * heads))).astype(BF16)
        qkv_a, gd, ab = _inproj(x2d, attn_norm[l][None, :], w_all, conv_w[l].astype(F32),
                                3 * Wa, 4 * Wb, _pick_tile(S, 512), S)

        y_a = _moba(qkv_a.reshape(B, S, 3 * Wa), B, S)

        prm = jnp.zeros((8, LANES), F32)
        prm = prm.at[0, :heads].set(A_log[l].astype(F32)).at[1, :heads].set(dt_bias[l].astype(F32))
        y_b = _gdn(gd.reshape(B, S, 4 * Wb), ab.reshape(B, S, LANES), prm,
                   gdn_norm[l][None, :].astype(F32), B, S, heads)

        wr = jnp.concatenate([w_router[l], w_group[l]], axis=1)
        wr = jnp.pad(wr, ((0, 0), (0, LANES - wr.shape[1]))).astype(BF16)
        x1, route, route_t, counts = _outproj(y_a.reshape(T, Wa), y_b.reshape(T, Wb), x2d,
                                              w_out[l].astype(F32), ffn_norm[l][None, :], wr, _pick_tile(T, 512))

        rt = MOE_ROW_TILE
        n_tiles = -(-2 * T // rt) + N_EXPERTS
        cnt = counts[0, :N_EXPERTS].astype(jnp.int32)
        seg = -(-cnt // rt) * rt
        seg_end = jnp.cumsum(seg)
        seg_start = seg_end - seg
        tile_row = jnp.arange(n_tiles, dtype=jnp.int32) * rt
        tile_expert = jnp.minimum(jnp.sum(tile_row[:, None] >= seg_end[None, :], axis=1), N_EXPERTS - 1)
        tile_valid = (tile_row < seg_end[-1]).astype(jnp.int32)
        e_ids = jnp.arange(N_EXPERTS, dtype=jnp.int32)[:, None]

        def dest_rows(e_lane, rank_lane):
            e = route_t[e_lane].astype(jnp.int32)
            return route_t[rank_lane].astype(jnp.int32) + jnp.sum(
                jnp.where(e[None, :] == e_ids, seg_start[:, None], 0), axis=0)

        pos1, pos2 = dest_rows(R_E1, R_RANK1), dest_rows(R_E2, R_RANK2)

        def tiled_pos(tm):
            return jnp.concatenate([pos1.reshape(T // tm, 1, tm), pos2.reshape(T // tm, 1, tm)], axis=2)

        F = w_gate.shape[-1]
        wge = w_gate[l].reshape(N_EXPERTS, D, F)
        wue = w_up[l].reshape(N_EXPERTS, D, F)
        wde = w_down[l].reshape(N_EXPERTS, F, D)
        tmd = _pick_tile(T, 512)
        tails = jnp.concatenate([jnp.where(seg > 0, seg_end // rt - 1, -1), seg_end[-1:] // rt])
        xs = _dispatch(tails, tiled_pos(tmd), x1, n_tiles * rt, tmd, rt)
        ys = _experts(tile_expert.astype(jnp.int32), tile_valid, xs, ffn_norm[l][None, :],
                      wge, wue, wde, rt)
        tmc = _pick_tile(T, 512)
        x2d = _combine(tiled_pos(tmc), x1, route, final_norm[None, :], ys, tmc)
    return x2d.reshape(B, S, D)
```

```python
import functools

import jax
import jax.numpy as jnp
from jax import lax
from jax.experimental import pallas as pl
from jax.experimental.pallas import tpu as pltpu

F32 = jnp.float32
BF16 = jnp.bfloat16

LANES = 128
MOBA_HEAD_DIM = 64
MOBA_BLOCK = 256
MOBA_TOPK = 3
MOBA_BATCH_ROWS = 2
MOBA_Q_SCALE = MOBA_HEAD_DIM ** -0.5 * 1.4426950408889634
GDN_HEAD_DIM = 128
GDN_CONV = 4
GDN_CHUNK = 256
GDN_BATCH_ROWS = 2
N_GROUPS = 4
EXPERTS_PER_GROUP = 8
N_EXPERTS = N_GROUPS * EXPERTS_PER_GROUP
RMS_EPS = 1e-6
NEG = -1e30
GATE_ROWS = 16
MOE_ROW_TILE = 512
ROUTER_SUBTILE = 128
VMEM_LIMIT = 48 * 1024 * 1024


def _dot(a, b):
    return jnp.dot(a, b, preferred_element_type=F32)


def _dot_nt(a, b):
    return lax.dot_general(a, b, (((1,), (1,)), ((), ())), preferred_element_type=F32)


def _sigmoid(x):
    return 1.0 / (1.0 + jnp.exp(-x))


def _silu(x):
    return x * _sigmoid(x)


def _rms(x, gain):
    return x * lax.rsqrt(jnp.mean(x * x, axis=-1, keepdims=True) + RMS_EPS) * gain


def _inproj_kernel(x_ref, gain_ref, w_ref, cw_ref, om_ref, og_ref, oab_ref, prev_ref, *, tiles_per_seq):
    nm, ng = om_ref.shape[1], og_ref.shape[1]
    tm = x_ref.shape[0]

    @pl.when(pl.program_id(0) % tiles_per_seq == 0)
    def _():
        prev_ref[...] = jnp.zeros_like(prev_ref)

    W = ng // 4
    Dh = GDN_HEAD_DIM
    cw = cw_ref[...]
    half = tm // 2
    h_top = _rms(x_ref[:half, :], gain_ref[...]).astype(BF16)
    h_bot = _rms(x_ref[half:, :], gain_ref[...]).astype(BF16)

    def gdn_tail(rows, og, halo):
        raw = og[:, :3 * W]
        ext = jnp.concatenate([halo, raw], axis=0)
        conv = cw[GDN_CONV - 1:GDN_CONV] * raw
        for d in range(1, GDN_CONV):
            conv = conv + cw[GDN_CONV - 1 - d:GDN_CONV - d] * pltpu.roll(ext, d, 0)[8:]
        qkv = _silu(conv)
        for j in range(2 * W // Dh):
            t = qkv[:, j * Dh:(j + 1) * Dh]
            t = t * lax.rsqrt(jnp.sum(t * t, axis=-1, keepdims=True) + 1e-6)
            og_ref[rows, j * Dh:(j + 1) * Dh] = t * (Dh ** -0.5) if j < W // Dh else t
        og_ref[rows, 2 * W:3 * W] = qkv[:, 2 * W:]
        og_ref[rows, 3 * W:] = og[:, 3 * W:]
        return raw[half - 8:, :]

    og_top = _dot(h_top, w_ref[:, nm:nm + ng])
    og_bot = _dot(h_bot, w_ref[:, nm:nm + ng])
    last = gdn_tail(slice(0, half), og_top, prev_ref[...])
    h = jnp.concatenate([h_top, h_bot], axis=0)
    om = _dot(h, w_ref[:, :nm])
    nq = nm // 3
    om_ref[:, :nq] = (om[:, :nq] * MOBA_Q_SCALE).astype(BF16)
    om_ref[:, nq:] = om[:, nq:].astype(BF16)
    prev_ref[...] = gdn_tail(slice(half, tm), og_bot, last)
    oab_ref[...] = _dot(h, w_ref[:, nm + ng:])


def _inproj(x2d, gain, w_all, conv_w, nm, ng, tm, seq_len):
    T, D = x2d.shape
    assert w_all.shape[1] == nm + ng + LANES and nm % LANES == 0 and ng % LANES == 0
    assert seq_len % tm == 0 and tm >= 16
    return pl.pallas_call(
        functools.partial(_inproj_kernel, tiles_per_seq=seq_len // tm),
        grid=(T // tm,),
        in_specs=[
            pl.BlockSpec((tm, D), lambda i: (i, 0)),
            pl.BlockSpec((1, D), lambda i: (0, 0)),
            pl.BlockSpec((D, nm + ng + LANES), lambda i: (0, 0)),
            pl.BlockSpec((GDN_CONV, 3 * ng // 4), lambda i: (0, 0)),
        ],
        out_specs=[
            pl.BlockSpec((tm, nm), lambda i: (i, 0)),
            pl.BlockSpec((tm, ng), lambda i: (i, 0)),
            pl.BlockSpec((tm, LANES), lambda i: (i, 0)),
        ],
        out_shape=[
            jax.ShapeDtypeStruct((T, nm), BF16),
            jax.ShapeDtypeStruct((T, ng), F32),
            jax.ShapeDtypeStruct((T, LANES), F32),
        ],
        scratch_shapes=[pltpu.VMEM((8, 3 * ng // 4), F32)],
        compiler_params=pltpu.CompilerParams(
            dimension_semantics=("arbitrary",), vmem_limit_bytes=VMEM_LIMIT),
        name="inproj",
    )(x2d, gain, w_all, conv_w)


def _moba_kernel(q_ref, k_ref, v_ref, o_ref, *, S):
    nb = S // MOBA_BLOCK
    L = MOBA_BLOCK
    lane = lax.broadcasted_iota(jnp.int32, (1, LANES), 1)
    key_blk = lax.broadcasted_iota(jnp.int32, (S, LANES), 0) // L
    lane_s = lax.broadcasted_iota(jnp.int32, (S, LANES), 1)
    qblk = lax.broadcasted_iota(jnp.int32, (GATE_ROWS, S), 1) // L
    cidx = lax.broadcasted_iota(jnp.int32, (GATE_ROWS, S), 0)
    r_i = lax.broadcasted_iota(jnp.int32, (L, L), 0)
    c_i = lax.broadcasted_iota(jnp.int32, (L, L), 1)
    causal = c_i <= r_i
    hms = [(lane >= MOBA_HEAD_DIM * h) & (lane < MOBA_HEAD_DIM * (h + 1)) for h in range(2)]

    chains = [(bi, h) for bi in range(q_ref.shape[0]) for h in range(2)]
    qa, ka, va = [], [], []
    for bi, h in chains:
        q, k, v = q_ref[bi], k_ref[bi], v_ref[bi]
        hm = hms[h]
        off = MOBA_HEAD_DIM * (1 - h)
        km = jnp.mean(k.astype(F32).reshape(nb, L, LANES), axis=1)
        km = jnp.concatenate([km, jnp.zeros((GATE_ROWS - nb, LANES), F32)], axis=0).astype(BF16)
        qh = jnp.where(hm, q, jnp.zeros_like(q))
        gate = _dot_nt(km, qh)
        valid = cidx < qblk
        gate = jnp.where(valid, gate, -jnp.inf)
        rank = jnp.zeros((GATE_ROWS, S), F32)
        for i in range(nb):
            gi = gate[i:i + 1, :]
            beats = (gi > gate) | ((gi == gate) & (i < cidx))
            rank = rank + beats.astype(F32)
        sel = valid & (rank < float(MOBA_TOPK))
        pen = jnp.where(sel | (cidx >= qblk), 0.0, NEG)
        pads = [jnp.zeros((r, S), F32) for r in (off, LANES - off - GATE_ROWS)]
        pen = jnp.concatenate(([pads[0]] if off else []) + [pen, pads[1]], axis=0)
        pen_q = pen.T.astype(BF16)
        qa.append(jnp.where(hm, q, pen_q))
        onehot = ((lane_s - off) == key_blk).astype(BF16)
        ka.append(jnp.where(hm, k, onehot))
        va.append(jnp.where(hm, v, jnp.ones_like(v)))

    HH = range(len(chains))
    for n in range(nb):
        s = [_dot_nt(qa[h][n * L:(n + 1) * L], ka[h][:(n + 1) * L]) for h in HH]
        s_own = [jnp.where(causal, s[h][:, n * L:], NEG) for h in HH]
        m = [jnp.max(s_own[h], axis=-1, keepdims=True) for h in HH]
        if n > 0:
            m = [jnp.maximum(m[h], jnp.max(s[h][:, :n * L], axis=-1, keepdims=True)) for h in HH]
            p = [jnp.concatenate([jnp.exp2(s[h][:, :n * L] - m[h]), jnp.exp2(s_own[h] - m[h])],
                                 axis=1).astype(BF16) for h in HH]
        else:
            p = [jnp.exp2(s_own[h] - m[h]).astype(BF16) for h in HH]
        acc = [_dot(p[h], va[h][:(n + 1) * L]) for h in HH]
        outs = [acc[h] / pltpu.roll(acc[h], MOBA_HEAD_DIM, 1) for h in HH]
        for bi in range(q_ref.shape[0]):
            o_ref[bi, n * L:(n + 1) * L, :] = jnp.where(hms[0], outs[2 * bi], outs[2 * bi + 1]).astype(BF16)


def _moba(qkv, B, S):
    W = qkv.shape[-1] // 3
    npair = W // LANES
    nbr = MOBA_BATCH_ROWS if B % MOBA_BATCH_ROWS == 0 else 1
    return pl.pallas_call(
        functools.partial(_moba_kernel, S=S),
        grid=(B // nbr, npair),
        in_specs=[
            pl.BlockSpec((nbr, S, LANES), lambda b, j: (b, 0, j)),
            pl.BlockSpec((nbr, S, LANES), lambda b, j: (b, 0, npair + j)),
            pl.BlockSpec((nbr, S, LANES), lambda b, j: (b, 0, 2 * npair + j)),
        ],
        out_specs=pl.BlockSpec((nbr, S, LANES), lambda b, j: (b, 0, j)),
        out_shape=jax.ShapeDtypeStruct((B, S, W), BF16),
        compiler_params=pltpu.CompilerParams(
            dimension_semantics=("arbitrary", "arbitrary"), vmem_limit_bytes=VMEM_LIMIT),
        name="moba",
    )(qkv, qkv, qkv)


def _gdn_kernel(gd_ref, ab_ref, prm_ref, gn_ref, o_ref, state_ref, *, heads):
    C = GDN_CHUNK
    Dh = GDN_HEAD_DIM
    W = heads * Dh
    c = pl.program_id(1)

    @pl.when(c == 0)
    def _():
        state_ref[...] = jnp.zeros_like(state_ref)

    nbatch = gd_ref.shape[0]
    prm = prm_ref[...]
    rows = lax.broadcasted_iota(jnp.int32, (C, LANES), 0)
    gds, Gs, GTs, betas = [], [], [], []
    for bi in range(nbatch):
        gds.append(gd_ref[bi])
        ab = ab_ref[bi]
        sp_in = ab + prm[1:2]
        softplus = jnp.maximum(sp_in, 0.0) + jnp.log1p(jnp.exp(-jnp.abs(sp_in)))
        G = -jnp.exp(prm[0:1]) * softplus
        betas.append(_sigmoid(ab))
        d = 1
        while d < C:
            G = G + jnp.where(rows >= d, pltpu.roll(G, d, 0), 0.0)
            d *= 2
        Gs.append(G)
        GTs.append(G.T)

    r_i = lax.broadcasted_iota(jnp.int32, (C, C), 0)
    c_i = lax.broadcasted_iota(jnp.int32, (C, C), 1)
    incl = c_i <= r_i
    strict = c_i < r_i
    rc = r_i ^ c_i
    level = jnp.full((C, C), -1, jnp.int32)
    for b in range(C.bit_length() - 1):
        level = level + (rc >= (1 << b)).astype(jnp.int32)
    level_b = level.astype(F32).astype(BF16)
    gn = gn_ref[...]

    chains = [(bi, hd) for bi in range(nbatch) for hd in range(heads)]
    H = range(len(chains))
    q = [gds[bi][:, hd * Dh:(hd + 1) * Dh] for bi, hd in chains]
    k = [gds[bi][:, W + hd * Dh:W + (hd + 1) * Dh] for bi, hd in chains]
    v = [gds[bi][:, 2 * W + hd * Dh:2 * W + (hd + 1) * Dh] for bi, hd in chains]
    Gc = [Gs[bi][:, hd:hd + 1] for bi, hd in chains]
    Gr = [GTs[bi][hd:hd + 1, :] for bi, hd in chains]
    bc = [betas[bi][:, heads + hd:heads + hd + 1] for bi, hd in chains]
    gl = [Gs[bi][C - 1:C, hd:hd + 1] for bi, hd in chains]

    kb = [t.astype(BF16) for t in k]
    decay = [jnp.exp(jnp.where(incl, Gc[h] - Gr[h], -jnp.inf)) for h in H]
    A = [jnp.where(strict, bc[h] * _dot_nt(kb[h], kb[h]) * decay[h], 0.0) for h in H]
    qk = [(_dot_nt(q[h].astype(BF16), kb[h]) * decay[h]).astype(BF16) for h in H]
    gam = [jnp.exp(t) for t in Gc]
    X = [jnp.concatenate([bc[h] * v[h], (bc[h] * gam[h]) * k[h]], axis=1) for h in H]
    Ab = [t.astype(BF16) for t in A]
    zero = jnp.zeros((C, C), BF16)
    Tm = [jnp.where(level_b == -1.0, jnp.ones((C, C), BF16), jnp.where(level_b == 0.0, -t, zero))
          for t in Ab]
    for lv in range(1, C.bit_length() - 2):
        at_lv = level_b == float(lv)
        E = [jnp.where(at_lv, t, zero) for t in Ab]
        F = [_dot(E[h], Tm[h]).astype(BF16) for h in H]
        Tm = [Tm[h] - _dot(Tm[h], F[h]).astype(BF16) for h in H]
    hc = C // 2
    F21 = [_dot(Ab[h][hc:, :hc], Tm[h][:hc, :hc]).astype(BF16) for h in H]
    T21 = [-_dot(Tm[h][hc:, hc:], F21[h]).astype(BF16) for h in H]
    Tm = [jnp.concatenate([Tm[h][:hc, :], jnp.concatenate([T21[h], Tm[h][hc:, hc:]], axis=1)], axis=0)
          for h in H]
    X = [X[h] + _dot(jnp.where(level_b == -1.0, zero, Tm[h]), X[h].astype(BF16)) for h in H]

    S0 = [state_ref[h] for h in H]
    Sb = [t.astype(BF16) for t in S0]
    ub = [(X[h][:, :Dh] - _dot(X[h][:, Dh:].astype(BF16), Sb[h])).astype(BF16) for h in H]
    o = [_dot((q[h] * gam[h]).astype(BF16), Sb[h]) + _dot(qk[h], ub[h]) for h in H]
    k_dec = [(k[h] * jnp.exp(gl[h] - Gc[h])).T.astype(BF16) for h in H]
    for h in H:
        state_ref[h] = jnp.exp(gl[h]) * S0[h] + _dot(k_dec[h], ub[h])
    for h, (bi, hd) in enumerate(chains):
        z = gds[bi][:, 3 * W + hd * Dh:3 * W + (hd + 1) * Dh]
        o_ref[bi, :, hd * Dh:(hd + 1) * Dh] = (_rms(o[h], gn) * _silu(z)).astype(BF16)


def _gdn(gd, ab, prm, gdn_norm, B, S, heads):
    C = GDN_CHUNK
    W = heads * GDN_HEAD_DIM
    nbatch = GDN_BATCH_ROWS if B % GDN_BATCH_ROWS == 0 else 1
    return pl.pallas_call(
        functools.partial(_gdn_kernel, heads=heads),
        grid=(B // nbatch, S // C),
        in_specs=[
            pl.BlockSpec((nbatch, C, 4 * W), lambda b, c: (b, c, 0)),
            pl.BlockSpec((nbatch, C, LANES), lambda b, c: (b, c, 0)),
            pl.BlockSpec((8, LANES), lambda b, c: (0, 0)),
            pl.BlockSpec((1, GDN_HEAD_DIM), lambda b, c: (0, 0)),
        ],
        out_specs=pl.BlockSpec((nbatch, C, W), lambda b, c: (b, c, 0)),
        out_shape=jax.ShapeDtypeStruct((B, S, W), BF16),
        scratch_shapes=[pltpu.VMEM((nbatch * heads, GDN_HEAD_DIM, GDN_HEAD_DIM), F32)],
        compiler_params=pltpu.CompilerParams(
            dimension_semantics=("arbitrary", "arbitrary"), vmem_limit_bytes=VMEM_LIMIT),
        name="gdn",
    )(gd, ab, prm, gdn_norm)


R_E1, R_E2, R_C1, R_C2, R_RANK1, R_RANK2 = range(6)


def _outproj_kernel(ya_ref, yb_ref, x_ref, wo_ref, fg_ref, wr_ref,
                    x1_ref, route_ref, route_t_ref, cnt_ref, carry_ref, wo16_ref):
    @pl.when(pl.program_id(0) == 0)
    def _():
        carry_ref[...] = jnp.zeros_like(carry_ref)
        wo16_ref[...] = wo_ref[...].astype(BF16)

    wa = ya_ref.shape[1]
    ts = ROUTER_SUBTILE
    subs = [pl.ds(j * ts, ts) for j in range(x_ref.shape[0] // ts)]
    each = lambda f, *lists: [f(*args) for args in zip(*lists)]
    rmax = lambda t: jnp.max(t, axis=-1, keepdims=True)
    rmin = lambda t: jnp.min(t, axis=-1, keepdims=True)
    rsum = lambda t: jnp.sum(t, axis=-1, keepdims=True)

    x1 = [x_ref[sl, :] + _dot(ya_ref[sl, :], wo16_ref[:wa, :]) + _dot(yb_ref[sl, :], wo16_ref[wa:, :])
          for sl in subs]
    for sl, t in zip(subs, x1):
        x1_ref[sl, :] = t
    h = each(lambda t: _rms(t, fg_ref[...]).astype(BF16), x1)
    logits = each(lambda t: _dot(t, wr_ref[...]), h)
    lane = lax.broadcasted_iota(jnp.int32, (ts, LANES), 1).astype(F32)
    big = float(LANES)

    is_g = (lane >= N_EXPERTS) & (lane < N_EXPERTS + N_GROUPS)
    lg = each(lambda t: jnp.where(is_g, t, -jnp.inf), logits)
    mg = each(rmax, lg)
    g_sel = each(lambda a, b: rmin(jnp.where(a == b, lane, big)) - N_EXPERTS, lg, mg)
    p_group = each(lambda a, b: 1.0 / rsum(jnp.exp(a - b)), lg, mg)

    le = each(lambda t, g: jnp.where((lane >= g * EXPERTS_PER_GROUP) & (lane < (g + 1) * EXPERTS_PER_GROUP),
                                     t, -jnp.inf), logits, g_sel)
    m1 = each(rmax, le)
    i1 = each(lambda a, b: rmin(jnp.where(a == b, lane, big)), le, m1)
    le2 = each(lambda a, i: jnp.where(lane == i, -jnp.inf, a), le, i1)
    m2 = each(rmax, le2)
    i2 = each(lambda a, b: rmin(jnp.where(a == b, lane, big)), le2, m2)
    se = each(lambda a, b: rsum(jnp.exp(a - b)), le, m1)
    p1 = each(lambda s: 1.0 / s, se)
    p2 = each(lambda a, b, s: jnp.exp(a - b) / s, m2, m1, se)
    c1 = each(lambda pg, a, b: pg * (a / (a + b)), p_group, p1, p2)
    c2 = each(lambda pg, a, b: pg * (b / (a + b)), p_group, p1, p2)

    hot = each(lambda a, b: ((lane == a) | (lane == b)).astype(BF16), i1, i2)
    r_i = lax.broadcasted_iota(jnp.int32, (ts, ts), 0)
    c_i = lax.broadcasted_iota(jnp.int32, (ts, ts), 1)
    before = (c_i < r_i).astype(BF16)
    within = each(lambda t: _dot(before, t), hot)
    totals = each(lambda t: jnp.sum(t.astype(F32), axis=0, keepdims=True), hot)
    base = [carry_ref[0:1, :]]
    for t in totals:
        base.append(base[-1] + t)
    seen = each(lambda a, b: a + b, within, base[:-1])
    rank1 = each(lambda i, s: rsum(jnp.where(lane == i, s, 0.0)), i1, seen)
    rank2 = each(lambda i, s: rsum(jnp.where(lane == i, s, 0.0)), i2, seen)
    carry_ref[...] = jnp.broadcast_to(base[-1], carry_ref.shape)
    cnt_ref[...] = jnp.broadcast_to(base[-1], cnt_ref.shape)

    for j, sl in enumerate(subs):
        rec = jnp.zeros((ts, LANES), F32)
        for slot, val in ((R_E1, i1), (R_E2, i2), (R_C1, c1), (R_C2, c2), (R_RANK1, rank1), (R_RANK2, rank2)):
            rec = jnp.where(lane == float(slot), val[j], rec)
        route_ref[sl, :] = rec
        route_t_ref[:, sl] = rec.T[0:8, :]


def _outproj(ya, yb, x2d, wo, fgain, wr, tm):
    T, D = x2d.shape
    Wa, Wb = ya.shape[1], yb.shape[1]
    assert tm % ROUTER_SUBTILE == 0
    return pl.pallas_call(
        _outproj_kernel,
        grid=(T // tm,),
        in_specs=[
            pl.BlockSpec((tm, Wa), lambda i: (i, 0)),
            pl.BlockSpec((tm, Wb), lambda i: (i, 0)),
            pl.BlockSpec((tm, D), lambda i: (i, 0)),
            pl.BlockSpec((Wa + Wb, D), lambda i: (0, 0)),
            pl.BlockSpec((1, D), lambda i: (0, 0)),
            pl.BlockSpec((D, LANES), lambda i: (0, 0)),
        ],
        out_specs=[
            pl.BlockSpec((tm, D), lambda i: (i, 0)),
            pl.BlockSpec((tm, LANES), lambda i: (i, 0)),
            pl.BlockSpec((8, tm), lambda i: (0, i)),
            pl.BlockSpec((8, LANES), lambda i: (0, 0)),
        ],
        out_shape=[
            jax.ShapeDtypeStruct((T, D), F32),
            jax.ShapeDtypeStruct((T, LANES), F32),
            jax.ShapeDtypeStruct((8, T), F32),
            jax.ShapeDtypeStruct((8, LANES), F32),
        ],
        scratch_shapes=[pltpu.VMEM((8, LANES), F32), pltpu.VMEM((Wa + Wb, D), BF16)],
        compiler_params=pltpu.CompilerParams(
            dimension_semantics=("arbitrary",), vmem_limit_bytes=VMEM_LIMIT),
        name="outproj_router",
    )(ya, yb, x2d, wo, fgain, wr)


def _dispatch_kernel(tail_ref, pos_ref, x_ref, xs_ref, zero_ref, sem):
    tm = x_ref.shape[0]
    rt = zero_ref.shape[0]

    @pl.when(pl.program_id(0) == 0)
    def _():
        zero_ref[...] = jnp.zeros_like(zero_ref)

        def tile_fill(j):
            return pltpu.make_async_copy(zero_ref, xs_ref.at[pl.ds(pl.multiple_of(j * rt, 8), rt)], sem)

        def for_each_fill(act):
            for e in range(N_EXPERTS):
                pl.when(tail_ref[e] >= 0)(lambda e=e: act(tile_fill(tail_ref[e])))
            lax.fori_loop(tail_ref[N_EXPERTS], xs_ref.shape[0] // rt, lambda j, c: (act(tile_fill(j)), c)[1], 0)

        for_each_fill(lambda f: f.start())
        for_each_fill(lambda f: f.wait())

    def issue(r, carry):
        for slot in range(2):
            p = pos_ref[0, slot * tm + r]
            pltpu.make_async_copy(x_ref.at[pl.ds(r, 1)], xs_ref.at[pl.ds(p, 1)], sem).start()
        return carry

    lax.fori_loop(0, tm, issue, 0, unroll=8)
    for _ in range(2):
        pltpu.make_async_copy(x_ref, xs_ref.at[pl.ds(0, tm)], sem).wait()


def _dispatch(tails, pos, h, n_rows, tm, rt):
    T, Dp = h.shape
    return pl.pallas_call(
        _dispatch_kernel,
        grid_spec=pltpu.PrefetchScalarGridSpec(
            num_scalar_prefetch=1,
            grid=(T // tm,),
            in_specs=[
                pl.BlockSpec((None, 1, 2 * tm), lambda i, tl: (i, 0, 0), memory_space=pltpu.SMEM),
                pl.BlockSpec((tm, Dp), lambda i, tl: (i, 0)),
            ],
            out_specs=pl.BlockSpec(memory_space=pl.ANY),
            scratch_shapes=[pltpu.VMEM((rt, Dp), h.dtype), pltpu.SemaphoreType.DMA],
        ),
        out_shape=jax.ShapeDtypeStruct((n_rows, Dp), h.dtype),
        compiler_params=pltpu.CompilerParams(
            dimension_semantics=("arbitrary",), vmem_limit_bytes=VMEM_LIMIT),
        name="moe_dispatch",
    )(tails, pos, h)


def _experts_kernel(te_ref, tv_ref, xs_ref, fg_ref, wg_ref, wu_ref, wd_ref, y_ref):
    del te_ref
    i = pl.program_id(0)

    @pl.when(tv_ref[i] > 0)
    def _():
        h = _rms(xs_ref[...], fg_ref[...]).astype(BF16)
        a = _dot(h, wg_ref[...].astype(BF16))
        b = _dot(h, wu_ref[...].astype(BF16))
        y_ref[...] = _dot((_silu(a) * b).astype(BF16), wd_ref[...].astype(BF16))

    @pl.when(tv_ref[i] == 0)
    def _():
        y_ref[...] = jnp.zeros_like(y_ref)


def _experts(tile_expert, tile_valid, xs, fgain, wg, wu, wd, rt):
    n_rows = tile_expert.shape[0] * rt
    Dp = xs.shape[1]
    E, D, F = wg.shape
    return pl.pallas_call(
        _experts_kernel,
        grid_spec=pltpu.PrefetchScalarGridSpec(
            num_scalar_prefetch=2,
            grid=(n_rows // rt,),
            in_specs=[
                pl.BlockSpec((rt, Dp), lambda i, te, tv: (i * tv[i], 0)),
                pl.BlockSpec((1, D), lambda i, te, tv: (0, 0)),
                pl.BlockSpec((None, D, F), lambda i, te, tv: (te[i], 0, 0)),
                pl.BlockSpec((None, D, F), lambda i, te, tv: (te[i], 0, 0)),
                pl.BlockSpec((None, F, D), lambda i, te, tv: (te[i], 0, 0)),
            ],
            out_specs=pl.BlockSpec((rt, D), lambda i, te, tv: (i, 0)),
        ),
        out_shape=jax.ShapeDtypeStruct((n_rows, D), F32),
        compiler_params=pltpu.CompilerParams(
            dimension_semantics=("arbitrary",), vmem_limit_bytes=VMEM_LIMIT),
        name="moe_experts",
    )(tile_expert, tile_valid, xs, fgain, wg, wu, wd)


def _combine_kernel(pos_ref, pos_next_ref, x1_ref, route_ref, fn_ref, y_ref, o_ref, buf_ref, sem):
    tm = x1_ref.shape[0]
    i = pl.program_id(0)
    ring = i % 2

    def row_copy(p_ref, r, slot, rg):
        p = p_ref[0, slot * tm + r]
        return pltpu.make_async_copy(y_ref.at[pl.ds(p, 1)], buf_ref.at[rg, slot, pl.ds(r, 1)], sem.at[rg])

    def issue(p_ref, rg):
        def body(r, carry):
            for slot in range(2):
                row_copy(p_ref, r, slot, rg).start()
            return carry
        lax.fori_loop(0, tm, body, 0, unroll=8)

    @pl.when(i == 0)
    def _():
        issue(pos_ref, ring)

    @pl.when(i + 1 < pl.num_programs(0))
    def _():
        issue(pos_next_ref, 1 - ring)

    for slot in range(2):
        pltpu.make_async_copy(y_ref.at[pl.ds(0, tm)], buf_ref.at[ring, slot], sem.at[ring]).wait()
    route = route_ref[...]
    c1 = route[:, R_C1:R_C1 + 1]
    c2 = route[:, R_C2:R_C2 + 1]
    o_ref[...] = _rms(x1_ref[...] + c1 * buf_ref[ring, 0] + c2 * buf_ref[ring, 1], fn_ref[...])


def _combine(pos, x1, route, fnorm, y, tm):
    T, D = x1.shape
    n = T // tm
    return pl.pallas_call(
        _combine_kernel,
        grid=(n,),
        in_specs=[
            pl.BlockSpec((None, 1, 2 * tm), lambda i: (i, 0, 0), memory_space=pltpu.SMEM),
            pl.BlockSpec((None, 1, 2 * tm), lambda i: (jnp.minimum(i + 1, n - 1), 0, 0),
                         memory_space=pltpu.SMEM),
            pl.BlockSpec((tm, D), lambda i: (i, 0)),
            pl.BlockSpec((tm, LANES), lambda i: (i, 0)),
            pl.BlockSpec((1, D), lambda i: (0, 0)),
            pl.BlockSpec(memory_space=pl.ANY),
        ],
        out_specs=pl.BlockSpec((tm, D), lambda i: (i, 0)),
        out_shape=jax.ShapeDtypeStruct((T, D), F32),
        scratch_shapes=[pltpu.VMEM((2, 2, tm, D), F32), pltpu.SemaphoreType.DMA((2,))],
        compiler_params=pltpu.CompilerParams(
            dimension_semantics=("arbitrary",), vmem_limit_bytes=VMEM_LIMIT),
        name="moe_combine",
    )(pos, pos, x1, route, fnorm, y)


def _pick_tile(T, pref):
    tm = min(pref, T)
    while T % tm:
        tm //= 2
    return tm


def kernel(x, attn_norm, w_in, conv_w, A_log, dt_bias, gdn_norm, w_out, ffn_norm,
           w_group, w_router, w_gate, w_up, w_down, final_norm):
    B, S, D = x.shape
    T = B * S
    depth = w_in.shape[0]
    heads = A_log.shape[1]
    Wb = heads * GDN_HEAD_DIM
    Wa = w_out.shape[1] - Wb
    assert S % MOBA_BLOCK == 0 and S // MOBA_BLOCK <= GATE_ROWS and S % GDN_CHUNK == 0
    assert Wa % LANES == 0 and 2 * heads <= LANES
    assert depth == 1, "the final RMSNorm is fused into the layer's last kernel"

    x2d = x.reshape(T, D)
    for l in range(depth):
        w_all = jnp.pad(w_in[l], ((0, 0), (0, LANES - 2 * heads))).astype(BF16)
        qkv_a, gd, ab = _inproj(x2d, attn_norm[l][None, :], w_all, conv_w[l].astype(F32),
                                3 * Wa, 4 * Wb, _pick_tile(S, 512), S)

        y_a = _moba(qkv_a.reshape(B, S, 3 * Wa), B, S)

        prm = jnp.zeros((8, LANES), F32)
        prm = prm.at[0, :heads].set(A_log[l].astype(F32)).at[1, :heads].set(dt_bias[l].astype(F32))
        y_b = _gdn(gd.reshape(B, S, 4 * Wb), ab.reshape(B, S, LANES), prm,
                   gdn_norm[l][None, :].astype(F32), B, S, heads)

        wr = jnp.concatenate([w_router[l], w_group[l]], axis=1)
        wr = jnp.pad(wr, ((0, 0), (0, LANES - wr.shape[1]))).astype(BF16)
        x1, route, route_t, counts = _outproj(y_a.reshape(T, Wa), y_b.reshape(T, Wb), x2d,
                                              w_out[l].astype(F32), ffn_norm[l][None, :], wr, _pick_tile(T, 512))

        rt = MOE_ROW_TILE
        n_tiles = -(-2 * T // rt) + N_EXPERTS
        cnt = counts[0, :N_EXPERTS].astype(jnp.int32)
        seg = -(-cnt // rt) * rt
        seg_end = jnp.cumsum(seg)
        seg_start = seg_end - seg
        tile_row = jnp.arange(n_tiles, dtype=jnp.int32) * rt
        tile_expert = jnp.minimum(jnp.sum(tile_row[:, None] >= seg_end[None, :], axis=1), N_EXPERTS - 1)
        tile_valid = (tile_row < seg_end[-1]).astype(jnp.int32)
        e_ids = jnp.arange(N_EXPERTS, dtype=jnp.int32)[:, None]

        def dest_rows(e_lane, rank_lane):
            e = route_t[e_lane].astype(jnp.int32)
            return route_t[rank_lane].astype(jnp.int32) + jnp.sum(
                jnp.where(e[None, :] == e_ids, seg_start[:, None], 0), axis=0)

        pos1, pos2 = dest_rows(R_E1, R_RANK1), dest_rows(R_E2, R_RANK2)

        def tiled_pos(tm):
            return jnp.concatenate([pos1.reshape(T // tm, 1, tm), pos2.reshape(T // tm, 1, tm)], axis=2)

        F = w_gate.shape[-1]
        wge = w_gate[l].reshape(N_EXPERTS, D, F)
        wue = w_up[l].reshape(N_EXPERTS, D, F)
        wde = w_down[l].reshape(N_EXPERTS, F, D)
        tmd = _pick_tile(T, 512)
        tails = jnp.concatenate([jnp.where(seg > 0, seg_end // rt - 1, -1), seg_end[-1:] // rt])
        xs = _dispatch(tails, tiled_pos(tmd), x1, n_tiles * rt, tmd, rt)
        ys = _experts(tile_expert.astype(jnp.int32), tile_valid, xs, ffn_norm[l][None, :],
                      wge, wue, wde, rt)
        tmc = _pick_tile(T, 512)
        x2d = _combine(tiled_pos(tmc), x1, route, final_norm[None, :], ys, tmc)
    return x2d.reshape(B, S, D)
```

```python
import functools

import jax
import jax.numpy as jnp
from jax import lax
from jax.experimental import pallas as pl
from jax.experimental.pallas import tpu as pltpu

F32 = jnp.float32
BF16 = jnp.bfloat16

LANES = 128
MOBA_HEAD_DIM = 64
MOBA_BLOCK = 256
MOBA_TOPK = 3
MOBA_BATCH_ROWS = 2
MOBA_Q_SCALE = MOBA_HEAD_DIM ** -0.5 * 1.4426950408889634
GDN_HEAD_DIM = 128
GDN_CONV = 4
GDN_CHUNK = 256
GDN_BATCH_ROWS = 2
N_GROUPS = 4
EXPERTS_PER_GROUP = 8
N_EXPERTS = N_GROUPS * EXPERTS_PER_GROUP
RMS_EPS = 1e-6
NEG = -1e30
GATE_ROWS = 16
MOE_ROW_TILE = 512
ROUTER_SUBTILE = 128
VMEM_LIMIT = 48 * 1024 * 1024


def _dot(a, b):
    return jnp.dot(a, b, preferred_element_type=F32)


def _dot_nt(a, b):
    return lax.dot_general(a, b, (((1,), (1,)), ((), ())), preferred_element_type=F32)


def _sigmoid(x):
    return 1.0 / (1.0 + jnp.exp(-x))


def _silu(x):
    return x * _sigmoid(x)


def _rms(x, gain):
    return x * lax.rsqrt(jnp.mean(x * x, axis=-1, keepdims=True) + RMS_EPS) * gain


def _inproj_kernel(x_ref, gain_ref, w_ref, cw_ref, om_ref, og_ref, oab_ref, prev_ref, *, tiles_per_seq):
    nm, ng = om_ref.shape[1], og_ref.shape[1]
    tm = x_ref.shape[0]

    @pl.when(pl.program_id(0) % tiles_per_seq == 0)
    def _():
        prev_ref[...] = jnp.zeros_like(prev_ref)

    W = ng // 4
    Dh = GDN_HEAD_DIM
    cw = cw_ref[...]
    half = tm // 2
    h_top = _rms(x_ref[:half, :], gain_ref[...]).astype(BF16)
    h_bot = _rms(x_ref[half:, :], gain_ref[...]).astype(BF16)

    def gdn_tail(rows, og, halo):
        raw = og[:, :3 * W]
        ext = jnp.concatenate([halo, raw], axis=0)
        conv = cw[GDN_CONV - 1:GDN_CONV] * raw
        for d in range(1, GDN_CONV):
            conv = conv + cw[GDN_CONV - 1 - d:GDN_CONV - d] * pltpu.roll(ext, d, 0)[8:]
        qkv = _silu(conv)
        for j in range(2 * W // Dh):
            t = qkv[:, j * Dh:(j + 1) * Dh]
            t = t * lax.rsqrt(jnp.sum(t * t, axis=-1, keepdims=True) + 1e-6)
            og_ref[rows, j * Dh:(j + 1) * Dh] = t * (Dh ** -0.5) if j < W // Dh else t
        og_ref[rows, 2 * W:3 * W] = qkv[:, 2 * W:]
        og_ref[rows, 3 * W:] = og[:, 3 * W:]
        return raw[half - 8:, :]

    og_top = _dot(h_top, w_ref[:, nm:nm + ng])
    og_bot = _dot(h_bot, w_ref[:, nm:nm + ng])
    last = gdn_tail(slice(0, half), og_top, prev_ref[...])
    h = jnp.concatenate([h_top, h_bot], axis=0)
    om = _dot(h, w_ref[:, :nm])
    nq = nm // 3
    om_ref[:, :nq] = (om[:, :nq] * MOBA_Q_SCALE).astype(BF16)
    om_ref[:, nq:] = om[:, nq:].astype(BF16)
    prev_ref[...] = gdn_tail(slice(half, tm), og_bot, last)
    oab_ref[...] = _dot(h, w_ref[:, nm + ng:])


def _inproj(x2d, gain, w_all, conv_w, nm, ng, tm, seq_len):
    T, D = x2d.shape
    assert w_all.shape[1] == nm + ng + LANES and nm % LANES == 0 and ng % LANES == 0
    assert seq_len % tm == 0 and tm >= 16
    return pl.pallas_call(
        functools.partial(_inproj_kernel, tiles_per_seq=seq_len // tm),
        grid=(T // tm,),
        in_specs=[
            pl.BlockSpec((tm, D), lambda i: (i, 0)),
            pl.BlockSpec((1, D), lambda i: (0, 0)),
            pl.BlockSpec((D, nm + ng + LANES), lambda i: (0, 0)),
            pl.BlockSpec((GDN_CONV, 3 * ng // 4), lambda i: (0, 0)),
        ],
        out_specs=[
            pl.BlockSpec((tm, nm), lambda i: (i, 0)),
            pl.BlockSpec((tm, ng), lambda i: (i, 0)),
            pl.BlockSpec((tm, LANES), lambda i: (i, 0)),
        ],
        out_shape=[
            jax.ShapeDtypeStruct((T, nm), BF16),
            jax.ShapeDtypeStruct((T, ng), F32),
            jax.ShapeDtypeStruct((T, LANES), F32),
        ],
        scratch_shapes=[pltpu.VMEM((8, 3 * ng // 4), F32)],
        compiler_params=pltpu.CompilerParams(
            dimension_semantics=("arbitrary",), vmem_limit_bytes=VMEM_LIMIT),
        name="inproj",
    )(x2d, gain, w_all, conv_w)


def _moba_kernel(q_ref, k_ref, v_ref, o_ref, *, S):
    nb = S // MOBA_BLOCK
    L = MOBA_BLOCK
    lane = lax.broadcasted_iota(jnp.int32, (1, LANES), 1)
    key_blk = lax.broadcasted_iota(jnp.int32, (S, LANES), 0) // L
    lane_s = lax.broadcasted_iota(jnp.int32, (S, LANES), 1)
    qblk = lax.broadcasted_iota(jnp.int32, (GATE_ROWS, S), 1) // L
    cidx = lax.broadcasted_iota(jnp.int32, (GATE_ROWS, S), 0)
    r_i = lax.broadcasted_iota(jnp.int32, (L, L), 0)
    c_i = lax.broadcasted_iota(jnp.int32, (L, L), 1)
    causal = c_i <= r_i
    hms = [(lane >= MOBA_HEAD_DIM * h) & (lane < MOBA_HEAD_DIM * (h + 1)) for h in range(2)]

    chains = [(bi, h) for bi in range(q_ref.shape[0]) for h in range(2)]
    qa, ka, va = [], [], []
    for bi, h in chains:
        q, k, v = q_ref[bi], k_ref[bi], v_ref[bi]
        hm = hms[h]
        off = MOBA_HEAD_DIM * (1 - h)
        km = jnp.mean(k.astype(F32).reshape(nb, L, LANES), axis=1)
        km = jnp.concatenate([km, jnp.zeros((GATE_ROWS - nb, LANES), F32)], axis=0).astype(BF16)
        qh = jnp.where(hm, q, jnp.zeros_like(q))
        gate = _dot_nt(km, qh)
        valid = cidx < qblk
        gate = jnp.where(valid, gate, -jnp.inf)
        rank = jnp.zeros((GATE_ROWS, S), F32)
        for i in range(nb):
            gi = gate[i:i + 1, :]
            beats = (gi > gate) | ((gi == gate) & (i < cidx))
            rank = rank + beats.astype(F32)
        sel = valid & (rank < float(MOBA_TOPK))
        pen = jnp.where(sel | (cidx >= qblk), 0.0, NEG)
        pads = [jnp.zeros((r, S), F32) for r in (off, LANES - off - GATE_ROWS)]
        pen = jnp.concatenate(([pads[0]] if off else []) + [pen, pads[1]], axis=0)
        pen_q = pen.T.astype(BF16)
        qa.append(jnp.where(hm, q, pen_q))
        onehot = ((lane_s - off) == key_blk).astype(BF16)
        ka.append(jnp.where(hm, k, onehot))
        va.append(jnp.where(hm, v, jnp.ones_like(v)))

    HH = range(len(chains))
    for n in range(nb):
        s = [_dot_nt(qa[h][n * L:(n + 1) * L], ka[h][:(n + 1) * L]) for h in HH]
        s_own = [jnp.where(causal, s[h][:, n * L:], NEG) for h in HH]
        m = [jnp.max(s_own[h], axis=-1, keepdims=True) for h in HH]
        if n > 0:
            m = [jnp.maximum(m[h], jnp.max(s[h][:, :n * L], axis=-1, keepdims=True)) for h in HH]
            p = [jnp.concatenate([jnp.exp2(s[h][:, :n * L] - m[h]), jnp.exp2(s_own[h] - m[h])],
                                 axis=1).astype(BF16) for h in HH]
        else:
            p = [jnp.exp2(s_own[h] - m[h]).astype(BF16) for h in HH]
        acc = [_dot(p[h], va[h][:(n + 1) * L]) for h in HH]
        outs = [acc[h] / pltpu.roll(acc[h], MOBA_HEAD_DIM, 1) for h in HH]
        for bi in range(q_ref.shape[0]):
            o_ref[bi, n * L:(n + 1) * L, :] = jnp.where(hms[0], outs[2 * bi], outs[2 * bi + 1]).astype(BF16)


def _moba(qkv, B, S):
    W = qkv.shape[-1] // 3
    npair = W // LANES
    nbr = MOBA_BATCH_ROWS if B % MOBA_BATCH_ROWS == 0 else 1
    return pl.pallas_call(
        functools.partial(_moba_kernel, S=S),
        grid=(B // nbr, npair),
        in_specs=[
            pl.BlockSpec((nbr, S, LANES), lambda b, j: (b, 0, j)),
            pl.BlockSpec((nbr, S, LANES), lambda b, j: (b, 0, npair + j)),
            pl.BlockSpec((nbr, S, LANES), lambda b, j: (b, 0, 2 * npair + j)),
        ],
        out_specs=pl.BlockSpec((nbr, S, LANES), lambda b, j: (b, 0, j)),
        out_shape=jax.ShapeDtypeStruct((B, S, W), BF16),
        compiler_params=pltpu.CompilerParams(
            dimension_semantics=("arbitrary", "arbitrary"), vmem_limit_bytes=VMEM_LIMIT),
        name="moba",
    )(qkv, qkv, qkv)


def _gdn_kernel(gd_ref, ab_ref, prm_ref, gn_ref, o_ref, state_ref, *, heads):
    C = GDN_CHUNK
    Dh = GDN_HEAD_DIM
    W = heads * Dh
    c = pl.program_id(1)

    @pl.when(c == 0)
    def _():
        state_ref[...] = jnp.zeros_like(state_ref)

    nbatch = gd_ref.shape[0]
    prm = prm_ref[...]
    rows = lax.broadcasted_iota(jnp.int32, (C, LANES), 0)
    gds, Gs, GTs, betas = [], [], [], []
    for bi in range(nbatch):
        gds.append(gd_ref[bi])
        ab = ab_ref[bi]
        sp_in = ab + prm[1:2]
        softplus = jnp.maximum(sp_in, 0.0) + jnp.log1p(jnp.exp(-jnp.abs(sp_in)))
        G = -jnp.exp(prm[0:1]) * softplus
        betas.append(_sigmoid(ab))
        d = 1
        while d < C:
            G = G + jnp.where(rows >= d, pltpu.roll(G, d, 0), 0.0)
            d *= 2
        Gs.append(G)
        GTs.append(G.T)

    r_i = lax.broadcasted_iota(jnp.int32, (C, C), 0)
    c_i = lax.broadcasted_iota(jnp.int32, (C, C), 1)
    incl = c_i <= r_i
    strict = c_i < r_i
    rc = r_i ^ c_i
    level = jnp.full((C, C), -1, jnp.int32)
    for b in range(C.bit_length() - 1):
        level = level + (rc >= (1 << b)).astype(jnp.int32)
    level_b = level.astype(F32).astype(BF16)
    gn = gn_ref[...]

    chains = [(bi, hd) for bi in range(nbatch) for hd in range(heads)]
    H = range(len(chains))
    q = [gds[bi][:, hd * Dh:(hd + 1) * Dh] for bi, hd in chains]
    k = [gds[bi][:, W + hd * Dh:W + (hd + 1) * Dh] for bi, hd in chains]
    v = [gds[bi][:, 2 * W + hd * Dh:2 * W + (hd + 1) * Dh] for bi, hd in chains]
    Gc = [Gs[bi][:, hd:hd + 1] for bi, hd in chains]
    Gr = [GTs[bi][hd:hd + 1, :] for bi, hd in chains]
    bc = [betas[bi][:, heads + hd:heads + hd + 1] for bi, hd in chains]
    gl = [Gs[bi][C - 1:C, hd:hd + 1] for bi, hd in chains]

    kb = [t.astype(BF16) for t in k]
    decay = [jnp.exp(jnp.where(incl, Gc[h] - Gr[h], -jnp.inf)) for h in H]
    A = [jnp.where(strict, bc[h] * _dot_nt(kb[h], kb[h]) * decay[h], 0.0) for h in H]
    qk = [(_dot_nt(q[h].astype(BF16), kb[h]) * decay[h]).astype(BF16) for h in H]
    gam = [jnp.exp(t) for t in Gc]
    X = [jnp.concatenate([bc[h] * v[h], (bc[h] * gam[h]) * k[h]], axis=1) for h in H]
    Ab = [t.astype(BF16) for t in A]
    zero = jnp.zeros((C, C), BF16)
    Tm = [jnp.where(level_b == -1.0, jnp.ones((C, C), BF16), jnp.where(level_b == 0.0, -t, zero))
          for t in Ab]
    for lv in range(1, C.bit_length() - 2):
        at_lv = level_b == float(lv)
        E = [jnp.where(at_lv, t, zero) for t in Ab]
        F = [_dot(E[h], Tm[h]).astype(BF16) for h in H]
        Tm = [Tm[h] - _dot(Tm[h], F[h]).astype(BF16) for h in H]
    hc = C // 2
    F21 = [_dot(Ab[h][hc:, :hc], Tm[h][:hc, :hc]).astype(BF16) for h in H]
    T21 = [-_dot(Tm[h][hc:, hc:], F21[h]).astype(BF16) for h in H]
    Tm = [jnp.concatenate([Tm[h][:hc, :], jnp.concatenate([T21[h], Tm[h][hc:, hc:]], axis=1)], axis=0)
          for h in H]
    X = [X[h] + _dot(jnp.where(level_b == -1.0, zero, Tm[h]), X[h].astype(BF16)) for h in H]

    S0 = [state_ref[h] for h in H]
    Sb = [t.astype(BF16) for t in S0]
    ub = [(X[h][:, :Dh] - _dot(X[h][:, Dh:].astype(BF16), Sb[h])).astype(BF16) for h in H]
    o = [_dot((q[h] * gam[h]).astype(BF16), Sb[h]) + _dot(qk[h], ub[h]) for h in H]
    k_dec = [(k[h] * jnp.exp(gl[h] - Gc[h])).T.astype(BF16) for h in H]
    for h in H:
        state_ref[h] = jnp.exp(gl[h]) * S0[h] + _dot(k_dec[h], ub[h])
    for h, (bi, hd) in enumerate(chains):
        z = gds[bi][:, 3 * W + hd * Dh:3 * W + (hd + 1) * Dh]
        o_ref[bi, :, hd * Dh:(hd + 1) * Dh] = (_rms(o[h], gn) * _silu(z)).astype(BF16)


def _gdn(gd, ab, prm, gdn_norm, B, S, heads):
    C = GDN_CHUNK
    W = heads * GDN_HEAD_DIM
    nbatch = GDN_BATCH_ROWS if B % GDN_BATCH_ROWS == 0 else 1
    return pl.pallas_call(
        functools.partial(_gdn_kernel, heads=heads),
        grid=(B // nbatch, S // C),
        in_specs=[
            pl.BlockSpec((nbatch, C, 4 * W), lambda b, c: (b, c, 0)),
            pl.BlockSpec((nbatch, C, LANES), lambda b, c: (b, c, 0)),
            pl.BlockSpec((8, LANES), lambda b, c: (0, 0)),
            pl.BlockSpec((1, GDN_HEAD_DIM), lambda b, c: (0, 0)),
        ],
        out_specs=pl.BlockSpec((nbatch, C, W), lambda b, c: (b, c, 0)),
        out_shape=jax.ShapeDtypeStruct((B, S, W), BF16),
        scratch_shapes=[pltpu.VMEM((nbatch * heads, GDN_HEAD_DIM, GDN_HEAD_DIM), F32)],
        compiler_params=pltpu.CompilerParams(
            dimension_semantics=("arbitrary", "arbitrary"), vmem_limit_bytes=VMEM_LIMIT),
        name="gdn",
    )(gd, ab, prm, gdn_norm)


R_E1, R_E2, R_C1, R_C2, R_RANK1, R_RANK2 = range(6)


def _outproj_kernel(ya_ref, yb_ref, x_ref, wo_ref, fg_ref, wr_ref,
                    x1_ref, route_ref, route_t_ref, cnt_ref, carry_ref, wo16_ref):
    @pl.when(pl.program_id(0) == 0)
    def _():
        carry_ref[...] = jnp.zeros_like(carry_ref)
        wo16_ref[...] = wo_ref[...].astype(BF16)

    wa = ya_ref.shape[1]
    ts = ROUTER_SUBTILE
    subs = [pl.ds(j * ts, ts) for j in range(x_ref.shape[0] // ts)]
    each = lambda f, *lists: [f(*args) for args in zip(*lists)]
    rmax = lambda t: jnp.max(t, axis=-1, keepdims=True)
    rmin = lambda t: jnp.min(t, axis=-1, keepdims=True)
    rsum = lambda t: jnp.sum(t, axis=-1, keepdims=True)

    x1 = [x_ref[sl, :] + _dot(ya_ref[sl, :], wo16_ref[:wa, :]) + _dot(yb_ref[sl, :], wo16_ref[wa:, :])
          for sl in subs]
    for sl, t in zip(subs, x1):
        x1_ref[sl, :] = t
    h = each(lambda t: _rms(t, fg_ref[...]).astype(BF16), x1)
    logits = each(lambda t: _dot(t, wr_ref[...]), h)
    lane = lax.broadcasted_iota(jnp.int32, (ts, LANES), 1).astype(F32)
    big = float(LANES)

    is_g = (lane >= N_EXPERTS) & (lane < N_EXPERTS + N_GROUPS)
    lg = each(lambda t: jnp.where(is_g, t, -jnp.inf), logits)
    mg = each(rmax, lg)
    g_sel = each(lambda a, b: rmin(jnp.where(a == b, lane, big)) - N_EXPERTS, lg, mg)
    p_group = each(lambda a, b: 1.0 / rsum(jnp.exp(a - b)), lg, mg)

    le = each(lambda t, g: jnp.where((lane >= g * EXPERTS_PER_GROUP) & (lane < (g + 1) * EXPERTS_PER_GROUP),
                                     t, -jnp.inf), logits, g_sel)
    m1 = each(rmax, le)
    i1 = each(lambda a, b: rmin(jnp.where(a == b, lane, big)), le, m1)
    le2 = each(lambda a, i: jnp.where(lane == i, -jnp.inf, a), le, i1)
    m2 = each(rmax, le2)
    i2 = each(lambda a, b: rmin(jnp.where(a == b, lane, big)), le2, m2)
    se = each(lambda a, b: rsum(jnp.exp(a - b)), le, m1)
    p1 = each(lambda s: 1.0 / s, se)
    p2 = each(lambda a, b, s: jnp.exp(a - b) / s, m2, m1, se)
    c1 = each(lambda pg, a, b: pg * (a / (a + b)), p_group, p1, p2)
    c2 = each(lambda pg, a, b: pg * (b / (a + b)), p_group, p1, p2)

    hot = each(lambda a, b: ((lane == a) | (lane == b)).astype(BF16), i1, i2)
    r_i = lax.broadcasted_iota(jnp.int32, (ts, ts), 0)
    c_i = lax.broadcasted_iota(jnp.int32, (ts, ts), 1)
    before = (c_i < r_i).astype(BF16)
    within = each(lambda t: _dot(before, t), hot)
    totals = each(lambda t: jnp.sum(t.astype(F32), axis=0, keepdims=True), hot)
    base = [carry_ref[0:1, :]]
    for t in totals:
        base.append(base[-1] + t)
    seen = each(lambda a, b: a + b, within, base[:-1])
    rank1 = each(lambda i, s: rsum(jnp.where(lane == i, s, 0.0)), i1, seen)
    rank2 = each(lambda i, s: rsum(jnp.where(lane == i, s, 0.0)), i2, seen)
    carry_ref[...] = jnp.broadcast_to(base[-1], carry_ref.shape)
    cnt_ref[...] = jnp.broadcast_to(base[-1], cnt_ref.shape)

    for j, sl in enumerate(subs):
        rec = jnp.zeros((ts, LANES), F32)
        for slot, val in ((R_E1, i1), (R_E2, i2), (R_C1, c1), (R_C2, c2), (R_RANK1, rank1), (R_RANK2, rank2)):
            rec = jnp.where(lane == float(slot), val[j], rec)
        route_ref[sl, :] = rec
        route_t_ref[:, sl] = rec.T[0:8, :]


def _outproj(ya, yb, x2d, wo, fgain, wr, tm):
    T, D = x2d.shape
    Wa, Wb = ya.shape[1], yb.shape[1]
    assert tm % ROUTER_SUBTILE == 0
    return pl.pallas_call(
        _outproj_kernel,
        grid=(T // tm,),
        in_specs=[
            pl.BlockSpec((tm, Wa), lambda i: (i, 0)),
            pl.BlockSpec((tm, Wb), lambda i: (i, 0)),
            pl.BlockSpec((tm, D), lambda i: (i, 0)),
            pl.BlockSpec((Wa + Wb, D), lambda i: (0, 0)),
            pl.BlockSpec((1, D), lambda i: (0, 0)),
            pl.BlockSpec((D, LANES), lambda i: (0, 0)),
        ],
        out_specs=[
            pl.BlockSpec((tm, D), lambda i: (i, 0)),
            pl.BlockSpec((tm, LANES), lambda i: (i, 0)),
            pl.BlockSpec((8, tm), lambda i: (0, i)),
            pl.BlockSpec((8, LANES), lambda i: (0, 0)),
        ],
        out_shape=[
            jax.ShapeDtypeStruct((T, D), F32),
            jax.ShapeDtypeStruct((T, LANES), F32),
            jax.ShapeDtypeStruct((8, T), F32),
            jax.ShapeDtypeStruct((8, LANES), F32),
        ],
        scratch_shapes=[pltpu.VMEM((8, LANES), F32), pltpu.VMEM((Wa + Wb, D), BF16)],
        compiler_params=pltpu.CompilerParams(
            dimension_semantics=("arbitrary",), vmem_limit_bytes=VMEM_LIMIT),
        name="outproj_router",
    )(ya, yb, x2d, wo, fgain, wr)


def _dispatch_kernel(tail_ref, pos_ref, x_ref, xs_ref, zero_ref, sem):
    tm = x_ref.shape[0]
    rt = zero_ref.shape[0]

    @pl.when(pl.program_id(0) == 0)
    def _():
        zero_ref[...] = jnp.zeros_like(zero_ref)

        def tile_fill(j):
            return pltpu.make_async_copy(zero_ref, xs_ref.at[pl.ds(pl.multiple_of(j * rt, 8), rt)], sem)

        def for_each_fill(act):
            for e in range(N_EXPERTS):
                pl.when(tail_ref[e] >= 0)(lambda e=e: act(tile_fill(tail_ref[e])))
            lax.fori_loop(tail_ref[N_EXPERTS], xs_ref.shape[0] // rt, lambda j, c: (act(tile_fill(j)), c)[1], 0)

        for_each_fill(lambda f: f.start())
        for_each_fill(lambda f: f.wait())

    def issue(r, carry):
        for slot in range(2):
            p = pos_ref[0, slot * tm + r]
            pltpu.make_async_copy(x_ref.at[pl.ds(r, 1)], xs_ref.at[pl.ds(p, 1)], sem).start()
        return carry

    lax.fori_loop(0, tm, issue, 0, unroll=8)
    for _ in range(2):
        pltpu.make_async_copy(x_ref, xs_ref.at[pl.ds(0, tm)], sem).wait()


def _dispatch(tails, pos, h, n_rows, tm, rt):
    T, Dp = h.shape
    return pl.pallas_call(
        _dispatch_kernel,
        grid_spec=pltpu.PrefetchScalarGridSpec(
            num_scalar_prefetch=1,
            grid=(T // tm,),
            in_specs=[
                pl.BlockSpec((None, 1, 2 * tm), lambda i, tl: (i, 0, 0), memory_space=pltpu.SMEM),
                pl.BlockSpec((tm, Dp), lambda i, tl: (i, 0)),
            ],
            out_specs=pl.BlockSpec(memory_space=pl.ANY),
            scratch_shapes=[pltpu.VMEM((rt, Dp), h.dtype), pltpu.SemaphoreType.DMA],
        ),
        out_shape=jax.ShapeDtypeStruct((n_rows, Dp), h.dtype),
        compiler_params=pltpu.CompilerParams(
            dimension_semantics=("arbitrary",), vmem_limit_bytes=VMEM_LIMIT),
        name="moe_dispatch",
    )(tails, pos, h)


def _experts_kernel(te_ref, tv_ref, xs_ref, fg_ref, wg_ref, wu_ref, wd_ref, y_ref):
    del te_ref
    i = pl.program_id(0)

    @pl.when(tv_ref[i] > 0)
    def _():
        h = _rms(xs_ref[...], fg_ref[...]).astype(BF16)
        a = _dot(h, wg_ref[...].astype(BF16))
        b = _dot(h, wu_ref[...].astype(BF16))
        y_ref[...] = _dot((_silu(a) * b).astype(BF16), wd_ref[...].astype(BF16))

    @pl.when(tv_ref[i] == 0)
    def _():
        y_ref[...] = jnp.zeros_like(y_ref)


def _experts(tile_expert, tile_valid, xs, fgain, wg, wu, wd, rt):
    n_rows = tile_expert.shape[0] * rt
    Dp = xs.shape[1]
    E, D, F = wg.shape
    return pl.pallas_call(
        _experts_kernel,
        grid_spec=pltpu.PrefetchScalarGridSpec(
            num_scalar_prefetch=2,
            grid=(n_rows // rt,),
            in_specs=[
                pl.BlockSpec((rt, Dp), lambda i, te, tv: (i * tv[i], 0)),
                pl.BlockSpec((1, D), lambda i, te, tv: (0, 0)),
                pl.BlockSpec((None, D, F), lambda i, te, tv: (te[i], 0, 0)),
                pl.BlockSpec((None, D, F), lambda i, te, tv: (te[i], 0, 0)),
                pl.BlockSpec((None, F, D), lambda i, te, tv: (te[i], 0, 0)),
            ],
            out_specs=pl.BlockSpec((rt, D), lambda i, te, tv: (i, 0)),
        ),
        out_shape=jax.ShapeDtypeStruct((n_rows, D), F32),
        compiler_params=pltpu.CompilerParams(
            dimension_semantics=("arbitrary",), vmem_limit_bytes=VMEM_LIMIT),
        name="moe_experts",
    )(tile_expert, tile_valid, xs, fgain, wg, wu, wd)


def _combine_kernel(pos_ref, pos_next_ref, x1_ref, route_ref, fn_ref, y_ref, o_ref, buf_ref, sem):
    tm = x1_ref.shape[0]
    i = pl.program_id(0)
    ring = i % 2

    def row_copy(p_ref, r, slot, rg):
        p = p_ref[0, slot * tm + r]
        return pltpu.make_async_copy(y_ref.at[pl.ds(p, 1)], buf_ref.at[rg, slot, pl.ds(r, 1)], sem.at[rg])

    def issue(p_ref, rg):
        def body(r, carry):
            for slot in range(2):
                row_copy(p_ref, r, slot, rg).start()
            return carry
        lax.fori_loop(0, tm, body, 0, unroll=8)

    @pl.when(i == 0)
    def _():
        issue(pos_ref, ring)

    @pl.when(i + 1 < pl.num_programs(0))
    def _():
        issue(pos_next_ref, 1 - ring)

    for slot in range(2):
        pltpu.make_async_copy(y_ref.at[pl.ds(0, tm)], buf_ref.at[ring, slot], sem.at[ring]).wait()
    route = route_ref[...]
    c1 = route[:, R_C1:R_C1 + 1]
    c2 = route[:, R_C2:R_C2 + 1]
    o_ref[...] = _rms(x1_ref[...] + c1 * buf_ref[ring, 0] + c2 * buf_ref[ring, 1], fn_ref[...])


def _combine(pos, x1, route, fnorm, y, tm):
    T, D = x1.shape
    n = T // tm
    return pl.pallas_call(
        _combine_kernel,
        grid=(n,),
        in_specs=[
            pl.BlockSpec((None, 1, 2 * tm), lambda i: (i, 0, 0), memory_space=pltpu.SMEM),
            pl.BlockSpec((None, 1, 2 * tm), lambda i: (jnp.minimum(i + 1, n - 1), 0, 0),
                         memory_space=pltpu.SMEM),
            pl.BlockSpec((tm, D), lambda i: (i, 0)),
            pl.BlockSpec((tm, LANES), lambda i: (i, 0)),
            pl.BlockSpec((1, D), lambda i: (0, 0)),
            pl.BlockSpec(memory_space=pl.ANY),
        ],
        out_specs=pl.BlockSpec((tm, D), lambda i: (i, 0)),
        out_shape=jax.ShapeDtypeStruct((T, D), F32),
        scratch_shapes=[pltpu.VMEM((2, 2, tm, D), F32), pltpu.SemaphoreType.DMA((2,))],
        compiler_params=pltpu.CompilerParams(
            dimension_semantics=("arbitrary",), vmem_limit_bytes=VMEM_LIMIT),
        name="moe_combine",
    )(pos, pos, x1, route, fnorm, y)


def _pick_tile(T, pref):
    tm = min(pref, T)
    while T % tm:
        tm //= 2
    return tm


def kernel(x, attn_norm, w_in, conv_w, A_log, dt_bias, gdn_norm, w_out, ffn_norm,
           w_group, w_router, w_gate, w_up, w_down, final_norm):
    B, S, D = x.shape
    T = B * S
    depth = w_in.shape[0]
    heads = A_log.shape[1]
    Wb = heads * GDN_HEAD_DIM
    Wa = w_out.shape[1] - Wb
    assert S % MOBA_BLOCK == 0 and S // MOBA_BLOCK <= GATE_ROWS and S % GDN_CHUNK == 0
    assert Wa % LANES == 0 and 2 * heads <= LANES
    assert depth == 1, "the final RMSNorm is fused into the layer's last kernel"

    x2d = x.reshape(T, D)
    for l in range(depth):
        w_all = jnp.pad(w_in[l], ((0, 0), (0, LANES - 2 * heads))).astype(BF16)
        qkv_a, gd, ab = _inproj(x2d, attn_norm[l][None, :], w_all, conv_w[l].astype(F32),
                                3 * Wa, 4 * Wb, _pick_tile(S, 512), S)

        y_a = _moba(qkv_a.reshape(B, S, 3 * Wa), B, S)

        prm = jnp.zeros((8, LANES), F32)
        prm = prm.at[0, :heads].set(A_log[l].astype(F32)).at[1, :heads].set(dt_bias[l].astype(F32))
        y_b = _gdn(gd.reshape(B, S, 4 * Wb), ab.reshape(B, S, LANES), prm,
                   gdn_norm[l][None, :].astype(F32), B, S, heads)

        wr = jnp.concatenate([w_router[l], w_group[l]], axis=1)
        wr = jnp.pad(wr, ((0, 0), (0, LANES - wr.shape[1]))).astype(BF16)
        x1, route, route_t, counts = _outproj(y_a.reshape(T, Wa), y_b.reshape(T, Wb), x2d,
                                              w_out[l].astype(F32), ffn_norm[l][None, :], wr, _pick_tile(T, 1024))

        rt = MOE_ROW_TILE
        n_tiles = -(-2 * T // rt) + N_EXPERTS
        cnt = counts[0, :N_EXPERTS].astype(jnp.int32)
        seg = -(-cnt // rt) * rt
        seg_end = jnp.cumsum(seg)
        seg_start = seg_end - seg
        tile_row = jnp.arange(n_tiles, dtype=jnp.int32) * rt
        tile_expert = jnp.minimum(jnp.sum(tile_row[:, None] >= seg_end[None, :], axis=1), N_EXPERTS - 1)
        tile_valid = (tile_row < seg_end[-1]).astype(jnp.int32)
        e_ids = jnp.arange(N_EXPERTS, dtype=jnp.int32)[:, None]

        def dest_rows(e_lane, rank_lane):
            e = route_t[e_lane].astype(jnp.int32)
            return route_t[rank_lane].astype(jnp.int32) + jnp.sum(
                jnp.where(e[None, :] == e_ids, seg_start[:, None], 0), axis=0)

        pos1, pos2 = dest_rows(R_E1, R_RANK1), dest_rows(R_E2, R_RANK2)

        def tiled_pos(tm):
            return jnp.concatenate([pos1.reshape(T // tm, 1, tm), pos2.reshape(T // tm, 1, tm)], axis=2)

        F = w_gate.shape[-1]
        wge = w_gate[l].reshape(N_EXPERTS, D, F)
        wue = w_up[l].reshape(N_EXPERTS, D, F)
        wde = w_down[l].reshape(N_EXPERTS, F, D)
        tmd = _pick_tile(T, 512)
        tails = jnp.concatenate([jnp.where(seg > 0, seg_end // rt - 1, -1), seg_end[-1:] // rt])
        xs = _dispatch(tails, tiled_pos(tmd), x1, n_tiles * rt, tmd, rt)
        ys = _experts(tile_expert.astype(jnp.int32), tile_valid, xs, ffn_norm[l][None, :],
                      wge, wue, wde, rt)
        tmc = _pick_tile(T, 512)
        x2d = _combine(tiled_pos(tmc), x1, route, final_norm[None, :], ys, tmc)
    return x2d.reshape(B, S, D)
```

```python
import functools

import jax
import jax.numpy as jnp
from jax import lax
from jax.experimental import pallas as pl
from jax.experimental.pallas import tpu as pltpu

F32 = jnp.float32
BF16 = jnp.bfloat16

LANES = 128
MOBA_HEAD_DIM = 64
MOBA_BLOCK = 256
MOBA_TOPK = 3
MOBA_BATCH_ROWS = 2
MOBA_Q_SCALE = MOBA_HEAD_DIM ** -0.5 * 1.4426950408889634
GDN_HEAD_DIM = 128
GDN_CONV = 4
GDN_CHUNK = 256
GDN_BATCH_ROWS = 2
N_GROUPS = 4
EXPERTS_PER_GROUP = 8
N_EXPERTS = N_GROUPS * EXPERTS_PER_GROUP
RMS_EPS = 1e-6
NEG = -1e30
GATE_ROWS = 16
MOE_ROW_TILE = 512
ROUTER_SUBTILE = 128
VMEM_LIMIT = 48 * 1024 * 1024


def _dot(a, b):
    return jnp.dot(a, b, preferred_element_type=F32)


def _dot_nt(a, b):
    return lax.dot_general(a, b, (((1,), (1,)), ((), ())), preferred_element_type=F32)


def _sigmoid(x):
    return 1.0 / (1.0 + jnp.exp(-x))


def _silu(x):
    return x * _sigmoid(x)


def _rms(x, gain):
    return x * lax.rsqrt(jnp.mean(x * x, axis=-1, keepdims=True) + RMS_EPS) * gain


def _inproj_kernel(x_ref, gain_ref, w_ref, cw_ref, om_ref, og_ref, oab_ref, prev_ref, *, tiles_per_seq):
    nm, ng = om_ref.shape[1], og_ref.shape[1]
    tm = x_ref.shape[0]

    @pl.when(pl.program_id(0) % tiles_per_seq == 0)
    def _():
        prev_ref[...] = jnp.zeros_like(prev_ref)

    W = ng // 4
    Dh = GDN_HEAD_DIM
    cw = cw_ref[...]
    half = tm // 2
    h_top = _rms(x_ref[:half, :], gain_ref[...]).astype(BF16)
    h_bot = _rms(x_ref[half:, :], gain_ref[...]).astype(BF16)

    def gdn_tail(rows, og, halo):
        raw = og[:, :3 * W]
        ext = jnp.concatenate([halo, raw], axis=0)
        conv = cw[GDN_CONV - 1:GDN_CONV] * raw
        for d in range(1, GDN_CONV):
            conv = conv + cw[GDN_CONV - 1 - d:GDN_CONV - d] * pltpu.roll(ext, d, 0)[8:]
        qkv = _silu(conv)
        for j in range(2 * W // Dh):
            t = qkv[:, j * Dh:(j + 1) * Dh]
            t = t * lax.rsqrt(jnp.sum(t * t, axis=-1, keepdims=True) + 1e-6)
            og_ref[rows, j * Dh:(j + 1) * Dh] = t * (Dh ** -0.5) if j < W // Dh else t
        og_ref[rows, 2 * W:3 * W] = qkv[:, 2 * W:]
        og_ref[rows, 3 * W:] = og[:, 3 * W:]
        return raw[half - 8:, :]

    og_top = _dot(h_top, w_ref[:, nm:nm + ng])
    og_bot = _dot(h_bot, w_ref[:, nm:nm + ng])
    last = gdn_tail(slice(0, half), og_top, prev_ref[...])
    h = jnp.concatenate([h_top, h_bot], axis=0)
    om = _dot(h, w_ref[:, :nm])
    nq = nm // 3
    om_ref[:, :nq] = (om[:, :nq] * MOBA_Q_SCALE).astype(BF16)
    om_ref[:, nq:] = om[:, nq:].astype(BF16)
    prev_ref[...] = gdn_tail(slice(half, tm), og_bot, last)
    oab_ref[...] = _dot(h, w_ref[:, nm + ng:])


def _inproj(x2d, gain, w_all, conv_w, nm, ng, tm, seq_len):
    T, D = x2d.shape
    assert w_all.shape[1] == nm + ng + LANES and nm % LANES == 0 and ng % LANES == 0
    assert seq_len % tm == 0 and tm >= 16
    return pl.pallas_call(
        functools.partial(_inproj_kernel, tiles_per_seq=seq_len // tm),
        grid=(T // tm,),
        in_specs=[
            pl.BlockSpec((tm, D), lambda i: (i, 0)),
            pl.BlockSpec((1, D), lambda i: (0, 0)),
            pl.BlockSpec((D, nm + ng + LANES), lambda i: (0, 0)),
            pl.BlockSpec((GDN_CONV, 3 * ng // 4), lambda i: (0, 0)),
        ],
        out_specs=[
            pl.BlockSpec((tm, nm), lambda i: (i, 0)),
            pl.BlockSpec((tm, ng), lambda i: (i, 0)),
            pl.BlockSpec((tm, LANES), lambda i: (i, 0)),
        ],
        out_shape=[
            jax.ShapeDtypeStruct((T, nm), BF16),
            jax.ShapeDtypeStruct((T, ng), F32),
            jax.ShapeDtypeStruct((T, LANES), F32),
        ],
        scratch_shapes=[pltpu.VMEM((8, 3 * ng // 4), F32)],
        compiler_params=pltpu.CompilerParams(
            dimension_semantics=("arbitrary",), vmem_limit_bytes=VMEM_LIMIT),
        name="inproj",
    )(x2d, gain, w_all, conv_w)


def _moba_kernel(q_ref, k_ref, v_ref, o_ref, *, S):
    nb = S // MOBA_BLOCK
    L = MOBA_BLOCK
    lane = lax.broadcasted_iota(jnp.int32, (1, LANES), 1)
    key_blk = lax.broadcasted_iota(jnp.int32, (S, LANES), 0) // L
    lane_s = lax.broadcasted_iota(jnp.int32, (S, LANES), 1)
    qblk = lax.broadcasted_iota(jnp.int32, (GATE_ROWS, S), 1) // L
    cidx = lax.broadcasted_iota(jnp.int32, (GATE_ROWS, S), 0)
    r_i = lax.broadcasted_iota(jnp.int32, (L, L), 0)
    c_i = lax.broadcasted_iota(jnp.int32, (L, L), 1)
    causal = c_i <= r_i
    hms = [(lane >= MOBA_HEAD_DIM * h) & (lane < MOBA_HEAD_DIM * (h + 1)) for h in range(2)]

    chains = [(bi, h) for bi in range(q_ref.shape[0]) for h in range(2)]
    qa, ka, va = [], [], []
    for bi, h in chains:
        q, k, v = q_ref[bi], k_ref[bi], v_ref[bi]
        hm = hms[h]
        off = MOBA_HEAD_DIM * (1 - h)
        km = jnp.mean(k.astype(F32).reshape(nb, L, LANES), axis=1)
        km = jnp.concatenate([km, jnp.zeros((GATE_ROWS - nb, LANES), F32)], axis=0).astype(BF16)
        qh = jnp.where(hm, q, jnp.zeros_like(q))
        gate = _dot_nt(km, qh)
        valid = cidx < qblk
        gate = jnp.where(valid, gate, -jnp.inf)
        rank = jnp.zeros((GATE_ROWS, S), F32)
        for i in range(nb):
            gi = gate[i:i + 1, :]
            beats = (gi > gate) | ((gi == gate) & (i < cidx))
            rank = rank + beats.astype(F32)
        sel = valid & (rank < float(MOBA_TOPK))
        pen = jnp.where(sel | (cidx >= qblk), 0.0, NEG)
        pads = [jnp.zeros((r, S), F32) for r in (off, LANES - off - GATE_ROWS)]
        pen = jnp.concatenate(([pads[0]] if off else []) + [pen, pads[1]], axis=0)
        pen_q = pen.T.astype(BF16)
        qa.append(jnp.where(hm, q, pen_q))
        onehot = ((lane_s - off) == key_blk).astype(BF16)
        ka.append(jnp.where(hm, k, onehot))
        va.append(jnp.where(hm, v, jnp.ones_like(v)))

    HH = range(len(chains))
    for n in range(nb):
        s = [_dot_nt(qa[h][n * L:(n + 1) * L], ka[h][:(n + 1) * L]) for h in HH]
        s_own = [jnp.where(causal, s[h][:, n * L:], NEG) for h in HH]
        m = [jnp.max(s_own[h], axis=-1, keepdims=True) for h in HH]
        if n > 0:
            m = [jnp.maximum(m[h], jnp.max(s[h][:, :n * L], axis=-1, keepdims=True)) for h in HH]
            p = [jnp.concatenate([jnp.exp2(s[h][:, :n * L] - m[h]), jnp.exp2(s_own[h] - m[h])],
                                 axis=1).astype(BF16) for h in HH]
        else:
            p = [jnp.exp2(s_own[h] - m[h]).astype(BF16) for h in HH]
        acc = [_dot(p[h], va[h][:(n + 1) * L]) for h in HH]
        outs = [acc[h] / pltpu.roll(acc[h], MOBA_HEAD_DIM, 1) for h in HH]
        for bi in range(q_ref.shape[0]):
            o_ref[bi, n * L:(n + 1) * L, :] = jnp.where(hms[0], outs[2 * bi], outs[2 * bi + 1]).astype(BF16)


def _moba(qkv, B, S):
    W = qkv.shape[-1] // 3
    npair = W // LANES
    nbr = MOBA_BATCH_ROWS if B % MOBA_BATCH_ROWS == 0 else 1
    return pl.pallas_call(
        functools.partial(_moba_kernel, S=S),
        grid=(B // nbr, npair),
        in_specs=[
            pl.BlockSpec((nbr, S, LANES), lambda b, j: (b, 0, j)),
            pl.BlockSpec((nbr, S, LANES), lambda b, j: (b, 0, npair + j)),
            pl.BlockSpec((nbr, S, LANES), lambda b, j: (b, 0, 2 * npair + j)),
        ],
        out_specs=pl.BlockSpec((nbr, S, LANES), lambda b, j: (b, 0, j)),
        out_shape=jax.ShapeDtypeStruct((B, S, W), BF16),
        compiler_params=pltpu.CompilerParams(
            dimension_semantics=("arbitrary", "arbitrary"), vmem_limit_bytes=VMEM_LIMIT),
        name="moba",
    )(qkv, qkv, qkv)


def _gdn_kernel(gd_ref, ab_ref, prm_ref, gn_ref, o_ref, state_ref, *, heads):
    C = GDN_CHUNK
    Dh = GDN_HEAD_DIM
    W = heads * Dh
    c = pl.program_id(1)

    @pl.when(c == 0)
    def _():
        state_ref[...] = jnp.zeros_like(state_ref)

    nbatch = gd_ref.shape[0]
    prm = prm_ref[...]
    rows = lax.broadcasted_iota(jnp.int32, (C, LANES), 0)
    gds, Gs, GTs, betas = [], [], [], []
    for bi in range(nbatch):
        gds.append(gd_ref[bi])
        ab = ab_ref[bi]
        sp_in = ab + prm[1:2]
        softplus = jnp.maximum(sp_in, 0.0) + jnp.log1p(jnp.exp(-jnp.abs(sp_in)))
        G = -jnp.exp(prm[0:1]) * softplus
        betas.append(_sigmoid(ab))
        d = 1
        while d < C:
            G = G + jnp.where(rows >= d, pltpu.roll(G, d, 0), 0.0)
            d *= 2
        Gs.append(G)
        GTs.append(G.T)

    r_i = lax.broadcasted_iota(jnp.int32, (C, C), 0)
    c_i = lax.broadcasted_iota(jnp.int32, (C, C), 1)
    incl = c_i <= r_i
    strict = c_i < r_i
    rc = r_i ^ c_i
    level = jnp.full((C, C), -1, jnp.int32)
    for b in range(C.bit_length() - 1):
        level = level + (rc >= (1 << b)).astype(jnp.int32)
    level_b = level.astype(F32).astype(BF16)
    gn = gn_ref[...]

    chains = [(bi, hd) for bi in range(nbatch) for hd in range(heads)]
    H = range(len(chains))
    q = [gds[bi][:, hd * Dh:(hd + 1) * Dh] for bi, hd in chains]
    k = [gds[bi][:, W + hd * Dh:W + (hd + 1) * Dh] for bi, hd in chains]
    v = [gds[bi][:, 2 * W + hd * Dh:2 * W + (hd + 1) * Dh] for bi, hd in chains]
    Gc = [Gs[bi][:, hd:hd + 1] for bi, hd in chains]
    Gr = [GTs[bi][hd:hd + 1, :] for bi, hd in chains]
    bc = [betas[bi][:, heads + hd:heads + hd + 1] for bi, hd in chains]
    gl = [Gs[bi][C - 1:C, hd:hd + 1] for bi, hd in chains]

    kb = [t.astype(BF16) for t in k]
    decay = [jnp.exp(jnp.where(incl, Gc[h] - Gr[h], -jnp.inf)) for h in H]
    A = [jnp.where(strict, bc[h] * _dot_nt(kb[h], kb[h]) * decay[h], 0.0) for h in H]
    qk = [(_dot_nt(q[h].astype(BF16), kb[h]) * decay[h]).astype(BF16) for h in H]
    gam = [jnp.exp(t) for t in Gc]
    X = [jnp.concatenate([bc[h] * v[h], (bc[h] * gam[h]) * k[h]], axis=1) for h in H]
    Ab = [t.astype(BF16) for t in A]
    zero = jnp.zeros((C, C), BF16)
    Tm = [jnp.where(level_b == -1.0, jnp.ones((C, C), BF16), jnp.where(level_b == 0.0, -t, zero))
          for t in Ab]
    for lv in range(1, C.bit_length() - 2):
        at_lv = level_b == float(lv)
        E = [jnp.where(at_lv, t, zero) for t in Ab]
        F = [_dot(E[h], Tm[h]).astype(BF16) for h in H]
        Tm = [Tm[h] - _dot(Tm[h], F[h]).astype(BF16) for h in H]
    hc = C // 2
    F21 = [_dot(Ab[h][hc:, :hc], Tm[h][:hc, :hc]).astype(BF16) for h in H]
    T21 = [-_dot(Tm[h][hc:, hc:], F21[h]).astype(BF16) for h in H]
    Tm = [jnp.concatenate([Tm[h][:hc, :], jnp.concatenate([T21[h], Tm[h][hc:, hc:]], axis=1)], axis=0)
          for h in H]
    X = [X[h] + _dot(jnp.where(level_b == -1.0, zero, Tm[h]), X[h].astype(BF16)) for h in H]

    S0 = [state_ref[h] for h in H]
    Sb = [t.astype(BF16) for t in S0]
    ub = [(X[h][:, :Dh] - _dot(X[h][:, Dh:].astype(BF16), Sb[h])).astype(BF16) for h in H]
    o = [_dot((q[h] * gam[h]).astype(BF16), Sb[h]) + _dot(qk[h], ub[h]) for h in H]
    k_dec = [(k[h] * jnp.exp(gl[h] - Gc[h])).T.astype(BF16) for h in H]
    for h in H:
        state_ref[h] = jnp.exp(gl[h]) * S0[h] + _dot(k_dec[h], ub[h])
    for h, (bi, hd) in enumerate(chains):
        z = gds[bi][:, 3 * W + hd * Dh:3 * W + (hd + 1) * Dh]
        o_ref[bi, :, hd * Dh:(hd + 1) * Dh] = (_rms(o[h], gn) * _silu(z)).astype(BF16)


def _gdn(gd, ab, prm, gdn_norm, B, S, heads):
    C = GDN_CHUNK
    W = heads * GDN_HEAD_DIM
    nbatch = GDN_BATCH_ROWS if B % GDN_BATCH_ROWS == 0 else 1
    return pl.pallas_call(
        functools.partial(_gdn_kernel, heads=heads),
        grid=(B // nbatch, S // C),
        in_specs=[
            pl.BlockSpec((nbatch, C, 4 * W), lambda b, c: (b, c, 0)),
            pl.BlockSpec((nbatch, C, LANES), lambda b, c: (b, c, 0)),
            pl.BlockSpec((8, LANES), lambda b, c: (0, 0)),
            pl.BlockSpec((1, GDN_HEAD_DIM), lambda b, c: (0, 0)),
        ],
        out_specs=pl.BlockSpec((nbatch, C, W), lambda b, c: (b, c, 0)),
        out_shape=jax.ShapeDtypeStruct((B, S, W), BF16),
        scratch_shapes=[pltpu.VMEM((nbatch * heads, GDN_HEAD_DIM, GDN_HEAD_DIM), F32)],
        compiler_params=pltpu.CompilerParams(
            dimension_semantics=("arbitrary", "arbitrary"), vmem_limit_bytes=VMEM_LIMIT),
        name="gdn",
    )(gd, ab, prm, gdn_norm)


R_E1, R_E2, R_C1, R_C2, R_RANK1, R_RANK2 = range(6)


def _outproj_kernel(ya_ref, yb_ref, x_ref, wo_ref, fg_ref, wr_ref,
                    x1_ref, route_ref, route_t_ref, cnt_ref, carry_ref, wo16_ref):
    @pl.when(pl.program_id(0) == 0)
    def _():
        carry_ref[...] = jnp.zeros_like(carry_ref)
        wo16_ref[...] = wo_ref[...].astype(BF16)

    wa = ya_ref.shape[1]
    ts = ROUTER_SUBTILE
    subs = [pl.ds(j * ts, ts) for j in range(x_ref.shape[0] // ts)]
    each = lambda f, *lists: [f(*args) for args in zip(*lists)]
    rmax = lambda t: jnp.max(t, axis=-1, keepdims=True)
    rmin = lambda t: jnp.min(t, axis=-1, keepdims=True)
    rsum = lambda t: jnp.sum(t, axis=-1, keepdims=True)

    x1 = [x_ref[sl, :] + _dot(ya_ref[sl, :], wo16_ref[:wa, :]) + _dot(yb_ref[sl, :], wo16_ref[wa:, :])
          for sl in subs]
    for sl, t in zip(subs, x1):
        x1_ref[sl, :] = t
    h = each(lambda t: _rms(t, fg_ref[...]).astype(BF16), x1)
    logits = each(lambda t: _dot(t, wr_ref[...]), h)
    lane = lax.broadcasted_iota(jnp.int32, (ts, LANES), 1).astype(F32)
    big = float(LANES)

    is_g = (lane >= N_EXPERTS) & (lane < N_EXPERTS + N_GROUPS)
    lg = each(lambda t: jnp.where(is_g, t, -jnp.inf), logits)
    mg = each(rmax, lg)
    g_sel = each(lambda a, b: rmin(jnp.where(a == b, lane, big)) - N_EXPERTS, lg, mg)
    p_group = each(lambda a, b: 1.0 / rsum(jnp.exp(a - b)), lg, mg)

    le = each(lambda t, g: jnp.where((lane >= g * EXPERTS_PER_GROUP) & (lane < (g + 1) * EXPERTS_PER_GROUP),
                                     t, -jnp.inf), logits, g_sel)
    m1 = each(rmax, le)
    i1 = each(lambda a, b: rmin(jnp.where(a == b, lane, big)), le, m1)
    le2 = each(lambda a, i: jnp.where(lane == i, -jnp.inf, a), le, i1)
    m2 = each(rmax, le2)
    i2 = each(lambda a, b: rmin(jnp.where(a == b, lane, big)), le2, m2)
    se = each(lambda a, b: rsum(jnp.exp(a - b)), le, m1)
    p1 = each(lambda s: 1.0 / s, se)
    p2 = each(lambda a, b, s: jnp.exp(a - b) / s, m2, m1, se)
    c1 = each(lambda pg, a, b: pg * (a / (a + b)), p_group, p1, p2)
    c2 = each(lambda pg, a, b: pg * (b / (a + b)), p_group, p1, p2)

    hot = each(lambda a, b: ((lane == a) | (lane == b)).astype(BF16), i1, i2)
    r_i = lax.broadcasted_iota(jnp.int32, (ts, ts), 0)
    c_i = lax.broadcasted_iota(jnp.int32, (ts, ts), 1)
    before = (c_i < r_i).astype(BF16)
    within = each(lambda t: _dot(before, t), hot)
    totals = each(lambda t: jnp.sum(t.astype(F32), axis=0, keepdims=True), hot)
    base = [carry_ref[0:1, :]]
    for t in totals:
        base.append(base[-1] + t)
    seen = each(lambda a, b: a + b, within, base[:-1])
    rank1 = each(lambda i, s: rsum(jnp.where(lane == i, s, 0.0)), i1, seen)
    rank2 = each(lambda i, s: rsum(jnp.where(lane == i, s, 0.0)), i2, seen)
    carry_ref[...] = jnp.broadcast_to(base[-1], carry_ref.shape)
    cnt_ref[...] = jnp.broadcast_to(base[-1], cnt_ref.shape)

    for j, sl in enumerate(subs):
        rec = jnp.zeros((ts, LANES), F32)
        for slot, val in ((R_E1, i1), (R_E2, i2), (R_C1, c1), (R_C2, c2), (R_RANK1, rank1), (R_RANK2, rank2)):
            rec = jnp.where(lane == float(slot), val[j], rec)
        route_ref[sl, :] = rec
        route_t_ref[:, sl] = rec.T[0:8, :]


def _outproj(ya, yb, x2d, wo, fgain, wr, tm):
    T, D = x2d.shape
    Wa, Wb = ya.shape[1], yb.shape[1]
    assert tm % ROUTER_SUBTILE == 0
    return pl.pallas_call(
        _outproj_kernel,
        grid=(T // tm,),
        in_specs=[
            pl.BlockSpec((tm, Wa), lambda i: (i, 0)),
            pl.BlockSpec((tm, Wb), lambda i: (i, 0)),
            pl.BlockSpec((tm, D), lambda i: (i, 0)),
            pl.BlockSpec((Wa + Wb, D), lambda i: (0, 0)),
            pl.BlockSpec((1, D), lambda i: (0, 0)),
            pl.BlockSpec((D, LANES), lambda i: (0, 0)),
        ],
        out_specs=[
            pl.BlockSpec((tm, D), lambda i: (i, 0)),
            pl.BlockSpec((tm, LANES), lambda i: (i, 0)),
            pl.BlockSpec((8, tm), lambda i: (0, i)),
            pl.BlockSpec((8, LANES), lambda i: (0, 0)),
        ],
        out_shape=[
            jax.ShapeDtypeStruct((T, D), F32),
            jax.ShapeDtypeStruct((T, LANES), F32),
            jax.ShapeDtypeStruct((8, T), F32),
            jax.ShapeDtypeStruct((8, LANES), F32),
        ],
        scratch_shapes=[pltpu.VMEM((8, LANES), F32), pltpu.VMEM((Wa + Wb, D), BF16)],
        compiler_params=pltpu.CompilerParams(
            dimension_semantics=("arbitrary",), vmem_limit_bytes=VMEM_LIMIT),
        name="outproj_router",
    )(ya, yb, x2d, wo, fgain, wr)


def _dispatch_kernel(tail_ref, pos_ref, x_ref, xs_ref, zero_ref, sem):
    tm = x_ref.shape[0]
    rt = zero_ref.shape[0]

    @pl.when(pl.program_id(0) == 0)
    def _():
        zero_ref[...] = jnp.zeros_like(zero_ref)

        def tile_fill(j):
            return pltpu.make_async_copy(zero_ref, xs_ref.at[pl.ds(pl.multiple_of(j * rt, 8), rt)], sem)

        def for_each_fill(act):
            for e in range(N_EXPERTS):
                pl.when(tail_ref[e] >= 0)(lambda e=e: act(tile_fill(tail_ref[e])))
            lax.fori_loop(tail_ref[N_EXPERTS], xs_ref.shape[0] // rt, lambda j, c: (act(tile_fill(j)), c)[1], 0)

        for_each_fill(lambda f: f.start())
        for_each_fill(lambda f: f.wait())

    def issue(r, carry):
        for slot in range(2):
            p = pos_ref[0, slot * tm + r]
            pltpu.make_async_copy(x_ref.at[pl.ds(r, 1)], xs_ref.at[pl.ds(p, 1)], sem).start()
        return carry

    lax.fori_loop(0, tm, issue, 0, unroll=128)
    for _ in range(2):
        pltpu.make_async_copy(x_ref, xs_ref.at[pl.ds(0, tm)], sem).wait()


def _dispatch(tails, pos, h, n_rows, tm, rt):
    T, Dp = h.shape
    return pl.pallas_call(
        _dispatch_kernel,
        grid_spec=pltpu.PrefetchScalarGridSpec(
            num_scalar_prefetch=1,
            grid=(T // tm,),
            in_specs=[
                pl.BlockSpec((None, 1, 2 * tm), lambda i, tl: (i, 0, 0), memory_space=pltpu.SMEM),
                pl.BlockSpec((tm, Dp), lambda i, tl: (i, 0)),
            ],
            out_specs=pl.BlockSpec(memory_space=pl.ANY),
            scratch_shapes=[pltpu.VMEM((rt, Dp), h.dtype), pltpu.SemaphoreType.DMA],
        ),
        out_shape=jax.ShapeDtypeStruct((n_rows, Dp), h.dtype),
        compiler_params=pltpu.CompilerParams(
            dimension_semantics=("arbitrary",), vmem_limit_bytes=VMEM_LIMIT),
        name="moe_dispatch",
    )(tails, pos, h)


def _experts_kernel(te_ref, tv_ref, xs_ref, fg_ref, wg_ref, wu_ref, wd_ref, y_ref):
    del te_ref
    i = pl.program_id(0)

    @pl.when(tv_ref[i] > 0)
    def _():
        h = _rms(xs_ref[...], fg_ref[...]).astype(BF16)
        a = _dot(h, wg_ref[...].astype(BF16))
        b = _dot(h, wu_ref[...].astype(BF16))
        y_ref[...] = _dot((_silu(a) * b).astype(BF16), wd_ref[...].astype(BF16))

    @pl.when(tv_ref[i] == 0)
    def _():
        y_ref[...] = jnp.zeros_like(y_ref)


def _experts(tile_expert, tile_valid, xs, fgain, wg, wu, wd, rt):
    n_rows = tile_expert.shape[0] * rt
    Dp = xs.shape[1]
    E, D, F = wg.shape
    return pl.pallas_call(
        _experts_kernel,
        grid_spec=pltpu.PrefetchScalarGridSpec(
            num_scalar_prefetch=2,
            grid=(n_rows // rt,),
            in_specs=[
                pl.BlockSpec((rt, Dp), lambda i, te, tv: (i * tv[i], 0)),
                pl.BlockSpec((1, D), lambda i, te, tv: (0, 0)),
                pl.BlockSpec((None, D, F), lambda i, te, tv: (te[i], 0, 0)),
                pl.BlockSpec((None, D, F), lambda i, te, tv: (te[i], 0, 0)),
                pl.BlockSpec((None, F, D), lambda i, te, tv: (te[i], 0, 0)),
            ],
            out_specs=pl.BlockSpec((rt, D), lambda i, te, tv: (i, 0)),
        ),
        out_shape=jax.ShapeDtypeStruct((n_rows, D), F32),
        compiler_params=pltpu.CompilerParams(
            dimension_semantics=("arbitrary",), vmem_limit_bytes=VMEM_LIMIT),
        name="moe_experts",
    )(tile_expert, tile_valid, xs, fgain, wg, wu, wd)


def _combine_kernel(pos_ref, pos_next_ref, x1_ref, route_ref, fn_ref, y_ref, o_ref, buf_ref, sem):
    tm = x1_ref.shape[0]
    i = pl.program_id(0)
    ring = i % 2

    def row_copy(p_ref, r, slot, rg):
        p = p_ref[0, slot * tm + r]
        return pltpu.make_async_copy(y_ref.at[pl.ds(p, 1)], buf_ref.at[rg, slot, pl.ds(r, 1)], sem.at[rg])

    def issue(p_ref, rg):
        def body(r, carry):
            for slot in range(2):
                row_copy(p_ref, r, slot, rg).start()
            return carry
        lax.fori_loop(0, tm, body, 0, unroll=128)

    @pl.when(i == 0)
    def _():
        issue(pos_ref, ring)

    @pl.when(i + 1 < pl.num_programs(0))
    def _():
        issue(pos_next_ref, 1 - ring)

    for slot in range(2):
        pltpu.make_async_copy(y_ref.at[pl.ds(0, tm)], buf_ref.at[ring, slot], sem.at[ring]).wait()
    route = route_ref[...]
    c1 = route[:, R_C1:R_C1 + 1]
    c2 = route[:, R_C2:R_C2 + 1]
    o_ref[...] = _rms(x1_ref[...] + c1 * buf_ref[ring, 0] + c2 * buf_ref[ring, 1], fn_ref[...])


def _combine(pos, x1, route, fnorm, y, tm):
    T, D = x1.shape
    n = T // tm
    return pl.pallas_call(
        _combine_kernel,
        grid=(n,),
        in_specs=[
            pl.BlockSpec((None, 1, 2 * tm), lambda i: (i, 0, 0), memory_space=pltpu.SMEM),
            pl.BlockSpec((None, 1, 2 * tm), lambda i: (jnp.minimum(i + 1, n - 1), 0, 0),
                         memory_space=pltpu.SMEM),
            pl.BlockSpec((tm, D), lambda i: (i, 0)),
            pl.BlockSpec((tm, LANES), lambda i: (i, 0)),
            pl.BlockSpec((1, D), lambda i: (0, 0)),
            pl.BlockSpec(memory_space=pl.ANY),
        ],
        out_specs=pl.BlockSpec((tm, D), lambda i: (i, 0)),
        out_shape=jax.ShapeDtypeStruct((T, D), F32),
        scratch_shapes=[pltpu.VMEM((2, 2, tm, D), F32), pltpu.SemaphoreType.DMA((2,))],
        compiler_params=pltpu.CompilerParams(
            dimension_semantics=("arbitrary",), vmem_limit_bytes=VMEM_LIMIT),
        name="moe_combine",
    )(pos, pos, x1, route, fnorm, y)


def _pick_tile(T, pref):
    tm = min(pref, T)
    while T % tm:
        tm //= 2
    return tm


def kernel(x, attn_norm, w_in, conv_w, A_log, dt_bias, gdn_norm, w_out, ffn_norm,
           w_group, w_router, w_gate, w_up, w_down, final_norm):
    B, S, D = x.shape
    T = B * S
    depth = w_in.shape[0]
    heads = A_log.shape[1]
    Wb = heads * GDN_HEAD_DIM
    Wa = w_out.shape[1] - Wb
    assert S % MOBA_BLOCK == 0 and S // MOBA_BLOCK <= GATE_ROWS and S % GDN_CHUNK == 0
    assert Wa % LANES == 0 and 2 * heads <= LANES
    assert depth == 1, "the final RMSNorm is fused into the layer's last kernel"

    x2d = x.reshape(T, D)
    for l in range(depth):
        w_all = jnp.pad(w_in[l], ((0, 0), (0, LANES - 2 * heads))).astype(BF16)
        qkv_a, gd, ab = _inproj(x2d, attn_norm[l][None, :], w_all, conv_w[l].astype(F32),
                                3 * Wa, 4 * Wb, _pick_tile(S, 512), S)

        y_a = _moba(qkv_a.reshape(B, S, 3 * Wa), B, S)

        prm = jnp.zeros((8, LANES), F32)
        prm = prm.at[0, :heads].set(A_log[l].astype(F32)).at[1, :heads].set(dt_bias[l].astype(F32))
        y_b = _gdn(gd.reshape(B, S, 4 * Wb), ab.reshape(B, S, LANES), prm,
                   gdn_norm[l][None, :].astype(F32), B, S, heads)

        wr = jnp.concatenate([w_router[l], w_group[l]], axis=1)
        wr = jnp.pad(wr, ((0, 0), (0, LANES - wr.shape[1]))).astype(BF16)
        x1, route, route_t, counts = _outproj(y_a.reshape(T, Wa), y_b.reshape(T, Wb), x2d,
                                              w_out[l].astype(F32), ffn_norm[l][None, :], wr, _pick_tile(T, 1024))

        rt = MOE_ROW_TILE
        n_tiles = -(-2 * T // rt) + N_EXPERTS
        cnt = counts[0, :N_EXPERTS].astype(jnp.int32)
        seg = -(-cnt // rt) * rt
        seg_end = jnp.cumsum(seg)
        seg_start = seg_end - seg
        tile_row = jnp.arange(n_tiles, dtype=jnp.int32) * rt
        tile_expert = jnp.minimum(jnp.sum(tile_row[:, None] >= seg_end[None, :], axis=1), N_EXPERTS - 1)
        tile_valid = (tile_row < seg_end[-1]).astype(jnp.int32)
        e_ids = jnp.arange(N_EXPERTS, dtype=jnp.int32)[:, None]

        def dest_rows(e_lane, rank_lane):
            e = route_t[e_lane].astype(jnp.int32)
            return route_t[rank_lane].astype(jnp.int32) + jnp.sum(
                jnp.where(e[None, :] == e_ids, seg_start[:, None], 0), axis=0)

        pos1, pos2 = dest_rows(R_E1, R_RANK1), dest_rows(R_E2, R_RANK2)

        def tiled_pos(tm):
            return jnp.concatenate([pos1.reshape(T // tm, 1, tm), pos2.reshape(T // tm, 1, tm)], axis=2)

        F = w_gate.shape[-1]
        wge = w_gate[l].reshape(N_EXPERTS, D, F)
        wue = w_up[l].reshape(N_EXPERTS, D, F)
        wde = w_down[l].reshape(N_EXPERTS, F, D)
        tmd = _pick_tile(T, 512)
        tails = jnp.concatenate([jnp.where(seg > 0, seg_end // rt - 1, -1), seg_end[-1:] // rt])
        xs = _dispatch(tails, tiled_pos(tmd), x1, n_tiles * rt, tmd, rt)
        ys = _experts(tile_expert.astype(jnp.int32), tile_valid, xs, ffn_norm[l][None, :],
                      wge, wue, wde, rt)
        tmc = _pick_tile(T, 512)
        x2d = _combine(tiled_pos(tmc), x1, route, final_norm[None, :], ys, tmc)
    return x2d.reshape(B, S, D)
```

```python
import functools

import jax
import jax.numpy as jnp
from jax import lax
from jax.experimental import pallas as pl
from jax.experimental.pallas import tpu as pltpu

F32 = jnp.float32
BF16 = jnp.bfloat16

LANES = 128
MOBA_HEAD_DIM = 64
MOBA_BLOCK = 256
MOBA_TOPK = 3
MOBA_BATCH_ROWS = 2
MOBA_Q_SCALE = MOBA_HEAD_DIM ** -0.5 * 1.4426950408889634
GDN_HEAD_DIM = 128
GDN_CONV = 4
GDN_CHUNK = 256
GDN_BATCH_ROWS = 2
N_GROUPS = 4
EXPERTS_PER_GROUP = 8
N_EXPERTS = N_GROUPS * EXPERTS_PER_GROUP
RMS_EPS = 1e-6
NEG = -1e30
GATE_ROWS = 16
MOE_ROW_TILE = 512
ROUTER_SUBTILE = 128
VMEM_LIMIT = 48 * 1024 * 1024


def _dot(a, b):
    return jnp.dot(a, b, preferred_element_type=F32)


def _dot_nt(a, b):
    return lax.dot_general(a, b, (((1,), (1,)), ((), ())), preferred_element_type=F32)


def _sigmoid(x):
    return 1.0 / (1.0 + jnp.exp(-x))


def _silu(x):
    return x * _sigmoid(x)


def _rms(x, gain):
    return x * lax.rsqrt(jnp.mean(x * x, axis=-1, keepdims=True) + RMS_EPS) * gain


def _inproj_kernel(x_ref, gain_ref, w_ref, cw_ref, om_ref, og_ref, oab_ref, prev_ref, *, tiles_per_seq):
    nm, ng = om_ref.shape[1], og_ref.shape[1]
    tm = x_ref.shape[0]

    @pl.when(pl.program_id(0) % tiles_per_seq == 0)
    def _():
        prev_ref[...] = jnp.zeros_like(prev_ref)

    W = ng // 4
    Dh = GDN_HEAD_DIM
    cw = cw_ref[...]
    half = tm // 2
    h_top = _rms(x_ref[:half, :], gain_ref[...]).astype(BF16)
    h_bot = _rms(x_ref[half:, :], gain_ref[...]).astype(BF16)

    def gdn_tail(rows, og, halo):
        raw = og[:, :3 * W]
        ext = jnp.concatenate([halo, raw], axis=0)
        conv = cw[GDN_CONV - 1:GDN_CONV] * raw
        for d in range(1, GDN_CONV):
            conv = conv + cw[GDN_CONV - 1 - d:GDN_CONV - d] * pltpu.roll(ext, d, 0)[8:]
        qkv = _silu(conv)
        for j in range(2 * W // Dh):
            t = qkv[:, j * Dh:(j + 1) * Dh]
            t = t * lax.rsqrt(jnp.sum(t * t, axis=-1, keepdims=True) + 1e-6)
            og_ref[rows, j * Dh:(j + 1) * Dh] = t * (Dh ** -0.5) if j < W // Dh else t
        og_ref[rows, 2 * W:3 * W] = qkv[:, 2 * W:]
        og_ref[rows, 3 * W:] = og[:, 3 * W:]
        return raw[half - 8:, :]

    og_top = _dot(h_top, w_ref[:, nm:nm + ng])
    og_bot = _dot(h_bot, w_ref[:, nm:nm + ng])
    last = gdn_tail(slice(0, half), og_top, prev_ref[...])
    h = jnp.concatenate([h_top, h_bot], axis=0)
    om = _dot(h, w_ref[:, :nm])
    nq = nm // 3
    om_ref[:, :nq] = (om[:, :nq] * MOBA_Q_SCALE).astype(BF16)
    om_ref[:, nq:] = om[:, nq:].astype(BF16)
    prev_ref[...] = gdn_tail(slice(half, tm), og_bot, last)
    oab_ref[...] = _dot(h, w_ref[:, nm + ng:])


def _inproj(x2d, gain, w_all, conv_w, nm, ng, tm, seq_len):
    T, D = x2d.shape
    assert w_all.shape[1] == nm + ng + LANES and nm % LANES == 0 and ng % LANES == 0
    assert seq_len % tm == 0 and tm >= 16
    return pl.pallas_call(
        functools.partial(_inproj_kernel, tiles_per_seq=seq_len // tm),
        grid=(T // tm,),
        in_specs=[
            pl.BlockSpec((tm, D), lambda i: (i, 0)),
            pl.BlockSpec((1, D), lambda i: (0, 0)),
            pl.BlockSpec((D, nm + ng + LANES), lambda i: (0, 0)),
            pl.BlockSpec((GDN_CONV, 3 * ng // 4), lambda i: (0, 0)),
        ],
        out_specs=[
            pl.BlockSpec((tm, nm), lambda i: (i, 0)),
            pl.BlockSpec((tm, ng), lambda i: (i, 0)),
            pl.BlockSpec((tm, LANES), lambda i: (i, 0)),
        ],
        out_shape=[
            jax.ShapeDtypeStruct((T, nm), BF16),
            jax.ShapeDtypeStruct((T, ng), F32),
            jax.ShapeDtypeStruct((T, LANES), F32),
        ],
        scratch_shapes=[pltpu.VMEM((8, 3 * ng // 4), F32)],
        compiler_params=pltpu.CompilerParams(
            dimension_semantics=("arbitrary",), vmem_limit_bytes=VMEM_LIMIT),
        name="inproj",
    )(x2d, gain, w_all, conv_w)


def _moba_kernel(q_ref, k_ref, v_ref, o_ref, *, S):
    nb = S // MOBA_BLOCK
    L = MOBA_BLOCK
    lane = lax.broadcasted_iota(jnp.int32, (1, LANES), 1)
    key_blk = lax.broadcasted_iota(jnp.int32, (S, LANES), 0) // L
    lane_s = lax.broadcasted_iota(jnp.int32, (S, LANES), 1)
    qblk = lax.broadcasted_iota(jnp.int32, (GATE_ROWS, S), 1) // L
    cidx = lax.broadcasted_iota(jnp.int32, (GATE_ROWS, S), 0)
    r_i = lax.broadcasted_iota(jnp.int32, (L, L), 0)
    c_i = lax.broadcasted_iota(jnp.int32, (L, L), 1)
    causal = c_i <= r_i
    hms = [(lane >= MOBA_HEAD_DIM * h) & (lane < MOBA_HEAD_DIM * (h + 1)) for h in range(2)]

    chains = [(bi, h) for bi in range(q_ref.shape[0]) for h in range(2)]
    qa, ka, va = [], [], []
    for bi, h in chains:
        q, k, v = q_ref[bi], k_ref[bi], v_ref[bi]
        hm = hms[h]
        off = MOBA_HEAD_DIM * (1 - h)
        km = jnp.mean(k.astype(F32).reshape(nb, L, LANES), axis=1)
        km = jnp.concatenate([km, jnp.zeros((GATE_ROWS - nb, LANES), F32)], axis=0).astype(BF16)
        qh = jnp.where(hm, q, jnp.zeros_like(q))
        gate = _dot_nt(km, qh)
        valid = cidx < qblk
        gate = jnp.where(valid, gate, -jnp.inf)
        rank = jnp.zeros((GATE_ROWS, S), F32)
        for i in range(nb):
            gi = gate[i:i + 1, :]
            beats = (gi > gate) | ((gi == gate) & (i < cidx))
            rank = rank + beats.astype(F32)
        sel = valid & (rank < float(MOBA_TOPK))
        pen = jnp.where(sel | (cidx >= qblk), 0.0, NEG)
        pads = [jnp.zeros((r, S), F32) for r in (off, LANES - off - GATE_ROWS)]
        pen = jnp.concatenate(([pads[0]] if off else []) + [pen, pads[1]], axis=0)
        pen_q = pen.T.astype(BF16)
        qa.append(jnp.where(hm, q, pen_q))
        onehot = ((lane_s - off) == key_blk).astype(BF16)
        ka.append(jnp.where(hm, k, onehot))
        va.append(jnp.where(hm, v, jnp.ones_like(v)))

    HH = range(len(chains))
    for n in range(nb):
        s = [_dot_nt(qa[h][n * L:(n + 1) * L], ka[h][:(n + 1) * L]) for h in HH]
        s_own = [jnp.where(causal, s[h][:, n * L:], NEG) for h in HH]
        m = [jnp.max(s_own[h], axis=-1, keepdims=True) for h in HH]
        if n > 0:
            m = [jnp.maximum(m[h], jnp.max(s[h][:, :n * L], axis=-1, keepdims=True)) for h in HH]
            p = [jnp.concatenate([jnp.exp2(s[h][:, :n * L] - m[h]), jnp.exp2(s_own[h] - m[h])],
                                 axis=1).astype(BF16) for h in HH]
        else:
            p = [jnp.exp2(s_own[h] - m[h]).astype(BF16) for h in HH]
        acc = [_dot(p[h], va[h][:(n + 1) * L]) for h in HH]
        outs = [acc[h] / pltpu.roll(acc[h], MOBA_HEAD_DIM, 1) for h in HH]
        for bi in range(q_ref.shape[0]):
            o_ref[bi, n * L:(n + 1) * L, :] = jnp.where(hms[0], outs[2 * bi], outs[2 * bi + 1]).astype(BF16)


def _moba(qkv, B, S):
    W = qkv.shape[-1] // 3
    npair = W // LANES
    nbr = MOBA_BATCH_ROWS if B % MOBA_BATCH_ROWS == 0 else 1
    return pl.pallas_call(
        functools.partial(_moba_kernel, S=S),
        grid=(B // nbr, npair),
        in_specs=[
            pl.BlockSpec((nbr, S, LANES), lambda b, j: (b, 0, j)),
            pl.BlockSpec((nbr, S, LANES), lambda b, j: (b, 0, npair + j)),
            pl.BlockSpec((nbr, S, LANES), lambda b, j: (b, 0, 2 * npair + j)),
        ],
        out_specs=pl.BlockSpec((nbr, S, LANES), lambda b, j: (b, 0, j)),
        out_shape=jax.ShapeDtypeStruct((B, S, W), BF16),
        compiler_params=pltpu.CompilerParams(
            dimension_semantics=("arbitrary", "arbitrary"), vmem_limit_bytes=VMEM_LIMIT),
        name="moba",
    )(qkv, qkv, qkv)


def _gdn_kernel(gd_ref, ab_ref, prm_ref, gn_ref, o_ref, state_ref, *, heads):
    C = GDN_CHUNK
    Dh = GDN_HEAD_DIM
    W = heads * Dh
    c = pl.program_id(1)

    @pl.when(c == 0)
    def _():
        state_ref[...] = jnp.zeros_like(state_ref)

    nbatch = gd_ref.shape[0]
    prm = prm_ref[...]
    rows = lax.broadcasted_iota(jnp.int32, (C, LANES), 0)
    gds, Gs, GTs, betas = [], [], [], []
    for bi in range(nbatch):
        gds.append(gd_ref[bi])
        ab = ab_ref[bi]
        sp_in = ab + prm[1:2]
        softplus = jnp.maximum(sp_in, 0.0) + jnp.log1p(jnp.exp(-jnp.abs(sp_in)))
        G = -jnp.exp(prm[0:1]) * softplus
        betas.append(_sigmoid(ab))
        d = 1
        while d < C:
            G = G + jnp.where(rows >= d, pltpu.roll(G, d, 0), 0.0)
            d *= 2
        Gs.append(G)
        GTs.append(G.T)

    r_i = lax.broadcasted_iota(jnp.int32, (C, C), 0)
    c_i = lax.broadcasted_iota(jnp.int32, (C, C), 1)
    incl = c_i <= r_i
    strict = c_i < r_i
    rc = r_i ^ c_i
    level = jnp.full((C, C), -1, jnp.int32)
    for b in range(C.bit_length() - 1):
        level = level + (rc >= (1 << b)).astype(jnp.int32)
    level_b = level.astype(F32).astype(BF16)
    gn = gn_ref[...]

    chains = [(bi, hd) for bi in range(nbatch) for hd in range(heads)]
    H = range(len(chains))
    q = [gds[bi][:, hd * Dh:(hd + 1) * Dh] for bi, hd in chains]
    k = [gds[bi][:, W + hd * Dh:W + (hd + 1) * Dh] for bi, hd in chains]
    v = [gds[bi][:, 2 * W + hd * Dh:2 * W + (hd + 1) * Dh] for bi, hd in chains]
    Gc = [Gs[bi][:, hd:hd + 1] for bi, hd in chains]
    Gr = [GTs[bi][hd:hd + 1, :] for bi, hd in chains]
    bc = [betas[bi][:, heads + hd:heads + hd + 1] for bi, hd in chains]
    gl = [Gs[bi][C - 1:C, hd:hd + 1] for bi, hd in chains]

    kb = [t.astype(BF16) for t in k]
    decay = [jnp.exp(jnp.where(incl, Gc[h] - Gr[h], -jnp.inf)) for h in H]
    A = [jnp.where(strict, bc[h] * _dot_nt(kb[h], kb[h]) * decay[h], 0.0) for h in H]
    qk = [(_dot_nt(q[h].astype(BF16), kb[h]) * decay[h]).astype(BF16) for h in H]
    gam = [jnp.exp(t) for t in Gc]
    X = [jnp.concatenate([bc[h] * v[h], (bc[h] * gam[h]) * k[h]], axis=1) for h in H]
    Ab = [t.astype(BF16) for t in A]
    zero = jnp.zeros((C, C), BF16)
    Tm = [jnp.where(level_b == -1.0, jnp.ones((C, C), BF16), jnp.where(level_b == 0.0, -t, zero))
          for t in Ab]
    for lv in range(1, C.bit_length() - 2):
        at_lv = level_b == float(lv)
        E = [jnp.where(at_lv, t, zero) for t in Ab]
        F = [_dot(E[h], Tm[h]).astype(BF16) for h in H]
        Tm = [Tm[h] - _dot(Tm[h], F[h]).astype(BF16) for h in H]
    hc = C // 2
    F21 = [_dot(Ab[h][hc:, :hc], Tm[h][:hc, :hc]).astype(BF16) for h in H]
    T21 = [-_dot(Tm[h][hc:, hc:], F21[h]).astype(BF16) for h in H]
    Tm = [jnp.concatenate([Tm[h][:hc, :], jnp.concatenate([T21[h], Tm[h][hc:, hc:]], axis=1)], axis=0)
          for h in H]
    X = [X[h] + _dot(jnp.where(level_b == -1.0, zero, Tm[h]), X[h].astype(BF16)) for h in H]

    S0 = [state_ref[h] for h in H]
    Sb = [t.astype(BF16) for t in S0]
    ub = [(X[h][:, :Dh] - _dot(X[h][:, Dh:].astype(BF16), Sb[h])).astype(BF16) for h in H]
    o = [_dot((q[h] * gam[h]).astype(BF16), Sb[h]) + _dot(qk[h], ub[h]) for h in H]
    k_dec = [(k[h] * jnp.exp(gl[h] - Gc[h])).T.astype(BF16) for h in H]
    for h in H:
        state_ref[h] = jnp.exp(gl[h]) * S0[h] + _dot(k_dec[h], ub[h])
    for h, (bi, hd) in enumerate(chains):
        z = gds[bi][:, 3 * W + hd * Dh:3 * W + (hd + 1) * Dh]
        o_ref[bi, :, hd * Dh:(hd + 1) * Dh] = (_rms(o[h], gn) * _silu(z)).astype(BF16)


def _gdn(gd, ab, prm, gdn_norm, B, S, heads):
    C = GDN_CHUNK
    W = heads * GDN_HEAD_DIM
    nbatch = GDN_BATCH_ROWS if B % GDN_BATCH_ROWS == 0 else 1
    return pl.pallas_call(
        functools.partial(_gdn_kernel, heads=heads),
        grid=(B // nbatch, S // C),
        in_specs=[
            pl.BlockSpec((nbatch, C, 4 * W), lambda b, c: (b, c, 0)),
            pl.BlockSpec((nbatch, C, LANES), lambda b, c: (b, c, 0)),
            pl.BlockSpec((8, LANES), lambda b, c: (0, 0)),
            pl.BlockSpec((1, GDN_HEAD_DIM), lambda b, c: (0, 0)),
        ],
        out_specs=pl.BlockSpec((nbatch, C, W), lambda b, c: (b, c, 0)),
        out_shape=jax.ShapeDtypeStruct((B, S, W), BF16),
        scratch_shapes=[pltpu.VMEM((nbatch * heads, GDN_HEAD_DIM, GDN_HEAD_DIM), F32)],
        compiler_params=pltpu.CompilerParams(
            dimension_semantics=("arbitrary", "arbitrary"), vmem_limit_bytes=VMEM_LIMIT),
        name="gdn",
    )(gd, ab, prm, gdn_norm)


R_E1, R_E2, R_C1, R_C2, R_RANK1, R_RANK2 = range(6)


def _outproj_kernel(ya_ref, yb_ref, x_ref, wo_ref, fg_ref, wr_ref,
                    x1_ref, route_ref, route_t_ref, cnt_ref, carry_ref, wo16_ref):
    @pl.when(pl.program_id(0) == 0)
    def _():
        carry_ref[...] = jnp.zeros_like(carry_ref)
        wo16_ref[...] = wo_ref[...].astype(BF16)

    wa = ya_ref.shape[1]
    ts = ROUTER_SUBTILE
    subs = [pl.ds(j * ts, ts) for j in range(x_ref.shape[0] // ts)]
    each = lambda f, *lists: [f(*args) for args in zip(*lists)]
    rmax = lambda t: jnp.max(t, axis=-1, keepdims=True)
    rmin = lambda t: jnp.min(t, axis=-1, keepdims=True)
    rsum = lambda t: jnp.sum(t, axis=-1, keepdims=True)

    x1 = [x_ref[sl, :] + _dot(ya_ref[sl, :], wo16_ref[:wa, :]) + _dot(yb_ref[sl, :], wo16_ref[wa:, :])
          for sl in subs]
    for sl, t in zip(subs, x1):
        x1_ref[sl, :] = t
    h = each(lambda t: _rms(t, fg_ref[...]).astype(BF16), x1)
    logits = each(lambda t: _dot(t, wr_ref[...]), h)
    lane = lax.broadcasted_iota(jnp.int32, (ts, LANES), 1).astype(F32)
    big = float(LANES)

    is_g = (lane >= N_EXPERTS) & (lane < N_EXPERTS + N_GROUPS)
    lg = each(lambda t: jnp.where(is_g, t, -jnp.inf), logits)
    mg = each(rmax, lg)
    g_sel = each(lambda a, b: rmin(jnp.where(a == b, lane, big)) - N_EXPERTS, lg, mg)
    p_group = each(lambda a, b: 1.0 / rsum(jnp.exp(a - b)), lg, mg)

    le = each(lambda t, g: jnp.where((lane >= g * EXPERTS_PER_GROUP) & (lane < (g + 1) * EXPERTS_PER_GROUP),
                                     t, -jnp.inf), logits, g_sel)
    m1 = each(rmax, le)
    i1 = each(lambda a, b: rmin(jnp.where(a == b, lane, big)), le, m1)
    le2 = each(lambda a, i: jnp.where(lane == i, -jnp.inf, a), le, i1)
    m2 = each(rmax, le2)
    i2 = each(lambda a, b: rmin(jnp.where(a == b, lane, big)), le2, m2)
    se = each(lambda a, b: rsum(jnp.exp(a - b)), le, m1)
    p1 = each(lambda s: 1.0 / s, se)
    p2 = each(lambda a, b, s: jnp.exp(a - b) / s, m2, m1, se)
    c1 = each(lambda pg, a, b: pg * (a / (a + b)), p_group, p1, p2)
    c2 = each(lambda pg, a, b: pg * (b / (a + b)), p_group, p1, p2)

    hot = each(lambda a, b: ((lane == a) | (lane == b)).astype(BF16), i1, i2)
    r_i = lax.broadcasted_iota(jnp.int32, (ts, ts), 0)
    c_i = lax.broadcasted_iota(jnp.int32, (ts, ts), 1)
    before = (c_i < r_i).astype(BF16)
    within = each(lambda t: _dot(before, t), hot)
    totals = each(lambda t: jnp.sum(t.astype(F32), axis=0, keepdims=True), hot)
    base = [carry_ref[0:1, :]]
    for t in totals:
        base.append(base[-1] + t)
    seen = each(lambda a, b: a + b, within, base[:-1])
    rank1 = each(lambda i, s: rsum(jnp.where(lane == i, s, 0.0)), i1, seen)
    rank2 = each(lambda i, s: rsum(jnp.where(lane == i, s, 0.0)), i2, seen)
    carry_ref[...] = jnp.broadcast_to(base[-1], carry_ref.shape)
    cnt_ref[...] = jnp.broadcast_to(base[-1], cnt_ref.shape)

    for j, sl in enumerate(subs):
        rec = jnp.zeros((ts, LANES), F32)
        for slot, val in ((R_E1, i1), (R_E2, i2), (R_C1, c1), (R_C2, c2), (R_RANK1, rank1), (R_RANK2, rank2)):
            rec = jnp.where(lane == float(slot), val[j], rec)
        route_ref[sl, :] = rec
        route_t_ref[:, sl] = rec.T[0:8, :]


def _outproj(ya, yb, x2d, wo, fgain, wr, tm):
    T, D = x2d.shape
    Wa, Wb = ya.shape[1], yb.shape[1]
    assert tm % ROUTER_SUBTILE == 0
    return pl.pallas_call(
        _outproj_kernel,
        grid=(T // tm,),
        in_specs=[
            pl.BlockSpec((tm, Wa), lambda i: (i, 0)),
            pl.BlockSpec((tm, Wb), lambda i: (i, 0)),
            pl.BlockSpec((tm, D), lambda i: (i, 0)),
            pl.BlockSpec((Wa + Wb, D), lambda i: (0, 0)),
            pl.BlockSpec((1, D), lambda i: (0, 0)),
            pl.BlockSpec((D, LANES), lambda i: (0, 0)),
        ],
        out_specs=[
            pl.BlockSpec((tm, D), lambda i: (i, 0)),
            pl.BlockSpec((tm, LANES), lambda i: (i, 0)),
            pl.BlockSpec((8, tm), lambda i: (0, i)),
            pl.BlockSpec((8, LANES), lambda i: (0, 0)),
        ],
        out_shape=[
            jax.ShapeDtypeStruct((T, D), F32),
            jax.ShapeDtypeStruct((T, LANES), F32),
            jax.ShapeDtypeStruct((8, T), F32),
            jax.ShapeDtypeStruct((8, LANES), F32),
        ],
        scratch_shapes=[pltpu.VMEM((8, LANES), F32), pltpu.VMEM((Wa + Wb, D), BF16)],
        compiler_params=pltpu.CompilerParams(
            dimension_semantics=("arbitrary",), vmem_limit_bytes=VMEM_LIMIT),
        name="outproj_router",
    )(ya, yb, x2d, wo, fgain, wr)


def _dispatch_kernel(tail_ref, pos_ref, x_ref, xs_ref, zero_ref, sem):
    tm = x_ref.shape[0]
    rt = zero_ref.shape[0]

    @pl.when(pl.program_id(0) == 0)
    def _():
        zero_ref[...] = jnp.zeros_like(zero_ref)

        def tile_fill(j):
            return pltpu.make_async_copy(zero_ref, xs_ref.at[pl.ds(pl.multiple_of(j * rt, 8), rt)], sem)

        def for_each_fill(act):
            for e in range(N_EXPERTS):
                pl.when(tail_ref[e] >= 0)(lambda e=e: act(tile_fill(tail_ref[e])))
            lax.fori_loop(tail_ref[N_EXPERTS], xs_ref.shape[0] // rt, lambda j, c: (act(tile_fill(j)), c)[1], 0)

        for_each_fill(lambda f: f.start())
        for_each_fill(lambda f: f.wait())

    def issue(r, carry):
        for slot in range(2):
            p = pos_ref[0, slot * tm + r]
            pltpu.make_async_copy(x_ref.at[pl.ds(r, 1)], xs_ref.at[pl.ds(p, 1)], sem).start()
        return carry

    lax.fori_loop(0, tm, issue, 0, unroll=True)
    for _ in range(2):
        pltpu.make_async_copy(x_ref, xs_ref.at[pl.ds(0, tm)], sem).wait()


def _dispatch(tails, pos, h, n_rows, tm, rt):
    T, Dp = h.shape
    return pl.pallas_call(
        _dispatch_kernel,
        grid_spec=pltpu.PrefetchScalarGridSpec(
            num_scalar_prefetch=1,
            grid=(T // tm,),
            in_specs=[
                pl.BlockSpec((None, 1, 2 * tm), lambda i, tl: (i, 0, 0), memory_space=pltpu.SMEM),
                pl.BlockSpec((tm, Dp), lambda i, tl: (i, 0)),
            ],
            out_specs=pl.BlockSpec(memory_space=pl.ANY),
            scratch_shapes=[pltpu.VMEM((rt, Dp), h.dtype), pltpu.SemaphoreType.DMA],
        ),
        out_shape=jax.ShapeDtypeStruct((n_rows, Dp), h.dtype),
        compiler_params=pltpu.CompilerParams(
            dimension_semantics=("arbitrary",), vmem_limit_bytes=VMEM_LIMIT),
        name="moe_dispatch",
    )(tails, pos, h)


def _experts_kernel(te_ref, tv_ref, xs_ref, fg_ref, wg_ref, wu_ref, wd_ref, y_ref):
    del te_ref
    i = pl.program_id(0)

    @pl.when(tv_ref[i] > 0)
    def _():
        h = _rms(xs_ref[...], fg_ref[...]).astype(BF16)
        a = _dot(h, wg_ref[...].astype(BF16))
        b = _dot(h, wu_ref[...].astype(BF16))
        y_ref[...] = _dot((_silu(a) * b).astype(BF16), wd_ref[...].astype(BF16))

    @pl.when(tv_ref[i] == 0)
    def _():
        y_ref[...] = jnp.zeros_like(y_ref)


def _experts(tile_expert, tile_valid, xs, fgain, wg, wu, wd, rt):
    n_rows = tile_expert.shape[0] * rt
    Dp = xs.shape[1]
    E, D, F = wg.shape
    return pl.pallas_call(
        _experts_kernel,
        grid_spec=pltpu.PrefetchScalarGridSpec(
            num_scalar_prefetch=2,
            grid=(n_rows // rt,),
            in_specs=[
                pl.BlockSpec((rt, Dp), lambda i, te, tv: (i * tv[i], 0)),
                pl.BlockSpec((1, D), lambda i, te, tv: (0, 0)),
                pl.BlockSpec((None, D, F), lambda i, te, tv: (te[i], 0, 0)),
                pl.BlockSpec((None, D, F), lambda i, te, tv: (te[i], 0, 0)),
                pl.BlockSpec((None, F, D), lambda i, te, tv: (te[i], 0, 0)),
            ],
            out_specs=pl.BlockSpec((rt, D), lambda i, te, tv: (i, 0)),
        ),
        out_shape=jax.ShapeDtypeStruct((n_rows, D), F32),
        compiler_params=pltpu.CompilerParams(
            dimension_semantics=("arbitrary",), vmem_limit_bytes=VMEM_LIMIT),
        name="moe_experts",
    )(tile_expert, tile_valid, xs, fgain, wg, wu, wd)


def _combine_kernel(pos_ref, pos_next_ref, x1_ref, route_ref, fn_ref, y_ref, o_ref, buf_ref, sem):
    tm = x1_ref.shape[0]
    i = pl.program_id(0)
    ring = i % 2

    def row_copy(p_ref, r, slot, rg):
        p = p_ref[0, slot * tm + r]
        return pltpu.make_async_copy(y_ref.at[pl.ds(p, 1)], buf_ref.at[rg, slot, pl.ds(r, 1)], sem.at[rg])

    def issue(p_ref, rg):
        def body(r, carry):
            for slot in range(2):
                row_copy(p_ref, r, slot, rg).start()
            return carry
        lax.fori_loop(0, tm, body, 0, unroll=True)

    @pl.when(i == 0)
    def _():
        issue(pos_ref, ring)

    @pl.when(i + 1 < pl.num_programs(0))
    def _():
        issue(pos_next_ref, 1 - ring)

    for slot in range(2):
        pltpu.make_async_copy(y_ref.at[pl.ds(0, tm)], buf_ref.at[ring, slot], sem.at[ring]).wait()
    route = route_ref[...]
    c1 = route[:, R_C1:R_C1 + 1]
    c2 = route[:, R_C2:R_C2 + 1]
    o_ref[...] = _rms(x1_ref[...] + c1 * buf_ref[ring, 0] + c2 * buf_ref[ring, 1], fn_ref[...])


def _combine(pos, x1, route, fnorm, y, tm):
    T, D = x1.shape
    n = T // tm
    return pl.pallas_call(
        _combine_kernel,
        grid=(n,),
        in_specs=[
            pl.BlockSpec((None, 1, 2 * tm), lambda i: (i, 0, 0), memory_space=pltpu.SMEM),
            pl.BlockSpec((None, 1, 2 * tm), lambda i: (jnp.minimum(i + 1, n - 1), 0, 0),
                         memory_space=pltpu.SMEM),
            pl.BlockSpec((tm, D), lambda i: (i, 0)),
            pl.BlockSpec((tm, LANES), lambda i: (i, 0)),
            pl.BlockSpec((1, D), lambda i: (0, 0)),
            pl.BlockSpec(memory_space=pl.ANY),
        ],
        out_specs=pl.BlockSpec((tm, D), lambda i: (i, 0)),
        out_shape=jax.ShapeDtypeStruct((T, D), F32),
        scratch_shapes=[pltpu.VMEM((2, 2, tm, D), F32), pltpu.SemaphoreType.DMA((2,))],
        compiler_params=pltpu.CompilerParams(
            dimension_semantics=("arbitrary",), vmem_limit_bytes=VMEM_LIMIT),
        name="moe_combine",
    )(pos, pos, x1, route, fnorm, y)


def _pick_tile(T, pref):
    tm = min(pref, T)
    while T % tm:
        tm //= 2
    return tm


def kernel(x, attn_norm, w_in, conv_w, A_log, dt_bias, gdn_norm, w_out, ffn_norm,
           w_group, w_router, w_gate, w_up, w_down, final_norm):
    B, S, D = x.shape
    T = B * S
    depth = w_in.shape[0]
    heads = A_log.shape[1]
    Wb = heads * GDN_HEAD_DIM
    Wa = w_out.shape[1] - Wb
    assert S % MOBA_BLOCK == 0 and S // MOBA_BLOCK <= GATE_ROWS and S % GDN_CHUNK == 0
    assert Wa % LANES == 0 and 2 * heads <= LANES
    assert depth == 1, "the final RMSNorm is fused into the layer's last kernel"

    x2d = x.reshape(T, D)
    for l in range(depth):
        w_all = jnp.pad(w_in[l], ((0, 0), (0, LANES - 2 * heads))).astype(BF16)
        qkv_a, gd, ab = _inproj(x2d, attn_norm[l][None, :], w_all, conv_w[l].astype(F32),
                                3 * Wa, 4 * Wb, _pick_tile(S, 512), S)

        y_a = _moba(qkv_a.reshape(B, S, 3 * Wa), B, S)

        prm = jnp.zeros((8, LANES), F32)
        prm = prm.at[0, :heads].set(A_log[l].astype(F32)).at[1, :heads].set(dt_bias[l].astype(F32))
        y_b = _gdn(gd.reshape(B, S, 4 * Wb), ab.reshape(B, S, LANES), prm,
                   gdn_norm[l][None, :].astype(F32), B, S, heads)

        wr = jnp.concatenate([w_router[l], w_group[l]], axis=1)
        wr = jnp.pad(wr, ((0, 0), (0, LANES - wr.shape[1]))).astype(BF16)
        x1, route, route_t, counts = _outproj(y_a.reshape(T, Wa), y_b.reshape(T, Wb), x2d,
                                              w_out[l].astype(F32), ffn_norm[l][None, :], wr, _pick_tile(T, 1024))

        rt = MOE_ROW_TILE
        n_tiles = -(-2 * T // rt) + N_EXPERTS
        cnt = counts[0, :N_EXPERTS].astype(jnp.int32)
        seg = -(-cnt // rt) * rt
        seg_end = jnp.cumsum(seg)
        seg_start = seg_end - seg
        tile_row = jnp.arange(n_tiles, dtype=jnp.int32) * rt
        tile_expert = jnp.minimum(jnp.sum(tile_row[:, None] >= seg_end[None, :], axis=1), N_EXPERTS - 1)
        tile_valid = (tile_row < seg_end[-1]).astype(jnp.int32)
        e_ids = jnp.arange(N_EXPERTS, dtype=jnp.int32)[:, None]

        def dest_rows(e_lane, rank_lane):
            e = route_t[e_lane].astype(jnp.int32)
            return route_t[rank_lane].astype(jnp.int32) + jnp.sum(
                jnp.where(e[None, :] == e_ids, seg_start[:, None], 0), axis=0)

        pos1, pos2 = dest_rows(R_E1, R_RANK1), dest_rows(R_E2, R_RANK2)

        def tiled_pos(tm):
            return jnp.concatenate([pos1.reshape(T // tm, 1, tm), pos2.reshape(T // tm, 1, tm)], axis=2)

        F = w_gate.shape[-1]
        wge = w_gate[l].reshape(N_EXPERTS, D, F)
        wue = w_up[l].reshape(N_EXPERTS, D, F)
        wde = w_down[l].reshape(N_EXPERTS, F, D)
        tmd = _pick_tile(T, 512)
        tails = jnp.concatenate([jnp.where(seg > 0, seg_end // rt - 1, -1), seg_end[-1:] // rt])
        xs = _dispatch(tails, tiled_pos(tmd), x1, n_tiles * rt, tmd, rt)
        ys = _experts(tile_expert.astype(jnp.int32), tile_valid, xs, ffn_norm[l][None, :],
                      wge, wue, wde, rt)
        tmc = _pick_tile(T, 512)
        x2d = _combine(tiled_pos(tmc), x1, route, final_norm[None, :], ys, tmc)
    return x2d.reshape(B, S, D)
```

```python
import functools

import jax
import jax.numpy as jnp
from jax import lax
from jax.experimental import pallas as pl
from jax.experimental.pallas import tpu as pltpu

F32 = jnp.float32
BF16 = jnp.bfloat16

LANES = 128
MOBA_HEAD_DIM = 64
MOBA_BLOCK = 256
MOBA_TOPK = 3
MOBA_BATCH_ROWS = 2
MOBA_Q_SCALE = MOBA_HEAD_DIM ** -0.5 * 1.4426950408889634
GDN_HEAD_DIM = 128
GDN_CONV = 4
GDN_CHUNK = 256
GDN_BATCH_ROWS = 2
N_GROUPS = 4
EXPERTS_PER_GROUP = 8
N_EXPERTS = N_GROUPS * EXPERTS_PER_GROUP
RMS_EPS = 1e-6
NEG = -1e30
GATE_ROWS = 16
MOE_ROW_TILE = 512
ROUTER_SUBTILE = 128
VMEM_LIMIT = 48 * 1024 * 1024


def _dot(a, b):
    return jnp.dot(a, b, preferred_element_type=F32)


def _dot_nt(a, b):
    return lax.dot_general(a, b, (((1,), (1,)), ((), ())), preferred_element_type=F32)


def _sigmoid(x):
    return 1.0 / (1.0 + jnp.exp(-x))


def _silu(x):
    return x * _sigmoid(x)


def _rms(x, gain):
    return x * lax.rsqrt(jnp.mean(x * x, axis=-1, keepdims=True) + RMS_EPS) * gain


def _inproj_kernel(x_ref, gain_ref, w_ref, cw_ref, om_ref, og_ref, oab_ref, prev_ref, *, tiles_per_seq):
    nm, ng = om_ref.shape[1], og_ref.shape[1]
    tm = x_ref.shape[0]

    @pl.when(pl.program_id(0) % tiles_per_seq == 0)
    def _():
        prev_ref[...] = jnp.zeros_like(prev_ref)

    W = ng // 4
    Dh = GDN_HEAD_DIM
    cw = cw_ref[...]
    half = tm // 2
    h_top = _rms(x_ref[:half, :], gain_ref[...]).astype(BF16)
    h_bot = _rms(x_ref[half:, :], gain_ref[...]).astype(BF16)

    def gdn_tail(rows, og, halo):
        raw = og[:, :3 * W]
        ext = jnp.concatenate([halo, raw], axis=0)
        conv = cw[GDN_CONV - 1:GDN_CONV] * raw
        for d in range(1, GDN_CONV):
            conv = conv + cw[GDN_CONV - 1 - d:GDN_CONV - d] * pltpu.roll(ext, d, 0)[8:]
        qkv = _silu(conv)
        for j in range(2 * W // Dh):
            t = qkv[:, j * Dh:(j + 1) * Dh]
            t = t * lax.rsqrt(jnp.sum(t * t, axis=-1, keepdims=True) + 1e-6)
            og_ref[rows, j * Dh:(j + 1) * Dh] = t * (Dh ** -0.5) if j < W // Dh else t
        og_ref[rows, 2 * W:3 * W] = qkv[:, 2 * W:]
        og_ref[rows, 3 * W:] = og[:, 3 * W:]
        return raw[half - 8:, :]

    og_top = _dot(h_top, w_ref[:, nm:nm + ng])
    og_bot = _dot(h_bot, w_ref[:, nm:nm + ng])
    last = gdn_tail(slice(0, half), og_top, prev_ref[...])
    h = jnp.concatenate([h_top, h_bot], axis=0)
    om = _dot(h, w_ref[:, :nm])
    nq = nm // 3
    om_ref[:, :nq] = (om[:, :nq] * MOBA_Q_SCALE).astype(BF16)
    om_ref[:, nq:] = om[:, nq:].astype(BF16)
    prev_ref[...] = gdn_tail(slice(half, tm), og_bot, last)
    oab_ref[...] = _dot(h, w_ref[:, nm + ng:])


def _inproj(x2d, gain, w_all, conv_w, nm, ng, tm, seq_len):
    T, D = x2d.shape
    assert w_all.shape[1] == nm + ng + LANES and nm % LANES == 0 and ng % LANES == 0
    assert seq_len % tm == 0 and tm >= 16
    return pl.pallas_call(
        functools.partial(_inproj_kernel, tiles_per_seq=seq_len // tm),
        grid=(T // tm,),
        in_specs=[
            pl.BlockSpec((tm, D), lambda i: (i, 0)),
            pl.BlockSpec((1, D), lambda i: (0, 0)),
            pl.BlockSpec((D, nm + ng + LANES), lambda i: (0, 0)),
            pl.BlockSpec((GDN_CONV, 3 * ng // 4), lambda i: (0, 0)),
        ],
        out_specs=[
            pl.BlockSpec((tm, nm), lambda i: (i, 0)),
            pl.BlockSpec((tm, ng), lambda i: (i, 0)),
            pl.BlockSpec((tm, LANES), lambda i: (i, 0)),
        ],
        out_shape=[
            jax.ShapeDtypeStruct((T, nm), BF16),
            jax.ShapeDtypeStruct((T, ng), F32),
            jax.ShapeDtypeStruct((T, LANES), F32),
        ],
        scratch_shapes=[pltpu.VMEM((8, 3 * ng // 4), F32)],
        compiler_params=pltpu.CompilerParams(
            dimension_semantics=("arbitrary",), vmem_limit_bytes=VMEM_LIMIT),
        name="inproj",
    )(x2d, gain, w_all, conv_w)


def _moba_kernel(q_ref, k_ref, v_ref, o_ref, *, S):
    nb = S // MOBA_BLOCK
    L = MOBA_BLOCK
    lane = lax.broadcasted_iota(jnp.int32, (1, LANES), 1)
    key_blk = lax.broadcasted_iota(jnp.int32, (S, LANES), 0) // L
    lane_s = lax.broadcasted_iota(jnp.int32, (S, LANES), 1)
    qblk = lax.broadcasted_iota(jnp.int32, (GATE_ROWS, S), 1) // L
    cidx = lax.broadcasted_iota(jnp.int32, (GATE_ROWS, S), 0)
    r_i = lax.broadcasted_iota(jnp.int32, (L, L), 0)
    c_i = lax.broadcasted_iota(jnp.int32, (L, L), 1)
    causal = c_i <= r_i
    hms = [(lane >= MOBA_HEAD_DIM * h) & (lane < MOBA_HEAD_DIM * (h + 1)) for h in range(2)]

    chains = [(bi, h) for bi in range(q_ref.shape[0]) for h in range(2)]
    qa, ka, va = [], [], []
    for bi, h in chains:
        q, k, v = q_ref[bi], k_ref[bi], v_ref[bi]
        hm = hms[h]
        off = MOBA_HEAD_DIM * (1 - h)
        km = jnp.mean(k.astype(F32).reshape(nb, L, LANES), axis=1)
        km = jnp.concatenate([km, jnp.zeros((GATE_ROWS - nb, LANES), F32)], axis=0).astype(BF16)
        qh = jnp.where(hm, q, jnp.zeros_like(q))
        gate = _dot_nt(km, qh)
        valid = cidx < qblk
        gate = jnp.where(valid, gate, -jnp.inf)
        rank = jnp.zeros((GATE_ROWS, S), F32)
        for i in range(nb):
            gi = gate[i:i + 1, :]
            beats = (gi > gate) | ((gi == gate) & (i < cidx))
            rank = rank + beats.astype(F32)
        sel = valid & (rank < float(MOBA_TOPK))
        pen = jnp.where(sel | (cidx >= qblk), 0.0, NEG)
        pads = [jnp.zeros((r, S), F32) for r in (off, LANES - off - GATE_ROWS)]
        pen = jnp.concatenate(([pads[0]] if off else []) + [pen, pads[1]], axis=0)
        pen_q = pen.T.astype(BF16)
        qa.append(jnp.where(hm, q, pen_q))
        onehot = ((lane_s - off) == key_blk).astype(BF16)
        ka.append(jnp.where(hm, k, onehot))
        va.append(jnp.where(hm, v, jnp.ones_like(v)))

    HH = range(len(chains))
    for n in range(nb):
        s = [_dot_nt(qa[h][n * L:(n + 1) * L], ka[h][:(n + 1) * L]) for h in HH]
        s_own = [jnp.where(causal, s[h][:, n * L:], NEG) for h in HH]
        m = [jnp.max(s_own[h], axis=-1, keepdims=True) for h in HH]
        if n > 0:
            m = [jnp.maximum(m[h], jnp.max(s[h][:, :n * L], axis=-1, keepdims=True)) for h in HH]
            p = [jnp.concatenate([jnp.exp2(s[h][:, :n * L] - m[h]), jnp.exp2(s_own[h] - m[h])],
                                 axis=1).astype(BF16) for h in HH]
        else:
            p = [jnp.exp2(s_own[h] - m[h]).astype(BF16) for h in HH]
        acc = [_dot(p[h], va[h][:(n + 1) * L]) for h in HH]
        outs = [acc[h] / pltpu.roll(acc[h], MOBA_HEAD_DIM, 1) for h in HH]
        for bi in range(q_ref.shape[0]):
            o_ref[bi, n * L:(n + 1) * L, :] = jnp.where(hms[0], outs[2 * bi], outs[2 * bi + 1]).astype(BF16)


def _moba(qkv, B, S):
    W = qkv.shape[-1] // 3
    npair = W // LANES
    nbr = MOBA_BATCH_ROWS if B % MOBA_BATCH_ROWS == 0 else 1
    return pl.pallas_call(
        functools.partial(_moba_kernel, S=S),
        grid=(B // nbr, npair),
        in_specs=[
            pl.BlockSpec((nbr, S, LANES), lambda b, j: (b, 0, j)),
            pl.BlockSpec((nbr, S, LANES), lambda b, j: (b, 0, npair + j)),
            pl.BlockSpec((nbr, S, LANES), lambda b, j: (b, 0, 2 * npair + j)),
        ],
        out_specs=pl.BlockSpec((nbr, S, LANES), lambda b, j: (b, 0, j)),
        out_shape=jax.ShapeDtypeStruct((B, S, W), BF16),
        compiler_params=pltpu.CompilerParams(
            dimension_semantics=("arbitrary", "arbitrary"), vmem_limit_bytes=VMEM_LIMIT),
        name="moba",
    )(qkv, qkv, qkv)


def _gdn_kernel(gd_ref, ab_ref, prm_ref, gn_ref, o_ref, state_ref, *, heads):
    C = GDN_CHUNK
    Dh = GDN_HEAD_DIM
    W = heads * Dh
    c = pl.program_id(1)

    @pl.when(c == 0)
    def _():
        state_ref[...] = jnp.zeros_like(state_ref)

    nbatch = gd_ref.shape[0]
    prm = prm_ref[...]
    rows = lax.broadcasted_iota(jnp.int32, (C, LANES), 0)
    gds, Gs, GTs, betas = [], [], [], []
    for bi in range(nbatch):
        gds.append(gd_ref[bi])
        ab = ab_ref[bi]
        sp_in = ab + prm[1:2]
        softplus = jnp.maximum(sp_in, 0.0) + jnp.log1p(jnp.exp(-jnp.abs(sp_in)))
        G = -jnp.exp(prm[0:1]) * softplus
        betas.append(_sigmoid(ab))
        d = 1
        while d < C:
            G = G + jnp.where(rows >= d, pltpu.roll(G, d, 0), 0.0)
            d *= 2
        Gs.append(G)
        GTs.append(G.T)

    r_i = lax.broadcasted_iota(jnp.int32, (C, C), 0)
    c_i = lax.broadcasted_iota(jnp.int32, (C, C), 1)
    incl = c_i <= r_i
    strict = c_i < r_i
    rc = r_i ^ c_i
    level = jnp.full((C, C), -1, jnp.int32)
    for b in range(C.bit_length() - 1):
        level = level + (rc >= (1 << b)).astype(jnp.int32)
    level_b = level.astype(F32).astype(BF16)
    gn = gn_ref[...]

    chains = [(bi, hd) for bi in range(nbatch) for hd in range(heads)]
    H = range(len(chains))
    q = [gds[bi][:, hd * Dh:(hd + 1) * Dh] for bi, hd in chains]
    k = [gds[bi][:, W + hd * Dh:W + (hd + 1) * Dh] for bi, hd in chains]
    v = [gds[bi][:, 2 * W + hd * Dh:2 * W + (hd + 1) * Dh] for bi, hd in chains]
    Gc = [Gs[bi][:, hd:hd + 1] for bi, hd in chains]
    Gr = [GTs[bi][hd:hd + 1, :] for bi, hd in chains]
    bc = [betas[bi][:, heads + hd:heads + hd + 1] for bi, hd in chains]
    gl = [Gs[bi][C - 1:C, hd:hd + 1] for bi, hd in chains]

    kb = [t.astype(BF16) for t in k]
    decay = [jnp.exp(jnp.where(incl, Gc[h] - Gr[h], -jnp.inf)) for h in H]
    A = [jnp.where(strict, bc[h] * _dot_nt(kb[h], kb[h]) * decay[h], 0.0) for h in H]
    qk = [(_dot_nt(q[h].astype(BF16), kb[h]) * decay[h]).astype(BF16) for h in H]
    gam = [jnp.exp(t) for t in Gc]
    X = [jnp.concatenate([bc[h] * v[h], (bc[h] * gam[h]) * k[h]], axis=1) for h in H]
    Ab = [t.astype(BF16) for t in A]
    zero = jnp.zeros((C, C), BF16)
    Tm = [jnp.where(level_b == -1.0, jnp.ones((C, C), BF16), jnp.where(level_b == 0.0, -t, zero))
          for t in Ab]
    for lv in range(1, C.bit_length() - 2):
        at_lv = level_b == float(lv)
        E = [jnp.where(at_lv, t, zero) for t in Ab]
        F = [_dot(E[h], Tm[h]).astype(BF16) for h in H]
        Tm = [Tm[h] - _dot(Tm[h], F[h]).astype(BF16) for h in H]
    hc = C // 2
    F21 = [_dot(Ab[h][hc:, :hc], Tm[h][:hc, :hc]).astype(BF16) for h in H]
    T21 = [-_dot(Tm[h][hc:, hc:], F21[h]).astype(BF16) for h in H]
    Tm = [jnp.concatenate([Tm[h][:hc, :], jnp.concatenate([T21[h], Tm[h][hc:, hc:]], axis=1)], axis=0)
          for h in H]
    X = [X[h] + _dot(jnp.where(level_b == -1.0, zero, Tm[h]), X[h].astype(BF16)) for h in H]

    S0 = [state_ref[h] for h in H]
    Sb = [t.astype(BF16) for t in S0]
    ub = [(X[h][:, :Dh] - _dot(X[h][:, Dh:].astype(BF16), Sb[h])).astype(BF16) for h in H]
    o = [_dot((q[h] * gam[h]).astype(BF16), Sb[h]) + _dot(qk[h], ub[h]) for h in H]
    k_dec = [(k[h] * jnp.exp(gl[h] - Gc[h])).T.astype(BF16) for h in H]
    for h in H:
        state_ref[h] = jnp.exp(gl[h]) * S0[h] + _dot(k_dec[h], ub[h])
    for h, (bi, hd) in enumerate(chains):
        z = gds[bi][:, 3 * W + hd * Dh:3 * W + (hd + 1) * Dh]
        o_ref[bi, :, hd * Dh:(hd + 1) * Dh] = (_rms(o[h], gn) * _silu(z)).astype(BF16)


def _gdn(gd, ab, prm, gdn_norm, B, S, heads):
    C = GDN_CHUNK
    W = heads * GDN_HEAD_DIM
    nbatch = GDN_BATCH_ROWS if B % GDN_BATCH_ROWS == 0 else 1
    return pl.pallas_call(
        functools.partial(_gdn_kernel, heads=heads),
        grid=(B // nbatch, S // C),
        in_specs=[
            pl.BlockSpec((nbatch, C, 4 * W), lambda b, c: (b, c, 0)),
            pl.BlockSpec((nbatch, C, LANES), lambda b, c: (b, c, 0)),
            pl.BlockSpec((8, LANES), lambda b, c: (0, 0)),
            pl.BlockSpec((1, GDN_HEAD_DIM), lambda b, c: (0, 0)),
        ],
        out_specs=pl.BlockSpec((nbatch, C, W), lambda b, c: (b, c, 0)),
        out_shape=jax.ShapeDtypeStruct((B, S, W), BF16),
        scratch_shapes=[pltpu.VMEM((nbatch * heads, GDN_HEAD_DIM, GDN_HEAD_DIM), F32)],
        compiler_params=pltpu.CompilerParams(
            dimension_semantics=("arbitrary", "arbitrary"), vmem_limit_bytes=VMEM_LIMIT),
        name="gdn",
    )(gd, ab, prm, gdn_norm)


R_E1, R_E2, R_C1, R_C2, R_RANK1, R_RANK2 = range(6)


def _outproj_kernel(ya_ref, yb_ref, x_ref, wo_ref, fg_ref, wr_ref,
                    x1_ref, route_ref, route_t_ref, cnt_ref, carry_ref, wo16_ref):
    @pl.when(pl.program_id(0) == 0)
    def _():
        carry_ref[...] = jnp.zeros_like(carry_ref)
        wo16_ref[...] = wo_ref[...].astype(BF16)

    wa = ya_ref.shape[1]
    ts = ROUTER_SUBTILE
    subs = [pl.ds(j * ts, ts) for j in range(x_ref.shape[0] // ts)]
    each = lambda f, *lists: [f(*args) for args in zip(*lists)]
    rmax = lambda t: jnp.max(t, axis=-1, keepdims=True)
    rmin = lambda t: jnp.min(t, axis=-1, keepdims=True)
    rsum = lambda t: jnp.sum(t, axis=-1, keepdims=True)

    x1 = [x_ref[sl, :] + _dot(ya_ref[sl, :], wo16_ref[:wa, :]) + _dot(yb_ref[sl, :], wo16_ref[wa:, :])
          for sl in subs]
    for sl, t in zip(subs, x1):
        x1_ref[sl, :] = t
    h = each(lambda t: _rms(t, fg_ref[...]).astype(BF16), x1)
    logits = each(lambda t: _dot(t, wr_ref[...]), h)
    lane = lax.broadcasted_iota(jnp.int32, (ts, LANES), 1).astype(F32)
    big = float(LANES)

    is_g = (lane >= N_EXPERTS) & (lane < N_EXPERTS + N_GROUPS)
    lg = each(lambda t: jnp.where(is_g, t, -jnp.inf), logits)
    mg = each(rmax, lg)
    g_sel = each(lambda a, b: rmin(jnp.where(a == b, lane, big)) - N_EXPERTS, lg, mg)
    p_group = each(lambda a, b: 1.0 / rsum(jnp.exp(a - b)), lg, mg)

    le = each(lambda t, g: jnp.where((lane >= g * EXPERTS_PER_GROUP) & (lane < (g + 1) * EXPERTS_PER_GROUP),
                                     t, -jnp.inf), logits, g_sel)
    m1 = each(rmax, le)
    i1 = each(lambda a, b: rmin(jnp.where(a == b, lane, big)), le, m1)
    le2 = each(lambda a, i: jnp.where(lane == i, -jnp.inf, a), le, i1)
    m2 = each(rmax, le2)
    i2 = each(lambda a, b: rmin(jnp.where(a == b, lane, big)), le2, m2)
    se = each(lambda a, b: rsum(jnp.exp(a - b)), le, m1)
    p1 = each(lambda s: 1.0 / s, se)
    p2 = each(lambda a, b, s: jnp.exp(a - b) / s, m2, m1, se)
    c1 = each(lambda pg, a, b: pg * (a / (a + b)), p_group, p1, p2)
    c2 = each(lambda pg, a, b: pg * (b / (a + b)), p_group, p1, p2)

    hot = each(lambda a, b: ((lane == a) | (lane == b)).astype(BF16), i1, i2)
    r_i = lax.broadcasted_iota(jnp.int32, (ts, ts), 0)
    c_i = lax.broadcasted_iota(jnp.int32, (ts, ts), 1)
    before = (c_i < r_i).astype(BF16)
    within = each(lambda t: _dot(before, t), hot)
    totals = each(lambda t: jnp.sum(t.astype(F32), axis=0, keepdims=True), hot)
    base = [carry_ref[0:1, :]]
    for t in totals:
        base.append(base[-1] + t)
    seen = each(lambda a, b: a + b, within, base[:-1])
    rank1 = each(lambda i, s: rsum(jnp.where(lane == i, s, 0.0)), i1, seen)
    rank2 = each(lambda i, s: rsum(jnp.where(lane == i, s, 0.0)), i2, seen)
    carry_ref[...] = jnp.broadcast_to(base[-1], carry_ref.shape)
    cnt_ref[...] = jnp.broadcast_to(base[-1], cnt_ref.shape)

    for j, sl in enumerate(subs):
        rec = jnp.zeros((ts, LANES), F32)
        for slot, val in ((R_E1, i1), (R_E2, i2), (R_C1, c1), (R_C2, c2), (R_RANK1, rank1), (R_RANK2, rank2)):
            rec = jnp.where(lane == float(slot), val[j], rec)
        route_ref[sl, :] = rec
        route_t_ref[:, sl] = rec.T[0:8, :]


def _outproj(ya, yb, x2d, wo, fgain, wr, tm):
    T, D = x2d.shape
    Wa, Wb = ya.shape[1], yb.shape[1]
    assert tm % ROUTER_SUBTILE == 0
    return pl.pallas_call(
        _outproj_kernel,
        grid=(T // tm,),
        in_specs=[
            pl.BlockSpec((tm, Wa), lambda i: (i, 0)),
            pl.BlockSpec((tm, Wb), lambda i: (i, 0)),
            pl.BlockSpec((tm, D), lambda i: (i, 0)),
            pl.BlockSpec((Wa + Wb, D), lambda i: (0, 0)),
            pl.BlockSpec((1, D), lambda i: (0, 0)),
            pl.BlockSpec((D, LANES), lambda i: (0, 0)),
        ],
        out_specs=[
            pl.BlockSpec((tm, D), lambda i: (i, 0)),
            pl.BlockSpec((tm, LANES), lambda i: (i, 0)),
            pl.BlockSpec((8, tm), lambda i: (0, i)),
            pl.BlockSpec((8, LANES), lambda i: (0, 0)),
        ],
        out_shape=[
            jax.ShapeDtypeStruct((T, D), F32),
            jax.ShapeDtypeStruct((T, LANES), F32),
            jax.ShapeDtypeStruct((8, T), F32),
            jax.ShapeDtypeStruct((8, LANES), F32),
        ],
        scratch_shapes=[pltpu.VMEM((8, LANES), F32), pltpu.VMEM((Wa + Wb, D), BF16)],
        compiler_params=pltpu.CompilerParams(
            dimension_semantics=("arbitrary",), vmem_limit_bytes=VMEM_LIMIT),
        name="outproj_router",
    )(ya, yb, x2d, wo, fgain, wr)


def _dispatch_kernel(tail_ref, pos_ref, x_ref, xs_ref, zero_ref, sem):
    tm = x_ref.shape[0]
    rt = zero_ref.shape[0]

    @pl.when(pl.program_id(0) == 0)
    def _():
        zero_ref[...] = jnp.zeros_like(zero_ref)

        def tile_fill(j):
            return pltpu.make_async_copy(zero_ref, xs_ref.at[pl.ds(pl.multiple_of(j * rt, 8), rt)], sem)

        def for_each_fill(act):
            for e in range(N_EXPERTS):
                pl.when(tail_ref[e] >= 0)(lambda e=e: act(tile_fill(tail_ref[e])))
            lax.fori_loop(tail_ref[N_EXPERTS], xs_ref.shape[0] // rt, lambda j, c: (act(tile_fill(j)), c)[1], 0)

        for_each_fill(lambda f: f.start())
        for_each_fill(lambda f: f.wait())

    def issue(r, carry):
        for slot in range(2):
            p = pos_ref[0, slot * tm + r]
            pltpu.make_async_copy(x_ref.at[pl.ds(r, 1)], xs_ref.at[pl.ds(p, 1)], sem).start(priority=slot)
        return carry

    lax.fori_loop(0, tm, issue, 0, unroll=True)
    for _ in range(2):
        pltpu.make_async_copy(x_ref, xs_ref.at[pl.ds(0, tm)], sem).wait()


def _dispatch(tails, pos, h, n_rows, tm, rt):
    T, Dp = h.shape
    return pl.pallas_call(
        _dispatch_kernel,
        grid_spec=pltpu.PrefetchScalarGridSpec(
            num_scalar_prefetch=1,
            grid=(T // tm,),
            in_specs=[
                pl.BlockSpec((None, 1, 2 * tm), lambda i, tl: (i, 0, 0), memory_space=pltpu.SMEM),
                pl.BlockSpec((tm, Dp), lambda i, tl: (i, 0)),
            ],
            out_specs=pl.BlockSpec(memory_space=pl.ANY),
            scratch_shapes=[pltpu.VMEM((rt, Dp), h.dtype), pltpu.SemaphoreType.DMA],
        ),
        out_shape=jax.ShapeDtypeStruct((n_rows, Dp), h.dtype),
        compiler_params=pltpu.CompilerParams(
            dimension_semantics=("arbitrary",), vmem_limit_bytes=VMEM_LIMIT),
        name="moe_dispatch",
    )(tails, pos, h)


def _experts_kernel(te_ref, tv_ref, xs_ref, fg_ref, wg_ref, wu_ref, wd_ref, y_ref):
    del te_ref
    i = pl.program_id(0)

    @pl.when(tv_ref[i] > 0)
    def _():
        h = _rms(xs_ref[...], fg_ref[...]).astype(BF16)
        a = _dot(h, wg_ref[...].astype(BF16))
        b = _dot(h, wu_ref[...].astype(BF16))
        y_ref[...] = _dot((_silu(a) * b).astype(BF16), wd_ref[...].astype(BF16))

    @pl.when(tv_ref[i] == 0)
    def _():
        y_ref[...] = jnp.zeros_like(y_ref)


def _experts(tile_expert, tile_valid, xs, fgain, wg, wu, wd, rt):
    n_rows = tile_expert.shape[0] * rt
    Dp = xs.shape[1]
    E, D, F = wg.shape
    return pl.pallas_call(
        _experts_kernel,
        grid_spec=pltpu.PrefetchScalarGridSpec(
            num_scalar_prefetch=2,
            grid=(n_rows // rt,),
            in_specs=[
                pl.BlockSpec((rt, Dp), lambda i, te, tv: (i * tv[i], 0)),
                pl.BlockSpec((1, D), lambda i, te, tv: (0, 0)),
                pl.BlockSpec((None, D, F), lambda i, te, tv: (te[i], 0, 0)),
                pl.BlockSpec((None, D, F), lambda i, te, tv: (te[i], 0, 0)),
                pl.BlockSpec((None, F, D), lambda i, te, tv: (te[i], 0, 0)),
            ],
            out_specs=pl.BlockSpec((rt, D), lambda i, te, tv: (i, 0)),
        ),
        out_shape=jax.ShapeDtypeStruct((n_rows, D), F32),
        compiler_params=pltpu.CompilerParams(
            dimension_semantics=("arbitrary",), vmem_limit_bytes=VMEM_LIMIT),
        name="moe_experts",
    )(tile_expert, tile_valid, xs, fgain, wg, wu, wd)


def _combine_kernel(pos_ref, pos_next_ref, x1_ref, route_ref, fn_ref, y_ref, o_ref, buf_ref, sem):
    tm = x1_ref.shape[0]
    i = pl.program_id(0)
    ring = i % 2

    def row_copy(p_ref, r, slot, rg):
        p = p_ref[0, slot * tm + r]
        return pltpu.make_async_copy(y_ref.at[pl.ds(p, 1)], buf_ref.at[rg, slot, pl.ds(r, 1)], sem.at[rg])

    def issue(p_ref, rg):
        def body(r, carry):
            for slot in range(2):
                row_copy(p_ref, r, slot, rg).start()
            return carry
        lax.fori_loop(0, tm, body, 0, unroll=True)

    @pl.when(i == 0)
    def _():
        issue(pos_ref, ring)

    @pl.when(i + 1 < pl.num_programs(0))
    def _():
        issue(pos_next_ref, 1 - ring)

    for slot in range(2):
        pltpu.make_async_copy(y_ref.at[pl.ds(0, tm)], buf_ref.at[ring, slot], sem.at[ring]).wait()
    route = route_ref[...]
    c1 = route[:, R_C1:R_C1 + 1]
    c2 = route[:, R_C2:R_C2 + 1]
    o_ref[...] = _rms(x1_ref[...] + c1 * buf_ref[ring, 0] + c2 * buf_ref[ring, 1], fn_ref[...])


def _combine(pos, x1, route, fnorm, y, tm):
    T, D = x1.shape
    n = T // tm
    return pl.pallas_call(
        _combine_kernel,
        grid=(n,),
        in_specs=[
            pl.BlockSpec((None, 1, 2 * tm), lambda i: (i, 0, 0), memory_space=pltpu.SMEM),
            pl.BlockSpec((None, 1, 2 * tm), lambda i: (jnp.minimum(i + 1, n - 1), 0, 0),
                         memory_space=pltpu.SMEM),
            pl.BlockSpec((tm, D), lambda i: (i, 0)),
            pl.BlockSpec((tm, LANES), lambda i: (i, 0)),
            pl.BlockSpec((1, D), lambda i: (0, 0)),
            pl.BlockSpec(memory_space=pl.ANY),
        ],
        out_specs=pl.BlockSpec((tm, D), lambda i: (i, 0)),
        out_shape=jax.ShapeDtypeStruct((T, D), F32),
        scratch_shapes=[pltpu.VMEM((2, 2, tm, D), F32), pltpu.SemaphoreType.DMA((2,))],
        compiler_params=pltpu.CompilerParams(
            dimension_semantics=("arbitrary",), vmem_limit_bytes=VMEM_LIMIT),
        name="moe_combine",
    )(pos, pos, x1, route, fnorm, y)


def _pick_tile(T, pref):
    tm = min(pref, T)
    while T % tm:
        tm //= 2
    return tm


def kernel(x, attn_norm, w_in, conv_w, A_log, dt_bias, gdn_norm, w_out, ffn_norm,
           w_group, w_router, w_gate, w_up, w_down, final_norm):
    B, S, D = x.shape
    T = B * S
    depth = w_in.shape[0]
    heads = A_log.shape[1]
    Wb = heads * GDN_HEAD_DIM
    Wa = w_out.shape[1] - Wb
    assert S % MOBA_BLOCK == 0 and S // MOBA_BLOCK <= GATE_ROWS and S % GDN_CHUNK == 0
    assert Wa % LANES == 0 and 2 * heads <= LANES
    assert depth == 1, "the final RMSNorm is fused into the layer's last kernel"

    x2d = x.reshape(T, D)
    for l in range(depth):
        w_all = jnp.pad(w_in[l], ((0, 0), (0, LANES - 2 * heads))).astype(BF16)
        qkv_a, gd, ab = _inproj(x2d, attn_norm[l][None, :], w_all, conv_w[l].astype(F32),
                                3 * Wa, 4 * Wb, _pick_tile(S, 512), S)

        y_a = _moba(qkv_a.reshape(B, S, 3 * Wa), B, S)

        prm = jnp.zeros((8, LANES), F32)
        prm = prm.at[0, :heads].set(A_log[l].astype(F32)).at[1, :heads].set(dt_bias[l].astype(F32))
        y_b = _gdn(gd.reshape(B, S, 4 * Wb), ab.reshape(B, S, LANES), prm,
                   gdn_norm[l][None, :].astype(F32), B, S, heads)

        wr = jnp.concatenate([w_router[l], w_group[l]], axis=1)
        wr = jnp.pad(wr, ((0, 0), (0, LANES - wr.shape[1]))).astype(BF16)
        x1, route, route_t, counts = _outproj(y_a.reshape(T, Wa), y_b.reshape(T, Wb), x2d,
                                              w_out[l].astype(F32), ffn_norm[l][None, :], wr, _pick_tile(T, 1024))

        rt = MOE_ROW_TILE
        n_tiles = -(-2 * T // rt) + N_EXPERTS
        cnt = counts[0, :N_EXPERTS].astype(jnp.int32)
        seg = -(-cnt // rt) * rt
        seg_end = jnp.cumsum(seg)
        seg_start = seg_end - seg
        tile_row = jnp.arange(n_tiles, dtype=jnp.int32) * rt
        tile_expert = jnp.minimum(jnp.sum(tile_row[:, None] >= seg_end[None, :], axis=1), N_EXPERTS - 1)
        tile_valid = (tile_row < seg_end[-1]).astype(jnp.int32)
        e_ids = jnp.arange(N_EXPERTS, dtype=jnp.int32)[:, None]

        def dest_rows(e_lane, rank_lane):
            e = route_t[e_lane].astype(jnp.int32)
            return route_t[rank_lane].astype(jnp.int32) + jnp.sum(
                jnp.where(e[None, :] == e_ids, seg_start[:, None], 0), axis=0)

        pos1, pos2 = dest_rows(R_E1, R_RANK1), dest_rows(R_E2, R_RANK2)

        def tiled_pos(tm):
            return jnp.concatenate([pos1.reshape(T // tm, 1, tm), pos2.reshape(T // tm, 1, tm)], axis=2)

        F = w_gate.shape[-1]
        wge = w_gate[l].reshape(N_EXPERTS, D, F)
        wue = w_up[l].reshape(N_EXPERTS, D, F)
        wde = w_down[l].reshape(N_EXPERTS, F, D)
        tmd = _pick_tile(T, 512)
        tails = jnp.concatenate([jnp.where(seg > 0, seg_end // rt - 1, -1), seg_end[-1:] // rt])
        xs = _dispatch(tails, tiled_pos(tmd), x1, n_tiles * rt, tmd, rt)
        ys = _experts(tile_expert.astype(jnp.int32), tile_valid, xs, ffn_norm[l][None, :],
                      wge, wue, wde, rt)
        tmc = _pick_tile(T, 512)
        x2d = _combine(tiled_pos(tmc), x1, route, final_norm[None, :], ys, tmc)
    return x2d.reshape(B, S, D)
```
